```python
import math
import jax, jax.numpy as jnp
from jax import lax
import numpy as np

D_MODEL = 4096
BATCH = 1
SEQ = 8192
DEPTH = 1

SSD_HEAD_DIM = 64
SSD_WIDTH = D_MODEL
SSD_HEADS = SSD_WIDTH // SSD_HEAD_DIM
SSD_GROUPS = 8
SSD_STATE = 128
SSD_CONV = 4
SSD_CHUNK = 128
SSD_CONV_DIM = SSD_WIDTH + 2 * SSD_GROUPS * SSD_STATE
DT_MIN = 0.001
DT_MAX = 0.1

GMLP_HEAD_DIM = 256
GMLP_WIDTH = D_MODEL
GMLP_HEADS = GMLP_WIDTH // GMLP_HEAD_DIM
GMLP_CHUNK = 128

MIX_WIDTH = SSD_WIDTH + GMLP_WIDTH
IN_PROJ_DIM = SSD_WIDTH + SSD_CONV_DIM + SSD_HEADS + 2 * GMLP_WIDTH

N_EXPERTS = 32
TOP_K = 4
EXPERT_FF = D_MODEL // 2
SWIGLU_ALPHA = 1.702
SWIGLU_LIMIT = 7.0
MOE_BLOCK = 128

EPS = 1e-5

kernel_name = "hymba_ssd_gmlp_moe_block"


def rms_norm(x, g):
    xf = x.astype(jnp.float32)
    y = xf * lax.rsqrt(jnp.mean(xf * xf, axis=-1, keepdims=True) + EPS)
    return (y * g.astype(jnp.float32)).astype(x.dtype)


def causal_depthwise_conv(x, w, b):
    k, c = w.shape
    out = lax.conv_general_dilated(
        x, w[:, None, :].astype(x.dtype), window_strides=(1,), padding=[(k - 1, 0)],
        dimension_numbers=('NWC', 'WIO', 'NWC'), feature_group_count=c)
    return out + b.astype(x.dtype)


def ssd_mixer(z, xbc, dt, conv_w, conv_b, dt_bias, a_log, d_skip, norm_g):
    bsz, t_len, _ = xbc.shape
    nc = t_len // SSD_CHUNK
    hpg = SSD_HEADS // SSD_GROUPS
    gn = SSD_GROUPS * SSD_STATE
    xbc = jax.nn.silu(causal_depthwise_conv(xbc, conv_w, conv_b))
    xs, bm, cm = jnp.split(xbc, [SSD_WIDTH, SSD_WIDTH + gn], axis=-1)
    xs = xs.reshape(bsz, nc, SSD_CHUNK, SSD_GROUPS, hpg, SSD_HEAD_DIM)
    bm = bm.reshape(bsz, nc, SSD_CHUNK, SSD_GROUPS, SSD_STATE)
    cm = cm.reshape(bsz, nc, SSD_CHUNK, SSD_GROUPS, SSD_STATE)
    dt = jax.nn.softplus(dt.astype(jnp.float32) + dt_bias.astype(jnp.float32))
    a = -jnp.exp(a_log.astype(jnp.float32))
    dt = dt.reshape(bsz, nc, SSD_CHUNK, SSD_GROUPS, hpg)
    a_dt = (dt * a.reshape(SSD_GROUPS, hpg)).transpose(0, 1, 3, 4, 2)
    a_cum = jnp.cumsum(a_dt, axis=-1)
    xdt = xs * dt[..., None]
    causal = jnp.tril(jnp.ones((SSD_CHUNK, SSD_CHUNK), dtype=bool))
    seg = a_cum[..., :, None] - a_cum[..., None, :]
    lmat = jnp.exp(jnp.where(causal, seg, -jnp.inf))
    cb = jnp.einsum('bclgn,bcsgn->bcgls', cm, bm)
    y_diag = jnp.einsum('bcgls,bcghls,bcsghp->bclghp', cb, lmat, xdt)
    decay_states = jnp.exp(a_cum[..., -1:] - a_cum)
    states = jnp.einsum('bclgn,bcghl,bclghp->bcghpn', bm, decay_states, xdt)
    chunk_decay = jnp.exp(a_cum[..., -1])

    def step(state, inp):
        s_c, d_c = inp
        return state * d_c[..., None, None] + s_c, state

    init = jnp.zeros((bsz, SSD_GROUPS, hpg, SSD_HEAD_DIM, SSD_STATE), states.dtype)
    _, prev_states = lax.scan(step, init, (jnp.moveaxis(states, 1, 0), jnp.moveaxis(chunk_decay, 1, 0)))
    prev_states = jnp.moveaxis(prev_states, 0, 1)
    y_off = jnp.einsum('bclgn,bcghpn,bcghl->bclghp', cm, prev_states, jnp.exp(a_cum))
    y = y_diag + y_off + xs * d_skip.reshape(SSD_GROUPS, hpg)[:, :, None]
    y = y.reshape(bsz, t_len, SSD_WIDTH).astype(jnp.float32)
    yg = (y * jax.nn.silu(z.astype(jnp.float32))).reshape(bsz, t_len, SSD_GROUPS, SSD_WIDTH // SSD_GROUPS)
    yg = yg * lax.rsqrt(jnp.mean(yg * yg, axis=-1, keepdims=True) + EPS)
    return (yg.reshape(bsz, t_len, SSD_WIDTH) * norm_g.astype(jnp.float32)).astype(z.dtype)


def gmlp_mixer(uv, v_norm_g, w_s, b_s, out_norm_g):
    bsz, t_len, _ = uv.shape
    nc = t_len // GMLP_CHUNK
    u, v = jnp.split(jax.nn.gelu(uv, approximate=False), 2, axis=-1)
    v = rms_norm(v.reshape(bsz, t_len, GMLP_HEADS, GMLP_HEAD_DIM),
                 v_norm_g.reshape(GMLP_HEADS, GMLP_HEAD_DIM))
    v = v.reshape(bsz, nc, GMLP_CHUNK, GMLP_HEADS, GMLP_HEAD_DIM)
    w = w_s * jnp.tril(jnp.ones((GMLP_CHUNK, GMLP_CHUNK), w_s.dtype))
    sv = jnp.einsum('hts,bcshd->bcthd', w, v) + b_s.T[None, None, :, :, None]
    y = u.reshape(bsz, nc, GMLP_CHUNK, GMLP_HEADS, GMLP_HEAD_DIM) * sv
    return rms_norm(y.reshape(bsz, t_len, GMLP_WIDTH), out_norm_g)


def clamped_swiglu(h):
    glu, lin = jnp.split(h, 2, axis=-1)
    glu = jnp.minimum(glu, SWIGLU_LIMIT)
    lin = jnp.clip(lin, -SWIGLU_LIMIT, SWIGLU_LIMIT)
    return glu * jax.nn.sigmoid(SWIGLU_ALPHA * glu) * (lin + 1.0)


def moe_ffn(x, w_router, b_router, w_gate_up, b_gate_up, w_down, b_down):
    bsz, t_len, d = x.shape
    xt = x.reshape(-1, d)
    n_tok = xt.shape[0]
    n_assign = n_tok * TOP_K
    logits = (xt @ w_router + b_router).astype(jnp.float32)
    top_val, top_idx = lax.top_k(logits, TOP_K)
    gate = jax.nn.softmax(top_val, axis=-1)
    flat_e = top_idx.reshape(-1).astype(jnp.int32)
    flat_tok = jnp.arange(n_assign, dtype=jnp.int32) // TOP_K
    sorted_e, order = lax.sort((flat_e, jnp.arange(n_assign, dtype=jnp.int32)), num_keys=1, is_stable=True)
    counts = jnp.bincount(flat_e, length=N_EXPERTS).astype(jnp.int32)
    start = jnp.cumsum(counts) - counts
    padded = ((counts + MOE_BLOCK - 1) // MOE_BLOCK) * MOE_BLOCK
    padded_end = jnp.cumsum(padded)
    padded_start = padded_end - padded
    rank = jnp.arange(n_assign, dtype=jnp.int32) - start[sorted_e]
    dest = padded_start[sorted_e] + rank
    n_blocks = -(-n_assign // MOE_BLOCK) + N_EXPERTS
    n_rows = n_blocks * MOE_BLOCK
    row_tok = jnp.zeros((n_rows,), jnp.int32).at[dest].set(flat_tok[order])
    block_e = jnp.minimum(
        jnp.searchsorted(padded_end, jnp.arange(n_blocks, dtype=jnp.int32) * MOE_BLOCK, side='right'),
        N_EXPERTS - 1).astype(jnp.int32)

    def expert_block(args):
        tok, e = args
        h = xt[tok] @ w_gate_up[e] + b_gate_up[e]
        return clamped_swiglu(h) @ w_down[e] + b_down[e]

    y_rows = lax.map(expert_block, (row_tok.reshape(n_blocks, MOE_BLOCK), block_e)).reshape(n_rows, d)
    assign_row = jnp.zeros((n_assign,), jnp.int32).at[order].set(dest)
    y = y_rows[assign_row].reshape(n_tok, TOP_K, d)
    out = jnp.einsum('tkd,tk->td', y, gate)
    return out.reshape(bsz, t_len, d).astype(x.dtype)


def setup_inputs(seed: int = 0) -> dict:
    key = jax.random.key(seed)
    ks = jax.random.split(key, 24)
    f32 = jnp.float32

    def nrm(k, shape, scale):
        return jax.random.normal(k, shape, f32) * scale

    def gain(k, shape):
        return 1.0 + 0.02 * jax.random.normal(k, shape, f32)

    x = nrm(ks[0], (BATCH, SEQ, D_MODEL), 1.0)
    mix_norm_g = gain(ks[1], (DEPTH, D_MODEL))
    w_in = nrm(ks[2], (DEPTH, D_MODEL, IN_PROJ_DIM), D_MODEL ** -0.5)
    ssd_conv_w = nrm(ks[3], (DEPTH, SSD_CONV, SSD_CONV_DIM), SSD_CONV ** -0.5)
    ssd_conv_b = nrm(ks[4], (DEPTH, SSD_CONV_DIM), 0.02)
    dt0 = jnp.exp(jax.random.uniform(ks[5], (DEPTH, SSD_HEADS), f32, math.log(DT_MIN), math.log(DT_MAX)))
    ssd_dt_bias = dt0 + jnp.log(-jnp.expm1(-dt0))
    ssd_a_log = jnp.log(jax.random.uniform(ks[6], (DEPTH, SSD_HEADS), f32, 1.0, 16.0))
    ssd_d = gain(ks[7], (DEPTH, SSD_HEADS))
    ssd_norm_g = gain(ks[8], (DEPTH, SSD_WIDTH))
    gmlp_v_norm_g = gain(ks[9], (DEPTH, GMLP_WIDTH))
    gmlp_w_s = nrm(ks[10], (DEPTH, GMLP_HEADS, GMLP_CHUNK, GMLP_CHUNK), GMLP_CHUNK ** -0.5)
    gmlp_b_s = gain(ks[11], (DEPTH, GMLP_HEADS, GMLP_CHUNK))
    gmlp_out_norm_g = gain(ks[12], (DEPTH, GMLP_WIDTH))
    w_out = nrm(ks[13], (DEPTH, MIX_WIDTH, D_MODEL), MIX_WIDTH ** -0.5)
    ffn_norm_g = gain(ks[14], (DEPTH, D_MODEL))
    w_router = nrm(ks[15], (DEPTH, D_MODEL, N_EXPERTS), D_MODEL ** -0.5)
    b_router = nrm(ks[16], (DEPTH, N_EXPERTS), 0.01)
    w_gate_up = nrm(ks[17], (DEPTH, N_EXPERTS, D_MODEL, 2 * EXPERT_FF), D_MODEL ** -0.5)
    b_gate_up = nrm(ks[18], (DEPTH, N_EXPERTS, 2 * EXPERT_FF), 0.02)
    w_down = nrm(ks[19], (DEPTH, N_EXPERTS, EXPERT_FF, D_MODEL), EXPERT_FF ** -0.5)
    b_down = nrm(ks[20], (DEPTH, N_EXPERTS, D_MODEL), 0.02)
    final_norm_g = gain(ks[21], (D_MODEL,))
    return {"x": x, "mix_norm_g": mix_norm_g, "w_in": w_in, "ssd_conv_w": ssd_conv_w,
            "ssd_conv_b": ssd_conv_b, "ssd_dt_bias": ssd_dt_bias, "ssd_a_log": ssd_a_log,
            "ssd_d": ssd_d, "ssd_norm_g": ssd_norm_g, "gmlp_v_norm_g": gmlp_v_norm_g,
            "gmlp_w_s": gmlp_w_s, "gmlp_b_s": gmlp_b_s, "gmlp_out_norm_g": gmlp_out_norm_g,
            "w_out": w_out, "ffn_norm_g": ffn_norm_g, "w_router": w_router, "b_router": b_router,
            "w_gate_up": w_gate_up, "b_gate_up": b_gate_up, "w_down": w_down, "b_down": b_down,
            "final_norm_g": final_norm_g}


def reference(x, mix_norm_g, w_in, ssd_conv_w, ssd_conv_b, ssd_dt_bias, ssd_a_log, ssd_d,
              ssd_norm_g, gmlp_v_norm_g, gmlp_w_s, gmlp_b_s, gmlp_out_norm_g, w_out,
              ffn_norm_g, w_router, b_router, w_gate_up, b_gate_up, w_down, b_down, final_norm_g):
    split_points = [SSD_WIDTH, SSD_WIDTH + SSD_CONV_DIM, SSD_WIDTH + SSD_CONV_DIM + SSD_HEADS]
    for layer in range(DEPTH):
        xn = rms_norm(x, mix_norm_g[layer])
        proj = jnp.einsum('btd,de->bte', xn, w_in[layer])
        z, xbc, dt, uv = jnp.split(proj, split_points, axis=-1)
        y_ssd = ssd_mixer(z, xbc, dt, ssd_conv_w[layer], ssd_conv_b[layer], ssd_dt_bias[layer],
                          ssd_a_log[layer], ssd_d[layer], ssd_norm_g[layer])
        y_gmlp = gmlp_mixer(uv, gmlp_v_norm_g[layer], gmlp_w_s[layer], gmlp_b_s[layer],
                            gmlp_out_norm_g[layer])
        mixed = jnp.concatenate([y_ssd, y_gmlp], axis=-1)
        x = x + jnp.einsum('bte,ed->btd', mixed, w_out[layer]).astype(x.dtype)
        x = x + moe_ffn(rms_norm(x, ffn_norm_g[layer]), w_router[layer], b_router[layer],
                        w_gate_up[layer], b_gate_up[layer], w_down[layer], b_down[layer])
    return rms_norm(x, final_norm_g)
```

```python
import functools

import jax
import jax.numpy as jnp
from jax import lax
from jax.experimental import pallas as pl
from jax.experimental.pallas import tpu as pltpu

F32, BF16, U32, I32 = jnp.float32, jnp.bfloat16, jnp.uint32, jnp.int32

EPS = 1e-5
SSD_STATE = 128
SSD_CONV = 4
CHUNK = 128
TOP_K = 4
SWIGLU_ALPHA = 1.702
SWIGLU_LIMIT = 7.0

V7X_LANES = 128
V7X_SUBLANES = 8
V7X_VMEM_BYTES = 64 * 1024 * 1024
VMEM_LIMIT = V7X_VMEM_BYTES - 8 * 1024 * 1024

MOE_BLK = 256
MOE_NSB = 6
HI_MASK = 0xFFFF0000


def _largest_tile(total, cap, quantum=V7X_LANES):
    best = None
    t = quantum
    while t <= min(total, cap):
        if total % t == 0:
            best = t
        t += quantum
    assert best is not None, (total, cap, quantum)
    return best


def _params(semantics):
    return pltpu.CompilerParams(dimension_semantics=semantics, vmem_limit_bytes=VMEM_LIMIT)


def _silu(x):
    return x * jax.nn.sigmoid(x)


def _gelu(x):
    return 0.5 * x * (1.0 + lax.erf(x * (2.0 ** -0.5)))


def _softplus(x):
    return jnp.maximum(x, 0.0) + jnp.log1p(jnp.exp(-jnp.abs(x)))


def _pack_pair(lo, hi):
    lo_bits = lax.bitcast_convert_type(lo.astype(BF16).astype(F32), U32)
    hi_bits = lax.bitcast_convert_type(hi.astype(BF16).astype(F32), U32)
    return (lo_bits >> 16) | (hi_bits & jnp.uint32(HI_MASK))


def _unpack_pair(u):
    lo = lax.bitcast_convert_type(u << 16, F32)
    hi = lax.bitcast_convert_type(u & jnp.uint32(HI_MASK), F32)
    return lo, hi


def _inproj_kernel(x_ref, g_ref, w_ref, wdt_ref, p_ref, dt_ref, xn_ref):
    @pl.when(pl.program_id(1) == 0)
    def _():
        x = x_ref[...]
        ms = jnp.mean(x * x, axis=-1, keepdims=True)
        xn = (x * lax.rsqrt(ms + EPS) * g_ref[...]).astype(BF16)
        xn_ref[...] = xn
        dt_ref[...] = jnp.dot(xn, wdt_ref[...], preferred_element_type=F32)

    p_ref[...] = jnp.dot(xn_ref[...], w_ref[...], preferred_element_type=F32).astype(BF16)


def _inproj(x, g, w_main, w_dt):
    t_len, d = x.shape
    n_out = w_main.shape[1]
    tm = _largest_tile(t_len, 512, V7X_SUBLANES)
    tn = _largest_tile(n_out, 1024)
    return pl.pallas_call(
        _inproj_kernel,
        grid=(t_len // tm, n_out // tn),
        in_specs=[
            pl.BlockSpec((tm, d), lambda i, j: (i, 0)),
            pl.BlockSpec((1, d), lambda i, j: (0, 0)),
            pl.BlockSpec((d, tn), lambda i, j: (0, j)),
            pl.BlockSpec((d, V7X_LANES), lambda i, j: (0, 0)),
        ],
        out_specs=[
            pl.BlockSpec((tm, tn), lambda i, j: (i, j)),
            pl.BlockSpec((tm, V7X_LANES), lambda i, j: (i, 0)),
        ],
        out_shape=[
            jax.ShapeDtypeStruct((t_len, n_out), BF16),
            jax.ShapeDtypeStruct((t_len, V7X_LANES), F32),
        ],
        scratch_shapes=[pltpu.VMEM((tm, d), BF16)],
        compiler_params=_params(("arbitrary", "arbitrary")),
        name="inproj",
    )(x, g.reshape(1, d), w_main, w_dt)


def _split3(x):
    hi = x.astype(BF16)
    r1 = x - hi.astype(F32)
    mid = r1.astype(BF16)
    lo = (r1 - mid.astype(F32)).astype(BF16)
    return hi, mid, lo


def _ssd_kernel(z_ref, xs_ref, b_ref, c_ref, dt_ref, cw_ref, cb_ref, dtb_ref, alog_ref, dsk_ref,
                ng_ref, y_ref, ext_ref, state_ref, *, hpg, hd):
    width = hpg * hd
    n_st = SSD_STATE
    halo = V7X_SUBLANES
    lanes = V7X_LANES

    @pl.when(pl.program_id(1) == 0)
    def _():
        state_ref[...] = jnp.zeros(state_ref.shape, F32)
        ext_ref[0:halo, :] = jnp.zeros((halo, ext_ref.shape[1]), F32)

    ext_ref[halo:halo + CHUNK, 0:width] = xs_ref[...].astype(F32)
    ext_ref[halo:halo + CHUNK, width:width + n_st] = b_ref[...].astype(F32)
    ext_ref[halo:halo + CHUNK, width + n_st:width + 2 * n_st] = c_ref[...].astype(F32)
    acc = jnp.broadcast_to(cb_ref[...], (CHUNK, ext_ref.shape[1]))
    for k in range(SSD_CONV):
        start = halo - (SSD_CONV - 1) + k
        acc = acc + cw_ref[k:k + 1, :] * ext_ref[start:start + CHUNK, :]
    ext_ref[0:halo, :] = ext_ref[CHUNK:CHUNK + halo, :]
    xbc = _silu(acc)
    xs = xbc[:, 0:width]
    bm = xbc[:, width:width + n_st].astype(BF16)
    cm = xbc[:, width + n_st:width + 2 * n_st].astype(BF16)

    dt = _softplus(dt_ref[...] + dtb_ref[...])
    a_dt = dt * (-jnp.exp(alog_ref[...]))
    ri = lax.broadcasted_iota(I32, (CHUNK, CHUNK), 0)
    ci = lax.broadcasted_iota(I32, (CHUNK, CHUNK), 1)
    causal = ri >= ci
    tril = causal.astype(BF16)
    acum = sum(jnp.dot(tril, piece, preferred_element_type=F32) for piece in _split3(a_dt))
    acum_t = acum.T

    def expand(q):
        per = lanes // hd
        cols = []
        for j in range(width // lanes):
            blk = jnp.broadcast_to(q[:, j * per:j * per + 1], (CHUNK, lanes))
            for t in range(1, per):
                nxt = jnp.broadcast_to(q[:, j * per + t:j * per + t + 1], (CHUNK, lanes))
                blk = jnp.where(ci >= t * hd, nxt, blk)
            cols.append(blk)
        return jnp.concatenate(cols, axis=1)

    acum_f = expand(acum)
    dt_f = expand(dt)
    last = acum_f[CHUNK - 1:CHUNK, :]
    xdt = xs * dt_f

    cb = lax.dot_general(cm, bm, (((1,), (1,)), ((), ())), preferred_element_type=F32)
    per = lanes // hd
    y_cols = []
    for j in range(width // lanes):
        x_col = xdt[:, j * lanes:(j + 1) * lanes]
        y_col = jnp.zeros((CHUNK, lanes), F32)
        for t in range(per):
            h = j * per + t
            seg = acum[:, h:h + 1] - acum_t[h:h + 1, :]
            lmat = jnp.exp(jnp.where(causal, seg, -jnp.inf))
            m_h = (cb * lmat).astype(BF16)
            mask = (ci >= t * hd) & (ci < (t + 1) * hd)
            x_h = jnp.where(mask, x_col, 0.0).astype(BF16)
            y_col = y_col + jnp.dot(m_h, x_h, preferred_element_type=F32)
        y_cols.append(y_col)
    y_diag = jnp.concatenate(y_cols, axis=1)

    state = state_ref[...]
    y_off = jnp.dot(cm, state.astype(BF16), preferred_element_type=F32) * jnp.exp(acum_f)
    x_dec = (xdt * jnp.exp(last - acum_f)).astype(BF16)
    s_new = lax.dot_general(bm, x_dec, (((0,), (0,)), ((), ())), preferred_element_type=F32)
    state_ref[...] = state * jnp.exp(last) + s_new

    y = y_diag + y_off + xs * dsk_ref[...]
    yg = y * _silu(z_ref[...].astype(F32))
    yn = yg * lax.rsqrt(jnp.mean(yg * yg, axis=-1, keepdims=True) + EPS) * ng_ref[...]
    y_ref[...] = yn.astype(BF16)


def _ssd(p, dt_raw, conv_w, conv_b, dt_bias, a_log, d_skip, norm_g, off_z, off_xs):
    t_len = p.shape[0]
    heads = dt_bias.shape[0]
    width = norm_g.shape[0]
    hd = width // heads
    groups = (conv_b.shape[0] - width) // (2 * SSD_STATE)
    hpg = heads // groups
    gw = hpg * hd
    n_st = SSD_STATE
    assert V7X_LANES % hd == 0 and gw % V7X_LANES == 0 and hpg <= V7X_LANES
    assert width // groups == gw and t_len % CHUNK == 0
    off_b = off_xs + width
    off_c = off_b + groups * n_st
    assert off_z % gw == 0 and off_xs % gw == 0 and off_b % n_st == 0

    def per_group_lanes(v):
        v = v.reshape(groups, 1, hpg)
        return jnp.pad(v, ((0, 0), (0, 0), (0, V7X_LANES - hpg)))

    dt_g = dt_raw[:, :heads].reshape(t_len, groups, hpg).transpose(1, 0, 2)
    dt_g = jnp.pad(dt_g, ((0, 0), (0, 0), (0, V7X_LANES - hpg)))
    cw_g = jnp.concatenate([
        conv_w[:, :width].reshape(SSD_CONV, groups, gw),
        conv_w[:, width:width + groups * n_st].reshape(SSD_CONV, groups, n_st),
        conv_w[:, width + groups * n_st:].reshape(SSD_CONV, groups, n_st)], axis=-1).transpose(1, 0, 2)
    cb_g = jnp.concatenate([
        conv_b[:width].reshape(groups, 1, gw),
        conv_b[width:width + groups * n_st].reshape(groups, 1, n_st),
        conv_b[width + groups * n_st:].reshape(groups, 1, n_st)], axis=-1)
    conv_cols = gw + 2 * n_st

    grp = lambda g, c: (g, 0, 0)
    return pl.pallas_call(
        functools.partial(_ssd_kernel, hpg=hpg, hd=hd),
        grid=(groups, t_len // CHUNK),
        in_specs=[
            pl.BlockSpec((CHUNK, gw), lambda g, c: (c, off_z // gw + g)),
            pl.BlockSpec((CHUNK, gw), lambda g, c: (c, off_xs // gw + g)),
            pl.BlockSpec((CHUNK, n_st), lambda g, c: (c, off_b // n_st + g)),
            pl.BlockSpec((CHUNK, n_st), lambda g, c: (c, off_c // n_st + g)),
            pl.BlockSpec((None, CHUNK, V7X_LANES), lambda g, c: (g, c, 0)),
            pl.BlockSpec((None, SSD_CONV, conv_cols), grp),
            pl.BlockSpec((None, 1, conv_cols), grp),
            pl.BlockSpec((None, 1, V7X_LANES), grp),
            pl.BlockSpec((None, 1, V7X_LANES), grp),
            pl.BlockSpec((1, gw), lambda g, c: (0, g)),
            pl.BlockSpec((1, gw), lambda g, c: (0, g)),
        ],
        out_specs=pl.BlockSpec((CHUNK, gw), lambda g, c: (c, g)),
        out_shape=jax.ShapeDtypeStruct((t_len, width), BF16),
        scratch_shapes=[
            pltpu.VMEM((CHUNK + V7X_SUBLANES, conv_cols), F32),
            pltpu.VMEM((n_st, gw), F32),
        ],
        compiler_params=_params(("arbitrary", "arbitrary")),
        name="ssd",
    )(p, p, p, p, dt_g, cw_g, cb_g, per_group_lanes(dt_bias), per_group_lanes(a_log),
      jnp.repeat(d_skip, hd).reshape(1, width), norm_g.reshape(1, width))


def _gmlp_kernel(u_ref, v_ref, vg_ref, ws_ref, bs_ref, og_ref, y_ref, yscr_ref, *, heads, hd):
    ri = lax.broadcasted_iota(I32, (CHUNK, CHUNK), 0)
    ci = lax.broadcasted_iota(I32, (CHUNK, CHUNK), 1)
    causal = ri >= ci
    ssq = jnp.zeros((CHUNK, 1), F32)
    for h in range(heads):
        cols = slice(h * hd, (h + 1) * hd)
        u_h = _gelu(u_ref[:, cols].astype(F32))
        v_h = _gelu(v_ref[:, cols].astype(F32))
        v_n = v_h * lax.rsqrt(jnp.mean(v_h * v_h, axis=-1, keepdims=True) + EPS) * vg_ref[:, cols]
        w_h = jnp.where(causal, ws_ref[h], 0.0).astype(BF16)
        sv = jnp.dot(w_h, v_n.astype(BF16), preferred_element_type=F32) + bs_ref[:, h:h + 1]
        y_h = u_h * sv
        yscr_ref[:, cols] = y_h
        ssq = ssq + jnp.sum(y_h * y_h, axis=-1, keepdims=True)
    scale = lax.rsqrt(ssq / (heads * hd) + EPS)
    y_ref[...] = (yscr_ref[...] * scale * og_ref[...]).astype(BF16)


def _gmlp(p, v_norm_g, w_s, b_s, out_norm_g, off_u):
    t_len = p.shape[0]
    heads = w_s.shape[0]
    width = out_norm_g.shape[0]
    hd = width // heads
    assert w_s.shape[1] == CHUNK and off_u % width == 0 and hd % V7X_LANES == 0
    return pl.pallas_call(
        functools.partial(_gmlp_kernel, heads=heads, hd=hd),
        grid=(t_len // CHUNK,),
        in_specs=[
            pl.BlockSpec((CHUNK, width), lambda c: (c, off_u // width)),
            pl.BlockSpec((CHUNK, width), lambda c: (c, off_u // width + 1)),
            pl.BlockSpec((1, width), lambda c: (0, 0)),
            pl.BlockSpec((heads, CHUNK, CHUNK), lambda c: (0, 0, 0)),
            pl.BlockSpec((CHUNK, heads), lambda c: (0, 0)),
            pl.BlockSpec((1, width), lambda c: (0, 0)),
        ],
        out_specs=pl.BlockSpec((CHUNK, width), lambda c: (c, 0)),
        out_shape=jax.ShapeDtypeStruct((t_len, width), BF16),
        scratch_shapes=[pltpu.VMEM((CHUNK, width), F32)],
        compiler_params=_params(("arbitrary",)),
        name="gmlp",
    )(p, p, v_norm_g.reshape(1, width), w_s, b_s.T, out_norm_g.reshape(1, width))


def _outproj_kernel(ys_ref, yg_ref, wa_ref, wb_ref, x_ref, o_ref):
    acc = jnp.dot(ys_ref[...], wa_ref[...], preferred_element_type=F32)
    acc = acc + jnp.dot(yg_ref[...], wb_ref[...], preferred_element_type=F32)
    o_ref[...] = x_ref[...] + acc


def _outproj(y_ssd, y_gmlp, w_a, w_b, x):
    t_len, d = x.shape
    ka, kb = y_ssd.shape[1], y_gmlp.shape[1]
    tm = _largest_tile(t_len, 512, V7X_SUBLANES)
    tn = _largest_tile(d, 512)
    return pl.pallas_call(
        _outproj_kernel,
        grid=(t_len // tm, d // tn),
        in_specs=[
            pl.BlockSpec((tm, ka), lambda i, j: (i, 0)),
            pl.BlockSpec((tm, kb), lambda i, j: (i, 0)),
            pl.BlockSpec((ka, tn), lambda i, j: (0, j)),
            pl.BlockSpec((kb, tn), lambda i, j: (0, j)),
            pl.BlockSpec((tm, tn), lambda i, j: (i, j)),
        ],
        out_specs=pl.BlockSpec((tm, tn), lambda i, j: (i, j)),
        out_shape=jax.ShapeDtypeStruct((t_len, d), F32),
        compiler_params=_params(("arbitrary", "arbitrary")),
        name="outproj",
    )(y_ssd, y_gmlp, w_a, w_b, x)


def _router_kernel(x_ref, g_ref, wr_ref, br_ref, xp_ref, idx_ref, gate_ref, *, n_exp):
    x = x_ref[...]
    tm, d = x.shape
    half = d // 2
    ms = jnp.mean(x * x, axis=-1, keepdims=True)
    xn = x * lax.rsqrt(ms + EPS) * g_ref[...]
    xp_ref[...] = _pack_pair(xn[:, :half], xn[:, half:])

    logits = jnp.dot(xn.astype(BF16), wr_ref[...], preferred_element_type=F32) + br_ref[...]
    lane = lax.broadcasted_iota(I32, (tm, V7X_LANES), 1)
    vals = jnp.where(lane < n_exp, logits, -jnp.inf)
    idx_out = jnp.zeros((tm, V7X_LANES), I32)
    val_out = jnp.full((tm, V7X_LANES), -jnp.inf, F32)
    for k in range(TOP_K):
        m = jnp.max(vals, axis=-1, keepdims=True)
        sel = jnp.min(jnp.where(vals == m, lane, V7X_LANES), axis=-1, keepdims=True)
        idx_out = jnp.where(lane == k, sel, idx_out)
        val_out = jnp.where(lane == k, m, val_out)
        vals = jnp.where(lane == sel, -jnp.inf, vals)
    e = jnp.exp(val_out - jnp.max(val_out, axis=-1, keepdims=True))
    idx_ref[...] = idx_out
    gate_ref[...] = e / jnp.sum(e, axis=-1, keepdims=True)


def _router(x1, g, w_router, b_router):
    t_len, d = x1.shape
    n_exp = w_router.shape[1]
    assert n_exp <= V7X_LANES and TOP_K <= n_exp
    tm = _largest_tile(t_len, 256, V7X_SUBLANES)
    wr = jnp.pad(w_router, ((0, 0), (0, V7X_LANES - n_exp))).astype(BF16)
    br = jnp.pad(b_router, (0, V7X_LANES - n_exp)).reshape(1, V7X_LANES)
    return pl.pallas_call(
        functools.partial(_router_kernel, n_exp=n_exp),
        grid=(t_len // tm,),
        in_specs=[
            pl.BlockSpec((tm, d), lambda i: (i, 0)),
            pl.BlockSpec((1, d), lambda i: (0, 0)),
            pl.BlockSpec((d, V7X_LANES), lambda i: (0, 0)),
            pl.BlockSpec((1, V7X_LANES), lambda i: (0, 0)),
        ],
        out_specs=[
            pl.BlockSpec((tm, d // 2), lambda i: (i, 0)),
            pl.BlockSpec((tm, V7X_LANES), lambda i: (i, 0)),
            pl.BlockSpec((tm, V7X_LANES), lambda i: (i, 0)),
        ],
        out_shape=[
            jax.ShapeDtypeStruct((t_len, d // 2), U32),
            jax.ShapeDtypeStruct((t_len, V7X_LANES), I32),
            jax.ShapeDtypeStruct((t_len, V7X_LANES), F32),
        ],
        compiler_params=_params(("arbitrary",)),
        name="router",
    )(x1, g.reshape(1, d), wr, br)


def _route_plan(top_idx, n_exp):
    t_len, top_k = top_idx.shape
    n_assign = t_len * top_k
    e = top_idx.reshape(n_assign)
    onehot = (e[:, None] == jnp.arange(n_exp, dtype=I32)[None, :]).astype(I32)
    csum = jnp.cumsum(onehot, axis=0)
    rank = jnp.sum(onehot * csum, axis=1) - 1
    counts = csum[-1]
    nb = (counts + MOE_BLK - 1) // MOE_BLK
    sb_end = jnp.cumsum(nb)
    sb_start = sb_end - nb
    dest = sb_start[e] * MOE_BLK + rank
    n_sb = n_assign // MOE_BLK + n_exp
    n_rows = n_sb * MOE_BLK
    row_tok = jnp.zeros((n_rows,), I32).at[dest].set(jnp.arange(n_assign, dtype=I32) // top_k)

    nv = (nb + MOE_NSB - 1) // MOE_NSB
    v_end = jnp.cumsum(nv)
    v_start = v_end - nv
    n_pass = -(-n_sb // MOE_NSB) + n_exp
    vid = jnp.arange(n_pass, dtype=I32)
    ve = jnp.minimum(jnp.searchsorted(v_end, vid, side='right'), n_exp - 1).astype(I32)
    q = vid - v_start[ve]
    valid = vid < v_end[-1]
    vs = jnp.where(valid, sb_start[ve] + q * MOE_NSB, 0).astype(I32)
    vn = jnp.where(valid, jnp.clip(nb[ve] - q * MOE_NSB, 0, MOE_NSB), 0).astype(I32)
    last_e = ve[jnp.maximum(v_end[-1] - 1, 0)]
    ve = jnp.where(valid, ve, last_e).astype(I32)
    used = sb_end[-1:].astype(I32)
    return dict(row_tok=row_tok, dest=dest.astype(I32), ve=ve, vs=vs, vn=vn, used=used, n_rows=n_rows,
                n_pass=n_pass)


def _zero_tail(obuf, n_used, n_sb, n_j, out_copy):
    tile = obuf.shape[2]
    obuf[0, 0:MOE_BLK, :] = jnp.zeros((MOE_BLK, tile), obuf.dtype)

    def start(sb, carry):
        for jj in range(n_j):
            out_copy(0, 0, pl.multiple_of(sb * MOE_BLK, MOE_BLK), jj * tile).start()
        return carry

    def wait(sb, carry):
        for jj in range(n_j):
            out_copy(0, 0, 0, 0).wait()
        return carry

    lax.fori_loop(n_used, n_sb, start, 0)
    lax.fori_loop(n_used, n_sb, wait, 0)


def _moe_up_kernel(ve_ref, vs_ref, vn_ref, used_ref, tok_ref, xp_hbm, wg_ref, wl_ref, bg_ref, bl_ref, act_hbm,
                   xbuf, wc_ref, obuf, xsem, osem, ocnt, *, tf):
    v = pl.program_id(0)
    j = pl.program_id(1)
    n_pass = pl.num_programs(0)
    n_j = pl.num_programs(1)
    half = xbuf.shape[2]
    slot = v % 2
    n_sub = vn_ref[v]

    def row_copy(tok, sl, r):
        return pltpu.make_async_copy(xp_hbm.at[pl.ds(tok, 1), :], xbuf.at[sl, pl.ds(r, 1), :], xsem.at[sl])

    def gather_start(vv, sl):
        base = vs_ref[vv] * MOE_BLK

        def body(r, carry):
            row_copy(tok_ref[base + r], sl, r).start()
            return carry

        lax.fori_loop(0, vn_ref[vv] * MOE_BLK, body, 0)

    def gather_wait(vv, sl):
        def body(r, carry):
            row_copy(0, sl, r).wait()
            return carry

        lax.fori_loop(0, vn_ref[vv] * MOE_BLK, body, 0)

    def out_copy(p, r0, row, col):
        return pltpu.make_async_copy(obuf.at[p, pl.ds(r0, MOE_BLK), :],
                                     act_hbm.at[pl.ds(row, MOE_BLK), pl.ds(col, tf)], osem.at[p])

    def drain(p):
        def body(i, carry):
            out_copy(p, 0, 0, 0).wait()
            return carry

        lax.fori_loop(0, ocnt[p], body, 0)
        ocnt[p] = 0

    @pl.when(j == 0)
    def _():
        @pl.when(v == 0)
        def _():
            ocnt[0] = 0
            ocnt[1] = 0
            gather_start(0, 0)

        @pl.when(v + 1 < n_pass)
        def _():
            gather_start(v + 1, 1 - slot)

        gather_wait(v, slot)

    step = v * n_j + j
    p = step % 2
    drain(p)

    @pl.when(n_sub > 0)
    def _():
        wc_ref[:, 0:tf] = wg_ref[...].astype(BF16)
        wc_ref[:, tf:2 * tf] = wl_ref[...].astype(BF16)

        def sub_block(i, carry):
            r0 = pl.multiple_of(i * MOE_BLK, MOE_BLK)
            lo, hi = _unpack_pair(xbuf[slot, pl.ds(r0, MOE_BLK), :])
            h = jnp.dot(lo.astype(BF16), wc_ref[0:half, :], preferred_element_type=F32)
            h = h + jnp.dot(hi.astype(BF16), wc_ref[half:2 * half, :], preferred_element_type=F32)
            glu = jnp.minimum(h[:, 0:tf] + bg_ref[...], SWIGLU_LIMIT)
            lin = jnp.clip(h[:, tf:2 * tf] + bl_ref[...], -SWIGLU_LIMIT, SWIGLU_LIMIT)
            act = glu * jax.nn.sigmoid(SWIGLU_ALPHA * glu) * (lin + 1.0)
            obuf[p, pl.ds(r0, MOE_BLK), :] = act.astype(BF16)
            row = pl.multiple_of((vs_ref[v] + i) * MOE_BLK, MOE_BLK)
            out_copy(p, r0, row, pl.multiple_of(j * tf, tf)).start()
            return carry

        lax.fori_loop(0, n_sub, sub_block, 0)
        ocnt[p] = n_sub

    @pl.when(step == n_pass * n_j - 1)
    def _():
        drain(0)
        drain(1)
        _zero_tail(obuf, used_ref[0], act_hbm.shape[0] // MOE_BLK, n_j, out_copy)


def _moe_up(xp, plan, w_gate_up, b_gate_up):
    n_exp, d, two_f = w_gate_up.shape
    f = two_f // 2
    half = xp.shape[1]
    assert half * 2 == d
    tf = _largest_tile(f, 256)
    n_j = f // tf
    rows_pass = MOE_NSB * MOE_BLK

    def w_map(off):
        def index_map(v, j, ve, vs, vn, used, tok):
            jj = jnp.where(vn[v] > 0, j, n_j - 1)
            return (ve[v], 0, off + jj)
        return index_map

    grid_spec = pltpu.PrefetchScalarGridSpec(
        num_scalar_prefetch=5,
        grid=(plan["n_pass"], n_j),
        in_specs=[
            pl.BlockSpec(memory_space=pl.ANY),
            pl.BlockSpec((None, d, tf), w_map(0)),
            pl.BlockSpec((None, d, tf), w_map(n_j)),
            pl.BlockSpec((None, 1, tf), w_map(0)),
            pl.BlockSpec((None, 1, tf), w_map(n_j)),
        ],
        out_specs=pl.BlockSpec(memory_space=pl.ANY),
        scratch_shapes=[
            pltpu.VMEM((2, rows_pass, half), U32),
            pltpu.VMEM((d, 2 * tf), BF16),
            pltpu.VMEM((2, rows_pass, tf), BF16),
            pltpu.SemaphoreType.DMA((2,)),
            pltpu.SemaphoreType.DMA((2,)),
            pltpu.SMEM((2,), I32),
        ],
    )
    b3 = b_gate_up.reshape(n_exp, 1, two_f)
    return pl.pallas_call(
        functools.partial(_moe_up_kernel, tf=tf),
        grid_spec=grid_spec,
        out_shape=jax.ShapeDtypeStruct((plan["n_rows"], f), BF16),
        compiler_params=_params(("arbitrary", "arbitrary")),
        name="moe_up",
    )(plan["ve"], plan["vs"], plan["vn"], plan["used"], plan["row_tok"], xp, w_gate_up, w_gate_up, b3, b3)


def _moe_down_kernel(ve_ref, vs_ref, vn_ref, used_ref, act_hbm, wa_ref, wb_ref, ba_ref, bb_ref, y_hbm,
                     abuf, wc_ref, obuf, asem, osem, ocnt, *, tn):
    v = pl.program_id(0)
    j = pl.program_id(1)
    n_pass = pl.num_programs(0)
    n_j = pl.num_programs(1)
    slot = v % 2
    n_sub = vn_ref[v]

    def in_copy(row, sl, r0):
        return pltpu.make_async_copy(act_hbm.at[pl.ds(row, MOE_BLK), :],
                                     abuf.at[sl, pl.ds(r0, MOE_BLK), :], asem.at[sl])

    def load_start(vv, sl):
        def body(i, carry):
            r0 = pl.multiple_of(i * MOE_BLK, MOE_BLK)
            in_copy(pl.multiple_of((vs_ref[vv] + i) * MOE_BLK, MOE_BLK), sl, r0).start()
            return carry

        lax.fori_loop(0, vn_ref[vv], body, 0)

    def load_wait(vv, sl):
        def body(i, carry):
            in_copy(0, sl, pl.multiple_of(i * MOE_BLK, MOE_BLK)).wait()
            return carry

        lax.fori_loop(0, vn_ref[vv], body, 0)

    def out_copy(p, r0, row, col):
        return pltpu.make_async_copy(obuf.at[p, pl.ds(r0, MOE_BLK), :],
                                     y_hbm.at[pl.ds(row, MOE_BLK), pl.ds(col, tn)], osem.at[p])

    def drain(p):
        def body(i, carry):
            out_copy(p, 0, 0, 0).wait()
            return carry

        lax.fori_loop(0, ocnt[p], body, 0)
        ocnt[p] = 0

    @pl.when(j == 0)
    def _():
        @pl.when(v == 0)
        def _():
            ocnt[0] = 0
            ocnt[1] = 0
            load_start(0, 0)

        @pl.when(v + 1 < n_pass)
        def _():
            load_start(v + 1, 1 - slot)

        load_wait(v, slot)

    step = v * n_j + j
    p = step % 2
    drain(p)

    @pl.when(n_sub > 0)
    def _():
        wc_ref[:, 0:tn] = wa_ref[...].astype(BF16)
        wc_ref[:, tn:2 * tn] = wb_ref[...].astype(BF16)

        def sub_block(i, carry):
            r0 = pl.multiple_of(i * MOE_BLK, MOE_BLK)
            y = jnp.dot(abuf[slot, pl.ds(r0, MOE_BLK), :], wc_ref[...], preferred_element_type=F32)
            obuf[p, pl.ds(r0, MOE_BLK), :] = _pack_pair(y[:, 0:tn] + ba_ref[...], y[:, tn:2 * tn] + bb_ref[...])
            row = pl.multiple_of((vs_ref[v] + i) * MOE_BLK, MOE_BLK)
            out_copy(p, r0, row, pl.multiple_of(j * tn, tn)).start()
            return carry

        lax.fori_loop(0, n_sub, sub_block, 0)
        ocnt[p] = n_sub

    @pl.when(step == n_pass * n_j - 1)
    def _():
        drain(0)
        drain(1)
        _zero_tail(obuf, used_ref[0], y_hbm.shape[0] // MOE_BLK, n_j, out_copy)


def _moe_down(act, plan, w_down, b_down):
    n_exp, f, d = w_down.shape
    half = d // 2
    tn = _largest_tile(half, 256)
    n_j = half // tn
    rows_pass = MOE_NSB * MOE_BLK

    def w_map(off):
        def index_map(v, j, ve, vs, vn, used):
            jj = jnp.where(vn[v] > 0, j, n_j - 1)
            return (ve[v], 0, off + jj)
        return index_map

    grid_spec = pltpu.PrefetchScalarGridSpec(
        num_scalar_prefetch=4,
        grid=(plan["n_pass"], n_j),
        in_specs=[
            pl.BlockSpec(memory_space=pl.ANY),
            pl.BlockSpec((None, f, tn), w_map(0)),
            pl.BlockSpec((None, f, tn), w_map(n_j)),
            pl.BlockSpec((None, 1, tn), w_map(0)),
            pl.BlockSpec((None, 1, tn), w_map(n_j)),
        ],
        out_specs=pl.BlockSpec(memory_space=pl.ANY),
        scratch_shapes=[
            pltpu.VMEM((2, rows_pass, f), BF16),
            pltpu.VMEM((f, 2 * tn), BF16),
            pltpu.VMEM((2, rows_pass, tn), U32),
            pltpu.SemaphoreType.DMA((2,)),
            pltpu.SemaphoreType.DMA((2,)),
            pltpu.SMEM((2,), I32),
        ],
    )
    b3 = b_down.reshape(n_exp, 1, d)
    return pl.pallas_call(
        functools.partial(_moe_down_kernel, tn=tn),
        grid_spec=grid_spec,
        out_shape=jax.ShapeDtypeStruct((plan["n_rows"], half), U32),
        compiler_params=_params(("arbitrary", "arbitrary")),
        name="moe_down",
    )(plan["ve"], plan["vs"], plan["vn"], plan["used"], act, w_down, w_down, b3, b3)


def _combine_kernel(dest_ref, y_hbm, gate_ref, x_ref, g_ref, o_ref, ybuf, sem):
    i = pl.program_id(0)
    n_i = pl.num_programs(0)
    tm, d = x_ref.shape
    half = d // 2
    slot = i % 2

    def row_copy(row, sl, k, t):
        return pltpu.make_async_copy(y_hbm.at[pl.ds(row, 1), :], ybuf.at[sl, k, pl.ds(t, 1), :], sem.at[sl])

    def gather_start(ii, sl):
        def body(t, carry):
            for k in range(TOP_K):
                row_copy(dest_ref[(ii * tm + t) * TOP_K + k], sl, k, t).start()
            return carry

        lax.fori_loop(0, tm, body, 0)

    def gather_wait(sl):
        def body(t, carry):
            for k in range(TOP_K):
                row_copy(0, sl, k, t).wait()
            return carry

        lax.fori_loop(0, tm, body, 0)

    @pl.when(i == 0)
    def _():
        gather_start(0, 0)

    @pl.when(i + 1 < n_i)
    def _():
        gather_start(i + 1, 1 - slot)

    gather_wait(slot)

    acc_lo = x_ref[:, 0:half]
    acc_hi = x_ref[:, half:d]
    for k in range(TOP_K):
        lo, hi = _unpack_pair(ybuf[slot, k])
        gk = gate_ref[:, k:k + 1]
        acc_lo = acc_lo + gk * lo
        acc_hi = acc_hi + gk * hi
    ms = (jnp.sum(acc_lo * acc_lo, axis=-1, keepdims=True)
          + jnp.sum(acc_hi * acc_hi, axis=-1, keepdims=True)) / d
    r = lax.rsqrt(ms + EPS)
    o_ref[:, 0:half] = acc_lo * r * g_ref[:, 0:half]
    o_ref[:, half:d] = acc_hi * r * g_ref[:, half:d]


def _combine(y_rows, dest, gate, x1, final_g):
    t_len, d = x1.shape
    half = d // 2
    tm = _largest_tile(t_len, 128, V7X_SUBLANES)
    grid_spec = pltpu.PrefetchScalarGridSpec(
        num_scalar_prefetch=1,
        grid=(t_len // tm,),
        in_specs=[
            pl.BlockSpec(memory_space=pl.ANY),
            pl.BlockSpec((tm, V7X_LANES), lambda i, dest: (i, 0)),
            pl.BlockSpec((tm, d), lambda i, dest: (i, 0)),
            pl.BlockSpec((1, d), lambda i, dest: (0, 0)),
        ],
        out_specs=pl.BlockSpec((tm, d), lambda i, dest: (i, 0)),
        scratch_shapes=[
            pltpu.VMEM((2, TOP_K, tm, half), U32),
            pltpu.SemaphoreType.DMA((2,)),
        ],
    )
    return pl.pallas_call(
        _combine_kernel,
        grid_spec=grid_spec,
        out_shape=jax.ShapeDtypeStruct((t_len, d), F32),
        compiler_params=_params(("arbitrary",)),
        name="combine",
    )(dest, y_rows, gate, x1, final_g.reshape(1, d))


def kernel(x, mix_norm_g, w_in, ssd_conv_w, ssd_conv_b, ssd_dt_bias, ssd_a_log, ssd_d, ssd_norm_g,
           gmlp_v_norm_g, gmlp_w_s, gmlp_b_s, gmlp_out_norm_g, w_out, ffn_norm_g, w_router, b_router,
           w_gate_up, b_gate_up, w_down, b_down, final_norm_g):
    bsz, t_len, d = x.shape
    assert bsz == 1
    depth = w_in.shape[0]
    ssd_w = ssd_norm_g.shape[-1]
    heads = ssd_dt_bias.shape[-1]
    gmlp_w = gmlp_out_norm_g.shape[-1]
    conv_dim = ssd_conv_b.shape[-1]
    assert heads <= V7X_LANES
    c_dt = ssd_w + conv_dim
    c_uv = c_dt + heads
    off_u, off_z, off_xs = 0, 2 * gmlp_w, 2 * gmlp_w + ssd_w

    h = x.reshape(t_len, d)
    out = None
    for layer in range(depth):
        w = w_in[layer]
        w_main = jnp.concatenate([w[:, c_uv:], w[:, :c_dt]], axis=1).astype(BF16)
        w_dt = jnp.pad(w[:, c_dt:c_uv], ((0, 0), (0, V7X_LANES - heads))).astype(BF16)
        p, dt_raw = _inproj(h, mix_norm_g[layer], w_main, w_dt)
        y_ssd = _ssd(p, dt_raw, ssd_conv_w[layer], ssd_conv_b[layer], ssd_dt_bias[layer],
                     ssd_a_log[layer], ssd_d[layer], ssd_norm_g[layer], off_z, off_xs)
        y_gmlp = _gmlp(p, gmlp_v_norm_g[layer], gmlp_w_s[layer], gmlp_b_s[layer],
                       gmlp_out_norm_g[layer], off_u)
        wo = w_out[layer].astype(BF16)
        x1 = _outproj(y_ssd, y_gmlp, wo[:ssd_w], wo[ssd_w:], h)

        xp, idx_pad, gate_pad = _router(x1, ffn_norm_g[layer], w_router[layer], b_router[layer])
        plan = _route_plan(idx_pad[:, :TOP_K], w_router.shape[-1])
        act = _moe_up(xp, plan, w_gate_up[layer], b_gate_up[layer])
        y_rows = _moe_down(act, plan, w_down[layer], b_down[layer])
        last = layer == depth - 1
        assert last, "multi-layer stacking needs an un-normalised combine"
        out = _combine(y_rows, plan["dest"], gate_pad, x1, final_norm_g)
    return out.reshape(bsz, t_len, d)
```

```python
import functools

import jax
import jax.numpy as jnp
from jax import lax
from jax.experimental import pallas as pl
from jax.experimental.pallas import tpu as pltpu

F32, BF16, U32, I32 = jnp.float32, jnp.bfloat16, jnp.uint32, jnp.int32

EPS = 1e-5
SSD_STATE = 128
SSD_CONV = 4
CHUNK = 128
TOP_K = 4
SWIGLU_ALPHA = 1.702
SWIGLU_LIMIT = 7.0

V7X_LANES = 128
V7X_SUBLANES = 8
V7X_VMEM_BYTES = 64 * 1024 * 1024
VMEM_LIMIT = V7X_VMEM_BYTES - 8 * 1024 * 1024

MOE_BLK = 256
MOE_NSB = 6
ISSUE_UNROLL = 8
HI_MASK = 0xFFFF0000


def _largest_tile(total, cap, quantum=V7X_LANES):
    best = None
    t = quantum
    while t <= min(total, cap):
        if total % t == 0:
            best = t
        t += quantum
    assert best is not None, (total, cap, quantum)
    return best


def _params(semantics):
    return pltpu.CompilerParams(dimension_semantics=semantics, vmem_limit_bytes=VMEM_LIMIT)


def _silu(x):
    return x * jax.nn.sigmoid(x)


def _gelu(x):
    return 0.5 * x * (1.0 + lax.erf(x * (2.0 ** -0.5)))


def _softplus(x):
    return jnp.maximum(x, 0.0) + jnp.log1p(jnp.exp(-jnp.abs(x)))


def _pack_pair(lo, hi):
    lo_bits = lax.bitcast_convert_type(lo.astype(BF16).astype(F32), U32)
    hi_bits = lax.bitcast_convert_type(hi.astype(BF16).astype(F32), U32)
    return (lo_bits >> 16) | (hi_bits & jnp.uint32(HI_MASK))


def _unpack_pair(u):
    lo = lax.bitcast_convert_type(u << 16, F32)
    hi = lax.bitcast_convert_type(u & jnp.uint32(HI_MASK), F32)
    return lo, hi


def _inproj_kernel(x_hbm, g_ref, w_ref, wdt_ref, p_ref, dt_ref, xbuf, xn_ref, sem):
    i = pl.program_id(0)
    tm = xbuf.shape[0]
    rows = _largest_tile(tm, 128, V7X_SUBLANES)

    def x_copy(ii):
        return pltpu.make_async_copy(x_hbm.at[pl.ds(pl.multiple_of(ii * tm, tm), tm), :], xbuf, sem.at[0])

    @pl.when(pl.program_id(1) == 0)
    def _():
        @pl.when(i == 0)
        def _():
            x_copy(0).start()

        x_copy(i).wait()

        def norm(r, carry):
            sl = pl.ds(pl.multiple_of(r * rows, rows), rows)
            x = xbuf[sl, :]
            ms = jnp.mean(x * x, axis=-1, keepdims=True)
            xn = (x * lax.rsqrt(ms + EPS) * g_ref[...]).astype(BF16)
            xn_ref[sl, :] = xn
            dt_ref[sl, :] = jnp.dot(xn, wdt_ref[...], preferred_element_type=F32)
            return carry

        lax.fori_loop(0, tm // rows, norm, 0)

        @pl.when(i + 1 < pl.num_programs(0))
        def _():
            x_copy(i + 1).start()

    p_ref[...] = jnp.dot(xn_ref[...], w_ref[...], preferred_element_type=F32).astype(BF16)


def _inproj(x, g, w_main, w_dt):
    t_len, d = x.shape
    n_out = w_main.shape[1]
    tm = _largest_tile(t_len, 1024, V7X_SUBLANES)
    tn = _largest_tile(n_out, 1024)
    return pl.pallas_call(
        _inproj_kernel,
        grid=(t_len // tm, n_out // tn),
        in_specs=[
            pl.BlockSpec(memory_space=pl.ANY),
            pl.BlockSpec((1, d), lambda i, j: (0, 0)),
            pl.BlockSpec((d, tn), lambda i, j: (0, j)),
            pl.BlockSpec((d, V7X_LANES), lambda i, j: (0, 0)),
        ],
        out_specs=[
            pl.BlockSpec((tm, tn), lambda i, j: (i, j)),
            pl.BlockSpec((tm, V7X_LANES), lambda i, j: (i, 0)),
        ],
        out_shape=[
            jax.ShapeDtypeStruct((t_len, n_out), BF16),
            jax.ShapeDtypeStruct((t_len, V7X_LANES), F32),
        ],
        scratch_shapes=[pltpu.VMEM((tm, d), F32), pltpu.VMEM((tm, d), BF16), pltpu.SemaphoreType.DMA((1,))],
        compiler_params=_params(("arbitrary", "arbitrary")),
        name="inproj",
    )(x, g.reshape(1, d), w_main, w_dt)


def _split3(x):
    hi = x.astype(BF16)
    r1 = x - hi.astype(F32)
    mid = r1.astype(BF16)
    lo = (r1 - mid.astype(F32)).astype(BF16)
    return hi, mid, lo


def _ssd_kernel(z_ref, xs_ref, b_ref, c_ref, dt_ref, cw_ref, cb_ref, dtb_ref, alog_ref, dsk_ref,
                ng_ref, y_ref, ext_ref, state_ref, *, hpg, hd):
    width = hpg * hd
    n_st = SSD_STATE
    halo = V7X_SUBLANES
    lanes = V7X_LANES

    @pl.when(pl.program_id(1) == 0)
    def _():
        state_ref[...] = jnp.zeros(state_ref.shape, F32)
        ext_ref[0:halo, :] = jnp.zeros((halo, ext_ref.shape[1]), F32)

    ext_ref[halo:halo + CHUNK, 0:width] = xs_ref[...].astype(F32)
    ext_ref[halo:halo + CHUNK, width:width + n_st] = b_ref[...].astype(F32)
    ext_ref[halo:halo + CHUNK, width + n_st:width + 2 * n_st] = c_ref[...].astype(F32)
    acc = jnp.broadcast_to(cb_ref[...], (CHUNK, ext_ref.shape[1]))
    for k in range(SSD_CONV):
        start = halo - (SSD_CONV - 1) + k
        acc = acc + cw_ref[k:k + 1, :] * ext_ref[start:start + CHUNK, :]
    ext_ref[0:halo, :] = ext_ref[CHUNK:CHUNK + halo, :]
    xbc = _silu(acc)
    xs = xbc[:, 0:width]
    bm = xbc[:, width:width + n_st].astype(BF16)
    cm = xbc[:, width + n_st:width + 2 * n_st].astype(BF16)

    dt = _softplus(dt_ref[...] + dtb_ref[...])
    a_dt = dt * (-jnp.exp(alog_ref[...]))
    ri = lax.broadcasted_iota(I32, (CHUNK, CHUNK), 0)
    ci = lax.broadcasted_iota(I32, (CHUNK, CHUNK), 1)
    causal = ri >= ci
    tril = causal.astype(BF16)
    acum = sum(jnp.dot(tril, piece, preferred_element_type=F32) for piece in _split3(a_dt))
    acum_t = acum.T

    def expand(q):
        per = lanes // hd
        cols = []
        for j in range(width // lanes):
            blk = jnp.broadcast_to(q[:, j * per:j * per + 1], (CHUNK, lanes))
            for t in range(1, per):
                nxt = jnp.broadcast_to(q[:, j * per + t:j * per + t + 1], (CHUNK, lanes))
                blk = jnp.where(ci >= t * hd, nxt, blk)
            cols.append(blk)
        return jnp.concatenate(cols, axis=1)

    acum_f = expand(acum)
    dt_f = expand(dt)
    last = acum_f[CHUNK - 1:CHUNK, :]
    xdt = xs * dt_f

    cb = lax.dot_general(cm, bm, (((1,), (1,)), ((), ())), preferred_element_type=F32)
    per = lanes // hd
    y_cols = []
    for j in range(width // lanes):
        x_col = xdt[:, j * lanes:(j + 1) * lanes]
        y_col = jnp.zeros((CHUNK, lanes), F32)
        for t in range(per):
            h = j * per + t
            seg = acum[:, h:h + 1] - acum_t[h:h + 1, :]
            lmat = jnp.exp(jnp.where(causal, seg, -jnp.inf))
            m_h = (cb * lmat).astype(BF16)
            mask = (ci >= t * hd) & (ci < (t + 1) * hd)
            x_h = jnp.where(mask, x_col, 0.0).astype(BF16)
            y_col = y_col + jnp.dot(m_h, x_h, preferred_element_type=F32)
        y_cols.append(y_col)
    y_diag = jnp.concatenate(y_cols, axis=1)

    state = state_ref[...]
    y_off = jnp.dot(cm, state.astype(BF16), preferred_element_type=F32) * jnp.exp(acum_f)
    x_dec = (xdt * jnp.exp(last - acum_f)).astype(BF16)
    s_new = lax.dot_general(bm, x_dec, (((0,), (0,)), ((), ())), preferred_element_type=F32)
    state_ref[...] = state * jnp.exp(last) + s_new

    y = y_diag + y_off + xs * dsk_ref[...]
    yg = y * _silu(z_ref[...].astype(F32))
    yn = yg * lax.rsqrt(jnp.mean(yg * yg, axis=-1, keepdims=True) + EPS) * ng_ref[...]
    y_ref[...] = yn.astype(BF16)


def _ssd(p, dt_raw, conv_w, conv_b, dt_bias, a_log, d_skip, norm_g, off_z, off_xs):
    t_len = p.shape[0]
    heads = dt_bias.shape[0]
    width = norm_g.shape[0]
    hd = width // heads
    groups = (conv_b.shape[0] - width) // (2 * SSD_STATE)
    hpg = heads // groups
    gw = hpg * hd
    n_st = SSD_STATE
    assert V7X_LANES % hd == 0 and gw % V7X_LANES == 0 and hpg <= V7X_LANES
    assert width // groups == gw and t_len % CHUNK == 0
    off_b = off_xs + width
    off_c = off_b + groups * n_st
    assert off_z % gw == 0 and off_xs % gw == 0 and off_b % n_st == 0

    def per_group_lanes(v):
        v = v.reshape(groups, 1, hpg)
        return jnp.pad(v, ((0, 0), (0, 0), (0, V7X_LANES - hpg)))

    dt_g = dt_raw[:, :heads].reshape(t_len, groups, hpg).transpose(1, 0, 2)
    dt_g = jnp.pad(dt_g, ((0, 0), (0, 0), (0, V7X_LANES - hpg)))
    cw_g = jnp.concatenate([
        conv_w[:, :width].reshape(SSD_CONV, groups, gw),
        conv_w[:, width:width + groups * n_st].reshape(SSD_CONV, groups, n_st),
        conv_w[:, width + groups * n_st:].reshape(SSD_CONV, groups, n_st)], axis=-1).transpose(1, 0, 2)
    cb_g = jnp.concatenate([
        conv_b[:width].reshape(groups, 1, gw),
        conv_b[width:width + groups * n_st].reshape(groups, 1, n_st),
        conv_b[width + groups * n_st:].reshape(groups, 1, n_st)], axis=-1)
    conv_cols = gw + 2 * n_st

    grp = lambda g, c: (g, 0, 0)
    return pl.pallas_call(
        functools.partial(_ssd_kernel, hpg=hpg, hd=hd),
        grid=(groups, t_len // CHUNK),
        in_specs=[
            pl.BlockSpec((CHUNK, gw), lambda g, c: (c, off_z // gw + g)),
            pl.BlockSpec((CHUNK, gw), lambda g, c: (c, off_xs // gw + g)),
            pl.BlockSpec((CHUNK, n_st), lambda g, c: (c, off_b // n_st + g)),
            pl.BlockSpec((CHUNK, n_st), lambda g, c: (c, off_c // n_st + g)),
            pl.BlockSpec((None, CHUNK, V7X_LANES), lambda g, c: (g, c, 0)),
            pl.BlockSpec((None, SSD_CONV, conv_cols), grp),
            pl.BlockSpec((None, 1, conv_cols), grp),
            pl.BlockSpec((None, 1, V7X_LANES), grp),
            pl.BlockSpec((None, 1, V7X_LANES), grp),
            pl.BlockSpec((1, gw), lambda g, c: (0, g)),
            pl.BlockSpec((1, gw), lambda g, c: (0, g)),
        ],
        out_specs=pl.BlockSpec((CHUNK, gw), lambda g, c: (c, g)),
        out_shape=jax.ShapeDtypeStruct((t_len, width), BF16),
        scratch_shapes=[
            pltpu.VMEM((CHUNK + V7X_SUBLANES, conv_cols), F32),
            pltpu.VMEM((n_st, gw), F32),
        ],
        compiler_params=_params(("arbitrary", "arbitrary")),
        name="ssd",
    )(p, p, p, p, dt_g, cw_g, cb_g, per_group_lanes(dt_bias), per_group_lanes(a_log),
      jnp.repeat(d_skip, hd).reshape(1, width), norm_g.reshape(1, width))


def _gmlp_kernel(u_ref, v_ref, vg_ref, ws_ref, bs_ref, og_ref, y_ref, yscr_ref, *, heads, hd):
    ri = lax.broadcasted_iota(I32, (CHUNK, CHUNK), 0)
    ci = lax.broadcasted_iota(I32, (CHUNK, CHUNK), 1)
    causal = ri >= ci
    ssq = jnp.zeros((CHUNK, 1), F32)
    for h in range(heads):
        cols = slice(h * hd, (h + 1) * hd)
        u_h = _gelu(u_ref[:, cols].astype(F32))
        v_h = _gelu(v_ref[:, cols].astype(F32))
        v_n = v_h * lax.rsqrt(jnp.mean(v_h * v_h, axis=-1, keepdims=True) + EPS) * vg_ref[:, cols]
        w_h = jnp.where(causal, ws_ref[h], 0.0).astype(BF16)
        sv = jnp.dot(w_h, v_n.astype(BF16), preferred_element_type=F32) + bs_ref[:, h:h + 1]
        y_h = u_h * sv
        yscr_ref[:, cols] = y_h
        ssq = ssq + jnp.sum(y_h * y_h, axis=-1, keepdims=True)
    scale = lax.rsqrt(ssq / (heads * hd) + EPS)
    y_ref[...] = (yscr_ref[...] * scale * og_ref[...]).astype(BF16)


def _gmlp(p, v_norm_g, w_s, b_s, out_norm_g, off_u):
    t_len = p.shape[0]
    heads = w_s.shape[0]
    width = out_norm_g.shape[0]
    hd = width // heads
    assert w_s.shape[1] == CHUNK and off_u % width == 0 and hd % V7X_LANES == 0
    return pl.pallas_call(
        functools.partial(_gmlp_kernel, heads=heads, hd=hd),
        grid=(t_len // CHUNK,),
        in_specs=[
            pl.BlockSpec((CHUNK, width), lambda c: (c, off_u // width)),
            pl.BlockSpec((CHUNK, width), lambda c: (c, off_u // width + 1)),
            pl.BlockSpec((1, width), lambda c: (0, 0)),
            pl.BlockSpec((heads, CHUNK, CHUNK), lambda c: (0, 0, 0)),
            pl.BlockSpec((CHUNK, heads), lambda c: (0, 0)),
            pl.BlockSpec((1, width), lambda c: (0, 0)),
        ],
        out_specs=pl.BlockSpec((CHUNK, width), lambda c: (c, 0)),
        out_shape=jax.ShapeDtypeStruct((t_len, width), BF16),
        scratch_shapes=[pltpu.VMEM((CHUNK, width), F32)],
        compiler_params=_params(("arbitrary",)),
        name="gmlp",
    )(p, p, v_norm_g.reshape(1, width), w_s, b_s.T, out_norm_g.reshape(1, width))


def _outproj_kernel(ys_ref, yg_ref, wa_ref, wb_ref, x_ref, o_ref):
    acc = jnp.dot(ys_ref[...], wa_ref[...], preferred_element_type=F32)
    acc = acc + jnp.dot(yg_ref[...], wb_ref[...], preferred_element_type=F32)
    o_ref[...] = x_ref[...] + acc


def _outproj(y_ssd, y_gmlp, w_out, x):
    t_len, d = x.shape
    ka, kb = y_ssd.shape[1], y_gmlp.shape[1]
    assert ka == kb and w_out.shape[0] == ka + kb
    tm = _largest_tile(t_len, 512, V7X_SUBLANES)
    tn = _largest_tile(d, 512)
    return pl.pallas_call(
        _outproj_kernel,
        grid=(t_len // tm, d // tn),
        in_specs=[
            pl.BlockSpec((tm, ka), lambda i, j: (i, 0)),
            pl.BlockSpec((tm, kb), lambda i, j: (i, 0)),
            pl.BlockSpec((ka, tn), lambda i, j: (0, j)),
            pl.BlockSpec((kb, tn), lambda i, j: (1, j)),
            pl.BlockSpec((tm, tn), lambda i, j: (i, j)),
        ],
        out_specs=pl.BlockSpec((tm, tn), lambda i, j: (i, j)),
        out_shape=jax.ShapeDtypeStruct((t_len, d), F32),
        compiler_params=_params(("arbitrary", "arbitrary")),
        name="outproj",
    )(y_ssd, y_gmlp, w_out, w_out, x)


def _router_kernel(x_ref, g_ref, wr_ref, br_ref, xp_ref, idx_ref, gate_ref, *, n_exp):
    x = x_ref[...]
    tm, d = x.shape
    half = d // 2
    ms = jnp.mean(x * x, axis=-1, keepdims=True)
    xn = x * lax.rsqrt(ms + EPS) * g_ref[...]
    xp_ref[...] = _pack_pair(xn[:, :half], xn[:, half:])

    logits = jnp.dot(xn.astype(BF16), wr_ref[...], preferred_element_type=F32) + br_ref[...]
    lane = lax.broadcasted_iota(I32, (tm, V7X_LANES), 1)
    vals = jnp.where(lane < n_exp, logits, -jnp.inf)
    idx_out = jnp.zeros((tm, V7X_LANES), I32)
    val_out = jnp.full((tm, V7X_LANES), -jnp.inf, F32)
    for k in range(TOP_K):
        m = jnp.max(vals, axis=-1, keepdims=True)
        sel = jnp.min(jnp.where(vals == m, lane, V7X_LANES), axis=-1, keepdims=True)
        idx_out = jnp.where(lane == k, sel, idx_out)
        val_out = jnp.where(lane == k, m, val_out)
        vals = jnp.where(lane == sel, -jnp.inf, vals)
    e = jnp.exp(val_out - jnp.max(val_out, axis=-1, keepdims=True))
    idx_ref[...] = idx_out
    gate_ref[...] = e / jnp.sum(e, axis=-1, keepdims=True)


def _router(x1, g, w_router, b_router):
    t_len, d = x1.shape
    n_exp = w_router.shape[1]
    assert n_exp <= V7X_LANES and TOP_K <= n_exp
    tm = _largest_tile(t_len, 256, V7X_SUBLANES)
    wr = jnp.pad(w_router, ((0, 0), (0, V7X_LANES - n_exp))).astype(BF16)
    br = jnp.pad(b_router, (0, V7X_LANES - n_exp)).reshape(1, V7X_LANES)
    return pl.pallas_call(
        functools.partial(_router_kernel, n_exp=n_exp),
        grid=(t_len // tm,),
        in_specs=[
            pl.BlockSpec((tm, d), lambda i: (i, 0)),
            pl.BlockSpec((1, d), lambda i: (0, 0)),
            pl.BlockSpec((d, V7X_LANES), lambda i: (0, 0)),
            pl.BlockSpec((1, V7X_LANES), lambda i: (0, 0)),
        ],
        out_specs=[
            pl.BlockSpec((tm, d // 2), lambda i: (i, 0)),
            pl.BlockSpec((tm, V7X_LANES), lambda i: (i, 0)),
            pl.BlockSpec((tm, V7X_LANES), lambda i: (i, 0)),
        ],
        out_shape=[
            jax.ShapeDtypeStruct((t_len, d // 2), U32),
            jax.ShapeDtypeStruct((t_len, V7X_LANES), I32),
            jax.ShapeDtypeStruct((t_len, V7X_LANES), F32),
        ],
        compiler_params=_params(("arbitrary",)),
        name="router",
    )(x1, g.reshape(1, d), wr, br)


def _route_plan(top_idx, n_exp):
    t_len, top_k = top_idx.shape
    n_assign = t_len * top_k
    e = top_idx.reshape(n_assign)
    onehot = (e[:, None] == jnp.arange(n_exp, dtype=I32)[None, :]).astype(I32)
    csum = jnp.cumsum(onehot, axis=0)
    rank = jnp.sum(onehot * csum, axis=1) - 1
    counts = csum[-1]
    nb = (counts + MOE_BLK - 1) // MOE_BLK
    sb_end = jnp.cumsum(nb)
    sb_start = sb_end - nb
    dest = sb_start[e] * MOE_BLK + rank
    n_sb = n_assign // MOE_BLK + n_exp
    n_rows = n_sb * MOE_BLK
    row_tok = jnp.zeros((n_rows,), I32).at[dest].set(jnp.arange(n_assign, dtype=I32) // top_k)

    nv = (nb + MOE_NSB - 1) // MOE_NSB
    v_end = jnp.cumsum(nv)
    v_start = v_end - nv
    n_pass = -(-n_sb // MOE_NSB) + n_exp
    vid = jnp.arange(n_pass, dtype=I32)
    ve = jnp.minimum(jnp.searchsorted(v_end, vid, side='right'), n_exp - 1).astype(I32)
    q = vid - v_start[ve]
    valid = vid < v_end[-1]
    vs = jnp.where(valid, sb_start[ve] + q * MOE_NSB, 0).astype(I32)
    vn = jnp.where(valid, jnp.clip(nb[ve] - q * MOE_NSB, 0, MOE_NSB), 0).astype(I32)
    last_e = ve[jnp.maximum(v_end[-1] - 1, 0)]
    ve = jnp.where(valid, ve, last_e).astype(I32)
    used = sb_end[-1:].astype(I32)
    return dict(row_tok=row_tok, dest=dest.astype(I32), ve=ve, vs=vs, vn=vn, used=used, n_rows=n_rows,
                n_pass=n_pass)


def _for_each_paired(n, compute, emit):
    def pair(q, carry):
        first = compute(2 * q)
        second = compute(2 * q + 1)
        emit(2 * q, first)
        emit(2 * q + 1, second)
        return carry

    lax.fori_loop(0, lax.shift_right_logical(n, 1), pair, 0)

    @pl.when((n & 1) == 1)
    def _():
        emit(n - 1, compute(n - 1))


def _zero_tail(obuf, n_used, n_sb, n_j, out_copy):
    tile = obuf.shape[2]
    obuf[0, 0:MOE_BLK, :] = jnp.zeros((MOE_BLK, tile), obuf.dtype)

    def start(sb, carry):
        for jj in range(n_j):
            out_copy(0, 0, pl.multiple_of(sb * MOE_BLK, MOE_BLK), jj * tile).start()
        return carry

    def wait(sb, carry):
        for jj in range(n_j):
            out_copy(0, 0, 0, 0).wait()
        return carry

    lax.fori_loop(n_used, n_sb, start, 0)
    lax.fori_loop(n_used, n_sb, wait, 0)


def _moe_up_kernel(ve_ref, vs_ref, vn_ref, used_ref, tok_ref, xp_hbm, wg_ref, wl_ref, bg_ref, bl_ref, act_hbm,
                   xbuf, wc_ref, obuf, xsem, osem, ocnt, *, tf):
    v = pl.program_id(0)
    j = pl.program_id(1)
    n_pass = pl.num_programs(0)
    n_j = pl.num_programs(1)
    half = xbuf.shape[2]
    slot = v % 2
    n_sub = vn_ref[v]

    def row_copy(tok, sl, r):
        return pltpu.make_async_copy(xp_hbm.at[pl.ds(tok, 1), :], xbuf.at[sl, pl.ds(r, 1), :], xsem.at[sl])

    def gather_start(vv, sl):
        base = vs_ref[vv] * MOE_BLK

        def body(q, carry):
            r0 = pl.multiple_of(q * ISSUE_UNROLL, ISSUE_UNROLL)
            for u in range(ISSUE_UNROLL):
                row_copy(tok_ref[base + r0 + u], sl, r0 + u).start()
            return carry

        lax.fori_loop(0, vn_ref[vv] * (MOE_BLK // ISSUE_UNROLL), body, 0)

    def gather_wait(vv, sl):
        def body(i, carry):
            r0 = pl.multiple_of(i * MOE_BLK, MOE_BLK)
            pltpu.make_async_copy(xp_hbm.at[pl.ds(0, MOE_BLK), :], xbuf.at[sl, pl.ds(r0, MOE_BLK), :],
                                  xsem.at[sl]).wait()
            return carry

        lax.fori_loop(0, vn_ref[vv], body, 0)

    def out_copy(p, r0, row, col):
        return pltpu.make_async_copy(obuf.at[p, pl.ds(r0, MOE_BLK), :],
                                     act_hbm.at[pl.ds(row, MOE_BLK), pl.ds(col, tf)], osem.at[p])

    def drain(p):
        def body(i, carry):
            out_copy(p, 0, 0, 0).wait()
            return carry

        lax.fori_loop(0, ocnt[p], body, 0)
        ocnt[p] = 0

    @pl.when(j == 0)
    def _():
        @pl.when(v == 0)
        def _():
            ocnt[0] = 0
            ocnt[1] = 0
            gather_start(0, 0)

        @pl.when(v + 1 < n_pass)
        def _():
            gather_start(v + 1, 1 - slot)

        gather_wait(v, slot)

    step = v * n_j + j
    p = step % 2
    drain(p)

    @pl.when(n_sub > 0)
    def _():
        wc_ref[:, 0:tf] = wg_ref[...].astype(BF16)
        wc_ref[:, tf:2 * tf] = wl_ref[...].astype(BF16)

        def compute(i):
            r0 = pl.multiple_of(i * MOE_BLK, MOE_BLK)
            lo, hi = _unpack_pair(xbuf[slot, pl.ds(r0, MOE_BLK), :])
            h = jnp.dot(lo.astype(BF16), wc_ref[0:half, :], preferred_element_type=F32)
            h = h + jnp.dot(hi.astype(BF16), wc_ref[half:2 * half, :], preferred_element_type=F32)
            glu = jnp.minimum(h[:, 0:tf] + bg_ref[...], SWIGLU_LIMIT)
            lin = jnp.clip(h[:, tf:2 * tf] + bl_ref[...], -SWIGLU_LIMIT, SWIGLU_LIMIT)
            return (glu * jax.nn.sigmoid(SWIGLU_ALPHA * glu) * (lin + 1.0)).astype(BF16)

        def emit(i, act):
            r0 = pl.multiple_of(i * MOE_BLK, MOE_BLK)
            obuf[p, pl.ds(r0, MOE_BLK), :] = act
            row = pl.multiple_of((vs_ref[v] + i) * MOE_BLK, MOE_BLK)
            out_copy(p, r0, row, pl.multiple_of(j * tf, tf)).start()

        _for_each_paired(n_sub, compute, emit)
        ocnt[p] = n_sub

    @pl.when(step == n_pass * n_j - 1)
    def _():
        drain(0)
        drain(1)
        _zero_tail(obuf, used_ref[0], act_hbm.shape[0] // MOE_BLK, n_j, out_copy)


def _moe_up(xp, plan, w_gate_up, b_gate_up):
    n_exp, d, two_f = w_gate_up.shape
    f = two_f // 2
    half = xp.shape[1]
    assert half * 2 == d
    tf = _largest_tile(f, 256)
    n_j = f // tf
    rows_pass = MOE_NSB * MOE_BLK

    def w_map(off):
        def index_map(v, j, ve, vs, vn, used, tok):
            jj = jnp.where(vn[v] > 0, j, n_j - 1)
            return (ve[v], 0, off + jj)
        return index_map

    grid_spec = pltpu.PrefetchScalarGridSpec(
        num_scalar_prefetch=5,
        grid=(plan["n_pass"], n_j),
        in_specs=[
            pl.BlockSpec(memory_space=pl.ANY),
            pl.BlockSpec((None, d, tf), w_map(0)),
            pl.BlockSpec((None, d, tf), w_map(n_j)),
            pl.BlockSpec((None, 1, tf), w_map(0)),
            pl.BlockSpec((None, 1, tf), w_map(n_j)),
        ],
        out_specs=pl.BlockSpec(memory_space=pl.ANY),
        scratch_shapes=[
            pltpu.VMEM((2, rows_pass, half), U32),
            pltpu.VMEM((d, 2 * tf), BF16),
            pltpu.VMEM((2, rows_pass, tf), BF16),
            pltpu.SemaphoreType.DMA((2,)),
            pltpu.SemaphoreType.DMA((2,)),
            pltpu.SMEM((2,), I32),
        ],
    )
    b3 = b_gate_up.reshape(n_exp, 1, two_f)
    return pl.pallas_call(
        functools.partial(_moe_up_kernel, tf=tf),
        grid_spec=grid_spec,
        out_shape=jax.ShapeDtypeStruct((plan["n_rows"], f), BF16),
        compiler_params=_params(("arbitrary", "arbitrary")),
        name="moe_up",
    )(plan["ve"], plan["vs"], plan["vn"], plan["used"], plan["row_tok"], xp, w_gate_up, w_gate_up, b3, b3)


def _moe_down_kernel(ve_ref, vs_ref, vn_ref, used_ref, act_hbm, wa_ref, wb_ref, ba_ref, bb_ref, y_hbm,
                     abuf, wc_ref, obuf, asem, osem, ocnt, *, tn):
    v = pl.program_id(0)
    j = pl.program_id(1)
    n_pass = pl.num_programs(0)
    n_j = pl.num_programs(1)
    slot = v % 2
    n_sub = vn_ref[v]

    def in_copy(row, sl, r0):
        return pltpu.make_async_copy(act_hbm.at[pl.ds(row, MOE_BLK), :],
                                     abuf.at[sl, pl.ds(r0, MOE_BLK), :], asem.at[sl])

    def load_start(vv, sl):
        def body(i, carry):
            r0 = pl.multiple_of(i * MOE_BLK, MOE_BLK)
            in_copy(pl.multiple_of((vs_ref[vv] + i) * MOE_BLK, MOE_BLK), sl, r0).start()
            return carry

        lax.fori_loop(0, vn_ref[vv], body, 0)

    def load_wait(vv, sl):
        def body(i, carry):
            in_copy(0, sl, pl.multiple_of(i * MOE_BLK, MOE_BLK)).wait()
            return carry

        lax.fori_loop(0, vn_ref[vv], body, 0)

    def out_copy(p, r0, row, col):
        return pltpu.make_async_copy(obuf.at[p, pl.ds(r0, MOE_BLK), :],
                                     y_hbm.at[pl.ds(row, MOE_BLK), pl.ds(col, tn)], osem.at[p])

    def drain(p):
        def body(i, carry):
            out_copy(p, 0, 0, 0).wait()
            return carry

        lax.fori_loop(0, ocnt[p], body, 0)
        ocnt[p] = 0

    @pl.when(j == 0)
    def _():
        @pl.when(v == 0)
        def _():
            ocnt[0] = 0
            ocnt[1] = 0
            load_start(0, 0)

        @pl.when(v + 1 < n_pass)
        def _():
            load_start(v + 1, 1 - slot)

        load_wait(v, slot)

    step = v * n_j + j
    p = step % 2
    drain(p)

    @pl.when(n_sub > 0)
    def _():
        wc_ref[:, 0:tn] = wa_ref[...].astype(BF16)
        wc_ref[:, tn:2 * tn] = wb_ref[...].astype(BF16)

        def compute(i):
            r0 = pl.multiple_of(i * MOE_BLK, MOE_BLK)
            y = jnp.dot(abuf[slot, pl.ds(r0, MOE_BLK), :], wc_ref[...], preferred_element_type=F32)
            return _pack_pair(y[:, 0:tn] + ba_ref[...], y[:, tn:2 * tn] + bb_ref[...])

        def emit(i, packed):
            r0 = pl.multiple_of(i * MOE_BLK, MOE_BLK)
            obuf[p, pl.ds(r0, MOE_BLK), :] = packed
            row = pl.multiple_of((vs_ref[v] + i) * MOE_BLK, MOE_BLK)
            out_copy(p, r0, row, pl.multiple_of(j * tn, tn)).start()

        _for_each_paired(n_sub, compute, emit)
        ocnt[p] = n_sub

    @pl.when(step == n_pass * n_j - 1)
    def _():
        drain(0)
        drain(1)
        _zero_tail(obuf, used_ref[0], y_hbm.shape[0] // MOE_BLK, n_j, out_copy)


def _moe_down(act, plan, w_down, b_down):
    n_exp, f, d = w_down.shape
    half = d // 2
    tn = _largest_tile(half, 256)
    n_j = half // tn
    rows_pass = MOE_NSB * MOE_BLK

    def w_map(off):
        def index_map(v, j, ve, vs, vn, used):
            jj = jnp.where(vn[v] > 0, j, n_j - 1)
            return (ve[v], 0, off + jj)
        return index_map

    grid_spec = pltpu.PrefetchScalarGridSpec(
        num_scalar_prefetch=4,
        grid=(plan["n_pass"], n_j),
        in_specs=[
            pl.BlockSpec(memory_space=pl.ANY),
            pl.BlockSpec((None, f, tn), w_map(0)),
            pl.BlockSpec((None, f, tn), w_map(n_j)),
            pl.BlockSpec((None, 1, tn), w_map(0)),
            pl.BlockSpec((None, 1, tn), w_map(n_j)),
        ],
        out_specs=pl.BlockSpec(memory_space=pl.ANY),
        scratch_shapes=[
            pltpu.VMEM((2, rows_pass, f), BF16),
            pltpu.VMEM((f, 2 * tn), BF16),
            pltpu.VMEM((2, rows_pass, tn), U32),
            pltpu.SemaphoreType.DMA((2,)),
            pltpu.SemaphoreType.DMA((2,)),
            pltpu.SMEM((2,), I32),
        ],
    )
    b3 = b_down.reshape(n_exp, 1, d)
    return pl.pallas_call(
        functools.partial(_moe_down_kernel, tn=tn),
        grid_spec=grid_spec,
        out_shape=jax.ShapeDtypeStruct((plan["n_rows"], half), U32),
        compiler_params=_params(("arbitrary", "arbitrary")),
        name="moe_down",
    )(plan["ve"], plan["vs"], plan["vn"], plan["used"], act, w_down, w_down, b3, b3)


def _combine_kernel(dest_ref, y_hbm, gate_ref, x_ref, g_ref, o_ref, ybuf, sem):
    i = pl.program_id(0)
    n_i = pl.num_programs(0)
    tm, d = x_ref.shape
    half = d // 2
    slot = i % 2

    def row_copy(row, sl, k, t):
        return pltpu.make_async_copy(y_hbm.at[pl.ds(row, 1), :], ybuf.at[sl, k, pl.ds(t, 1), :], sem.at[sl])

    def gather_start(ii, sl):
        def body(q, carry):
            t0 = pl.multiple_of(q * ISSUE_UNROLL, ISSUE_UNROLL)
            for u in range(ISSUE_UNROLL):
                for k in range(TOP_K):
                    row_copy(dest_ref[(ii * tm + t0 + u) * TOP_K + k], sl, k, t0 + u).start()
            return carry

        lax.fori_loop(0, tm // ISSUE_UNROLL, body, 0)

    def gather_wait(sl):
        for k in range(TOP_K):
            pltpu.make_async_copy(y_hbm.at[pl.ds(0, tm), :], ybuf.at[sl, k], sem.at[sl]).wait()

    @pl.when(i == 0)
    def _():
        gather_start(0, 0)

    @pl.when(i + 1 < n_i)
    def _():
        gather_start(i + 1, 1 - slot)

    gather_wait(slot)

    acc_lo = x_ref[:, 0:half]
    acc_hi = x_ref[:, half:d]
    for k in range(TOP_K):
        lo, hi = _unpack_pair(ybuf[slot, k])
        gk = gate_ref[:, k:k + 1]
        acc_lo = acc_lo + gk * lo
        acc_hi = acc_hi + gk * hi
    ms = (jnp.sum(acc_lo * acc_lo, axis=-1, keepdims=True)
          + jnp.sum(acc_hi * acc_hi, axis=-1, keepdims=True)) / d
    r = lax.rsqrt(ms + EPS)
    o_ref[:, 0:half] = acc_lo * r * g_ref[:, 0:half]
    o_ref[:, half:d] = acc_hi * r * g_ref[:, half:d]


def _combine(y_rows, dest, gate, x1, final_g):
    t_len, d = x1.shape
    half = d // 2
    tm = _largest_tile(t_len, 128, V7X_SUBLANES)
    grid_spec = pltpu.PrefetchScalarGridSpec(
        num_scalar_prefetch=1,
        grid=(t_len // tm,),
        in_specs=[
            pl.BlockSpec(memory_space=pl.ANY),
            pl.BlockSpec((tm, V7X_LANES), lambda i, dest: (i, 0)),
            pl.BlockSpec((tm, d), lambda i, dest: (i, 0)),
            pl.BlockSpec((1, d), lambda i, dest: (0, 0)),
        ],
        out_specs=pl.BlockSpec((tm, d), lambda i, dest: (i, 0)),
        scratch_shapes=[
            pltpu.VMEM((2, TOP_K, tm, half), U32),
            pltpu.SemaphoreType.DMA((2,)),
        ],
    )
    return pl.pallas_call(
        _combine_kernel,
        grid_spec=grid_spec,
        out_shape=jax.ShapeDtypeStruct((t_len, d), F32),
        compiler_params=_params(("arbitrary",)),
        name="combine",
    )(dest, y_rows, gate, x1, final_g.reshape(1, d))


def kernel(x, mix_norm_g, w_in, ssd_conv_w, ssd_conv_b, ssd_dt_bias, ssd_a_log, ssd_d, ssd_norm_g,
           gmlp_v_norm_g, gmlp_w_s, gmlp_b_s, gmlp_out_norm_g, w_out, ffn_norm_g, w_router, b_router,
           w_gate_up, b_gate_up, w_down, b_down, final_norm_g):
    bsz, t_len, d = x.shape
    assert bsz == 1
    depth = w_in.shape[0]
    ssd_w = ssd_norm_g.shape[-1]
    heads = ssd_dt_bias.shape[-1]
    gmlp_w = gmlp_out_norm_g.shape[-1]
    conv_dim = ssd_conv_b.shape[-1]
    assert heads <= V7X_LANES
    c_dt = ssd_w + conv_dim
    c_uv = c_dt + heads
    off_u, off_z, off_xs = 0, 2 * gmlp_w, 2 * gmlp_w + ssd_w

    h = x.reshape(t_len, d)
    out = None
    for layer in range(depth):
        w = w_in[layer]
        w_main = jnp.concatenate([w[:, c_uv:].astype(BF16), w[:, :c_dt].astype(BF16)], axis=1)
        w_dt = jnp.pad(w[:, c_dt:c_uv], ((0, 0), (0, V7X_LANES - heads))).astype(BF16)
        p, dt_raw = _inproj(h, mix_norm_g[layer], w_main, w_dt)
        y_ssd = _ssd(p, dt_raw, ssd_conv_w[layer], ssd_conv_b[layer], ssd_dt_bias[layer],
                     ssd_a_log[layer], ssd_d[layer], ssd_norm_g[layer], off_z, off_xs)
        y_gmlp = _gmlp(p, gmlp_v_norm_g[layer], gmlp_w_s[layer], gmlp_b_s[layer],
                       gmlp_out_norm_g[layer], off_u)
        x1 = _outproj(y_ssd, y_gmlp, w_out[layer].astype(BF16), h)

        xp, idx_pad, gate_pad = _router(x1, ffn_norm_g[layer], w_router[layer], b_router[layer])
        plan = _route_plan(idx_pad[:, :TOP_K], w_router.shape[-1])
        act = _moe_up(xp, plan, w_gate_up[layer], b_gate_up[layer])
        y_rows = _moe_down(act, plan, w_down[layer], b_down[layer])
        last = layer == depth - 1
        assert last, "multi-layer stacking needs an un-normalised combine"
        out = _combine(y_rows, plan["dest"], gate_pad, x1, final_norm_g)
    return out.reshape(bsz, t_len, d)
```

```python
import functools
import math

import jax
import jax.numpy as jnp
from jax import lax
from jax.experimental import pallas as pl
from jax.experimental.pallas import tpu as pltpu

F32, BF16, U32, I32 = jnp.float32, jnp.bfloat16, jnp.uint32, jnp.int32

EPS = 1e-5
SSD_STATE = 128
SSD_CONV = 4
CHUNK = 128
TOP_K = 4
SSD_ROWS = 512
SWIGLU_ALPHA = 1.702
SWIGLU_LIMIT = 7.0

V7X_LANES = 128
V7X_SUBLANES = 8
V7X_VMEM_BYTES = 64 * 1024 * 1024
VMEM_LIMIT = V7X_VMEM_BYTES - 8 * 1024 * 1024

MOE_BLK = 256
MOE_NSB = 6
ISSUE_UNROLL = 8
HI_MASK = 0xFFFF0000


_gcd = math.gcd


def _largest_tile(total, cap, quantum=V7X_LANES):
    best = None
    t = quantum
    while t <= min(total, cap):
        if total % t == 0:
            best = t
        t += quantum
    assert best is not None, (total, cap, quantum)
    return best


def _params(semantics):
    return pltpu.CompilerParams(dimension_semantics=semantics, vmem_limit_bytes=VMEM_LIMIT)


def _silu(x):
    return x * jax.nn.sigmoid(x)


def _gelu(x):
    return 0.5 * x * (1.0 + lax.erf(x * (2.0 ** -0.5)))


def _softplus(x):
    return jnp.maximum(x, 0.0) + jnp.log1p(jnp.exp(-jnp.abs(x)))


def _pack_pair(lo, hi):
    lo_bits = lax.bitcast_convert_type(lo.astype(BF16).astype(F32), U32)
    hi_bits = lax.bitcast_convert_type(hi.astype(BF16).astype(F32), U32)
    return (lo_bits >> 16) | (hi_bits & jnp.uint32(HI_MASK))


def _unpack_pair(u):
    lo = lax.bitcast_convert_type(u << 16, F32)
    hi = lax.bitcast_convert_type(u & jnp.uint32(HI_MASK), F32)
    return lo, hi


NT_DIMS = (((1,), (1,)), ((), ()))


def _inproj_kernel(x_hbm, g_ref, w_ref, wdt_ref, cw_ref, cb_ref, p_ref, dt_ref, xbuf, xn_ref, halo_ref, sem,
                   *, n_gelu, n_silu):
    i = pl.program_id(0)
    j = pl.program_id(1)
    tm = xbuf.shape[0]
    rows = _largest_tile(tm, 128, V7X_SUBLANES)
    tail = halo_ref.shape[1]

    def x_copy(ii):
        return pltpu.make_async_copy(x_hbm.at[pl.ds(pl.multiple_of(ii * tm, tm), tm), :], xbuf, sem.at[0])

    @pl.when(j == 0)
    def _():
        @pl.when(i == 0)
        def _():
            x_copy(0).start()
            halo_ref[...] = jnp.zeros(halo_ref.shape, F32)

        x_copy(i).wait()

        def norm(r, carry):
            sl = pl.ds(pl.multiple_of(r * rows, rows), rows)
            x = xbuf[sl, :]
            ms = jnp.mean(x * x, axis=-1, keepdims=True)
            xn = (x * lax.rsqrt(ms + EPS) * g_ref[...]).astype(BF16)
            xn_ref[sl, :] = xn
            dt_ref[sl, :] = lax.dot_general(xn, wdt_ref[...], NT_DIMS, preferred_element_type=F32)
            return carry

        lax.fori_loop(0, tm // rows, norm, 0)

        @pl.when(i + 1 < pl.num_programs(0))
        def _():
            x_copy(i + 1).start()

    def project():
        return lax.dot_general(xn_ref[...], w_ref[...], NT_DIMS, preferred_element_type=F32)

    @pl.when(j < n_gelu)
    def _():
        p_ref[...] = _gelu(project()).astype(BF16)

    @pl.when((j >= n_gelu) & (j < n_gelu + n_silu))
    def _():
        p_ref[...] = _silu(project()).astype(BF16)

    @pl.when(j >= n_gelu + n_silu)
    def _():
        x = project()
        jc = j - (n_gelu + n_silu)
        ext = jnp.concatenate([halo_ref[jc], x], axis=0)
        acc = cb_ref[...] + cw_ref[SSD_CONV - 1:SSD_CONV, :] * ext
        for s in range(1, SSD_CONV):
            acc = acc + cw_ref[SSD_CONV - 1 - s:SSD_CONV - s, :] * pltpu.roll(ext, s, axis=0)
        p_ref[...] = _silu(acc[tail:, :]).astype(BF16)
        halo_ref[jc] = x[tm - tail:tm, :]


def _inproj(x, g, w_t, w_dt, conv_w, conv_b, n_gelu_cols, n_silu_cols, row_gelu):
    t_len, d = x.shape
    n_conv_cols = conv_b.shape[0]
    n_out = n_gelu_cols + n_silu_cols + n_conv_cols
    assert row_gelu + n_gelu_cols == w_t.shape[0] and row_gelu >= n_silu_cols + n_conv_cols
    tm = _largest_tile(t_len, 1024, V7X_SUBLANES)
    tn = _largest_tile(_gcd(_gcd(n_gelu_cols, n_silu_cols), n_conv_cols), 512)
    n_gelu, n_silu, n_conv = n_gelu_cols // tn, n_silu_cols // tn, n_conv_cols // tn
    tail = 2 * V7X_SUBLANES
    assert row_gelu % tail == 0 and tn % tail == 0
    conv_map = lambda i, j: (0, jnp.maximum(j - (n_gelu + n_silu), 0))
    w_map = lambda i, j: (
        tail * jnp.where(j < n_gelu, row_gelu // tail + j * (tn // tail), (j - n_gelu) * (tn // tail)), 0)
    return pl.pallas_call(
        functools.partial(_inproj_kernel, n_gelu=n_gelu, n_silu=n_silu),
        grid=(t_len // tm, n_out // tn),
        in_specs=[
            pl.BlockSpec(memory_space=pl.ANY),
            pl.BlockSpec((1, d), lambda i, j: (0, 0)),
            pl.BlockSpec((pl.Element(tn), pl.Element(d)), w_map),
            pl.BlockSpec((V7X_LANES, d), lambda i, j: (0, 0)),
            pl.BlockSpec((SSD_CONV, tn), conv_map),
            pl.BlockSpec((1, tn), conv_map),
        ],
        out_specs=[
            pl.BlockSpec((tm, tn), lambda i, j: (i, j)),
            pl.BlockSpec((tm, V7X_LANES), lambda i, j: (i, 0)),
        ],
        out_shape=[
            jax.ShapeDtypeStruct((t_len, n_out), BF16),
            jax.ShapeDtypeStruct((t_len, V7X_LANES), F32),
        ],
        scratch_shapes=[
            pltpu.VMEM((tm, d), F32),
            pltpu.VMEM((tm, d), BF16),
            pltpu.VMEM((n_conv, tail, tn), F32),
            pltpu.SemaphoreType.DMA((1,)),
        ],
        compiler_params=_params(("arbitrary", "arbitrary")),
        name="inproj",
    )(x, g.reshape(1, d), w_t, w_dt, conv_w, conv_b.reshape(1, n_conv_cols))


def _split3(x):
    hi = x.astype(BF16)
    r1 = x - hi.astype(F32)
    mid = r1.astype(BF16)
    lo = (r1 - mid.astype(F32)).astype(BF16)
    return hi, mid, lo


def _ssd_kernel(sz_ref, xs_ref, b_ref, c_ref, dt_ref, dtb_ref, alog_ref, dsk_ref, ng_ref, y_ref, state_ref,
                *, hpg, hd):
    width = hpg * hd
    lanes = V7X_LANES
    per = lanes // hd

    @pl.when(pl.program_id(1) == 0)
    def _():
        state_ref[...] = jnp.zeros(state_ref.shape, F32)

    ri = lax.broadcasted_iota(I32, (CHUNK, CHUNK), 0)
    ci = lax.broadcasted_iota(I32, (CHUNK, CHUNK), 1)
    causal = ri >= ci
    tril = causal.astype(BF16)
    neg_a = -jnp.exp(alog_ref[...])

    def expand(q):
        cols = []
        for j in range(width // lanes):
            blk = jnp.broadcast_to(q[:, j * per:j * per + 1], (CHUNK, lanes))
            for t in range(1, per):
                nxt = jnp.broadcast_to(q[:, j * per + t:j * per + t + 1], (CHUNK, lanes))
                blk = jnp.where(ci >= t * hd, nxt, blk)
            cols.append(blk)
        return jnp.concatenate(cols, axis=1)

    def chunk(k, carry):
        rows = pl.ds(pl.multiple_of(k * CHUNK, CHUNK), CHUNK)
        xs = xs_ref[rows, :].astype(F32)
        bm = b_ref[rows, :]
        cm = c_ref[rows, :]

        dt = _softplus(dt_ref[rows, :] + dtb_ref[...])
        a_dt = dt * neg_a
        acum = sum(jnp.dot(tril, piece, preferred_element_type=F32) for piece in _split3(a_dt))
        acum_t = acum.T
        acum_f = expand(acum)
        last = acum_f[CHUNK - 1:CHUNK, :]
        xdt = xs * expand(dt)

        cb = lax.dot_general(cm, bm, (((1,), (1,)), ((), ())), preferred_element_type=F32)
        y_cols = []
        for j in range(width // lanes):
            x_col = xdt[:, j * lanes:(j + 1) * lanes]
            y_col = jnp.zeros((CHUNK, lanes), F32)
            for t in range(per):
                h = j * per + t
                seg = acum[:, h:h + 1] - acum_t[h:h + 1, :]
                lmat = jnp.exp(jnp.where(causal, seg, -jnp.inf))
                m_h = (cb * lmat).astype(BF16)
                mask = (ci >= t * hd) & (ci < (t + 1) * hd)
                x_h = jnp.where(mask, x_col, 0.0).astype(BF16)
                y_col = y_col + jnp.dot(m_h, x_h, preferred_element_type=F32)
            y_cols.append(y_col)
        y_diag = jnp.concatenate(y_cols, axis=1)

        state = state_ref[...]
        y_off = jnp.dot(cm, state.astype(BF16), preferred_element_type=F32) * jnp.exp(acum_f)
        x_dec = (xdt * jnp.exp(last - acum_f)).astype(BF16)
        s_new = lax.dot_general(bm, x_dec, (((0,), (0,)), ((), ())), preferred_element_type=F32)
        state_ref[...] = state * jnp.exp(last) + s_new

        yg = (y_diag + y_off + xs * dsk_ref[...]) * sz_ref[rows, :].astype(F32)
        yn = yg * lax.rsqrt(jnp.mean(yg * yg, axis=-1, keepdims=True) + EPS) * ng_ref[...]
        y_ref[rows, :] = yn.astype(BF16)
        return carry

    lax.fori_loop(0, y_ref.shape[0] // CHUNK, chunk, 0)


def _ssd(p, dt_raw, dt_bias, a_log, d_skip, norm_g, groups, off_z, off_xs):
    t_len = p.shape[0]
    heads = dt_bias.shape[0]
    width = norm_g.shape[0]
    hd = width // heads
    hpg = heads // groups
    gw = hpg * hd
    n_st = SSD_STATE
    rows = _largest_tile(t_len, SSD_ROWS, CHUNK)
    assert V7X_LANES % hd == 0 and gw % V7X_LANES == 0 and hpg <= V7X_LANES
    assert width // groups == gw
    off_b = off_xs + width
    off_c = off_b + groups * n_st
    assert off_z % gw == 0 and off_xs % gw == 0 and off_b % n_st == 0

    def per_group_lanes(v):
        v = v.reshape(groups, 1, hpg)
        return jnp.pad(v, ((0, 0), (0, 0), (0, V7X_LANES - hpg)))

    dt_g = dt_raw[:, :heads].reshape(t_len, groups, hpg).transpose(1, 0, 2)
    dt_g = jnp.pad(dt_g, ((0, 0), (0, 0), (0, V7X_LANES - hpg)))

    grp = lambda g, c: (g, 0, 0)
    return pl.pallas_call(
        functools.partial(_ssd_kernel, hpg=hpg, hd=hd),
        grid=(groups, t_len // rows),
        in_specs=[
            pl.BlockSpec((rows, gw), lambda g, c: (c, off_z // gw + g)),
            pl.BlockSpec((rows, gw), lambda g, c: (c, off_xs // gw + g)),
            pl.BlockSpec((rows, n_st), lambda g, c: (c, off_b // n_st + g)),
            pl.BlockSpec((rows, n_st), lambda g, c: (c, off_c // n_st + g)),
            pl.BlockSpec((None, rows, V7X_LANES), lambda g, c: (g, c, 0)),
            pl.BlockSpec((None, 1, V7X_LANES), grp),
            pl.BlockSpec((None, 1, V7X_LANES), grp),
            pl.BlockSpec((1, gw), lambda g, c: (0, g)),
            pl.BlockSpec((1, gw), lambda g, c: (0, g)),
        ],
        out_specs=pl.BlockSpec((rows, gw), lambda g, c: (c, g)),
        out_shape=jax.ShapeDtypeStruct((t_len, width), BF16),
        scratch_shapes=[pltpu.VMEM((n_st, gw), F32)],
        compiler_params=_params(("arbitrary", "arbitrary")),
        name="ssd",
    )(p, p, p, p, dt_g, per_group_lanes(dt_bias), per_group_lanes(a_log),
      jnp.repeat(d_skip, hd).reshape(1, width), norm_g.reshape(1, width))


def _gmlp_kernel(u_ref, v_ref, vg_ref, ws_ref, bs_ref, og_ref, y_ref, yscr_ref, *, heads, hd):
    ri = lax.broadcasted_iota(I32, (CHUNK, CHUNK), 0)
    ci = lax.broadcasted_iota(I32, (CHUNK, CHUNK), 1)
    causal = ri >= ci
    ssq = jnp.zeros((CHUNK, 1), F32)
    for h in range(heads):
        cols = slice(h * hd, (h + 1) * hd)
        u_h = u_ref[:, cols].astype(F32)
        v_h = v_ref[:, cols].astype(F32)
        v_n = v_h * lax.rsqrt(jnp.mean(v_h * v_h, axis=-1, keepdims=True) + EPS) * vg_ref[:, cols]
        w_h = jnp.where(causal, ws_ref[h], 0.0).astype(BF16)
        sv = jnp.dot(w_h, v_n.astype(BF16), preferred_element_type=F32) + bs_ref[:, h:h + 1]
        y_h = u_h * sv
        yscr_ref[:, cols] = y_h
        ssq = ssq + jnp.sum(y_h * y_h, axis=-1, keepdims=True)
    scale = lax.rsqrt(ssq / (heads * hd) + EPS)
    y_ref[...] = (yscr_ref[...] * scale * og_ref[...]).astype(BF16)


def _gmlp(p, v_norm_g, w_s, b_s, out_norm_g, off_u):
    t_len = p.shape[0]
    heads = w_s.shape[0]
    width = out_norm_g.shape[0]
    hd = width // heads
    assert w_s.shape[1] == CHUNK and off_u % width == 0 and hd % V7X_LANES == 0
    return pl.pallas_call(
        functools.partial(_gmlp_kernel, heads=heads, hd=hd),
        grid=(t_len // CHUNK,),
        in_specs=[
            pl.BlockSpec((CHUNK, width), lambda c: (c, off_u // width)),
            pl.BlockSpec((CHUNK, width), lambda c: (c, off_u // width + 1)),
            pl.BlockSpec((1, width), lambda c: (0, 0)),
            pl.BlockSpec((heads, CHUNK, CHUNK), lambda c: (0, 0, 0)),
            pl.BlockSpec((CHUNK, heads), lambda c: (0, 0)),
            pl.BlockSpec((1, width), lambda c: (0, 0)),
        ],
        out_specs=pl.BlockSpec((CHUNK, width), lambda c: (c, 0)),
        out_shape=jax.ShapeDtypeStruct((t_len, width), BF16),
        scratch_shapes=[pltpu.VMEM((CHUNK, width), F32)],
        compiler_params=_params(("arbitrary",)),
        name="gmlp",
    )(p, p, v_norm_g.reshape(1, width), w_s, b_s.T, out_norm_g.reshape(1, width))


def _outproj_kernel(ys_ref, yg_ref, wa_ref, wb_ref, x_ref, o_ref):
    acc = jnp.dot(ys_ref[...], wa_ref[...], preferred_element_type=F32)
    acc = acc + jnp.dot(yg_ref[...], wb_ref[...], preferred_element_type=F32)
    o_ref[...] = x_ref[...] + acc


def _outproj(y_ssd, y_gmlp, w_out, x):
    t_len, d = x.shape
    ka, kb = y_ssd.shape[1], y_gmlp.shape[1]
    assert ka == kb and w_out.shape[0] == ka + kb
    tm = _largest_tile(t_len, 512, V7X_SUBLANES)
    tn = _largest_tile(d, 512)
    return pl.pallas_call(
        _outproj_kernel,
        grid=(t_len // tm, d // tn),
        in_specs=[
            pl.BlockSpec((tm, ka), lambda i, j: (i, 0)),
            pl.BlockSpec((tm, kb), lambda i, j: (i, 0)),
            pl.BlockSpec((ka, tn), lambda i, j: (0, j)),
            pl.BlockSpec((kb, tn), lambda i, j: (1, j)),
            pl.BlockSpec((tm, tn), lambda i, j: (i, j)),
        ],
        out_specs=pl.BlockSpec((tm, tn), lambda i, j: (i, j)),
        out_shape=jax.ShapeDtypeStruct((t_len, d), F32),
        compiler_params=_params(("arbitrary", "arbitrary")),
        name="outproj",
    )(y_ssd, y_gmlp, w_out, w_out, x)


def _router_kernel(x_ref, g_ref, wr_ref, br_ref, xp_ref, idx_ref, gate_ref, *, n_exp):
    x = x_ref[...]
    tm, d = x.shape
    half = d // 2
    ms = jnp.mean(x * x, axis=-1, keepdims=True)
    xn = x * lax.rsqrt(ms + EPS) * g_ref[...]
    xp_ref[...] = _pack_pair(xn[:, :half], xn[:, half:])

    logits = jnp.dot(xn.astype(BF16), wr_ref[...], preferred_element_type=F32) + br_ref[...]
    lane = lax.broadcasted_iota(I32, (tm, V7X_LANES), 1)
    vals = jnp.where(lane < n_exp, logits, -jnp.inf)
    idx_out = jnp.zeros((tm, V7X_LANES), I32)
    val_out = jnp.full((tm, V7X_LANES), -jnp.inf, F32)
    for k in range(TOP_K):
        m = jnp.max(vals, axis=-1, keepdims=True)
        sel = jnp.min(jnp.where(vals == m, lane, V7X_LANES), axis=-1, keepdims=True)
        idx_out = jnp.where(lane == k, sel, idx_out)
        val_out = jnp.where(lane == k, m, val_out)
        vals = jnp.where(lane == sel, -jnp.inf, vals)
    e = jnp.exp(val_out - jnp.max(val_out, axis=-1, keepdims=True))
    idx_ref[...] = idx_out
    gate_ref[...] = e / jnp.sum(e, axis=-1, keepdims=True)


def _router(x1, g, w_router, b_router):
    t_len, d = x1.shape
    n_exp = w_router.shape[1]
    assert n_exp <= V7X_LANES and TOP_K <= n_exp
    tm = _largest_tile(t_len, 256, V7X_SUBLANES)
    wr = jnp.pad(w_router, ((0, 0), (0, V7X_LANES - n_exp))).astype(BF16)
    br = jnp.pad(b_router, (0, V7X_LANES - n_exp)).reshape(1, V7X_LANES)
    return pl.pallas_call(
        functools.partial(_router_kernel, n_exp=n_exp),
        grid=(t_len // tm,),
        in_specs=[
            pl.BlockSpec((tm, d), lambda i: (i, 0)),
            pl.BlockSpec((1, d), lambda i: (0, 0)),
            pl.BlockSpec((d, V7X_LANES), lambda i: (0, 0)),
            pl.BlockSpec((1, V7X_LANES), lambda i: (0, 0)),
        ],
        out_specs=[
            pl.BlockSpec((tm, d // 2), lambda i: (i, 0)),
            pl.BlockSpec((tm, V7X_LANES), lambda i: (i, 0)),
            pl.BlockSpec((tm, V7X_LANES), lambda i: (i, 0)),
        ],
        out_shape=[
            jax.ShapeDtypeStruct((t_len, d // 2), U32),
            jax.ShapeDtypeStruct((t_len, V7X_LANES), I32),
            jax.ShapeDtypeStruct((t_len, V7X_LANES), F32),
        ],
        compiler_params=_params(("arbitrary",)),
        name="router",
    )(x1, g.reshape(1, d), wr, br)


def _route_plan(top_idx, n_exp):
    t_len, top_k = top_idx.shape
    n_assign = t_len * top_k
    e = top_idx.reshape(n_assign)
    onehot = (e[:, None] == jnp.arange(n_exp, dtype=I32)[None, :]).astype(I32)
    csum = jnp.cumsum(onehot, axis=0)
    rank = jnp.sum(onehot * csum, axis=1) - 1
    counts = csum[-1]
    nb = (counts + MOE_BLK - 1) // MOE_BLK
    sb_end = jnp.cumsum(nb)
    sb_start = sb_end - nb
    dest = sb_start[e] * MOE_BLK + rank
    n_sb = n_assign // MOE_BLK + n_exp
    n_rows = n_sb * MOE_BLK
    row_tok = jnp.zeros((n_rows,), I32).at[dest].set(jnp.arange(n_assign, dtype=I32) // top_k)

    nv = (nb + MOE_NSB - 1) // MOE_NSB
    v_end = jnp.cumsum(nv)
    v_start = v_end - nv
    n_pass = -(-n_sb // MOE_NSB) + n_exp
    vid = jnp.arange(n_pass, dtype=I32)
    ve = jnp.minimum(jnp.searchsorted(v_end, vid, side='right'), n_exp - 1).astype(I32)
    q = vid - v_start[ve]
    valid = vid < v_end[-1]
    vs = jnp.where(valid, sb_start[ve] + q * MOE_NSB, 0).astype(I32)
    vn = jnp.where(valid, jnp.clip(nb[ve] - q * MOE_NSB, 0, MOE_NSB), 0).astype(I32)
    last_e = ve[jnp.maximum(v_end[-1] - 1, 0)]
    ve = jnp.where(valid, ve, last_e).astype(I32)
    used = sb_end[-1:].astype(I32)
    return dict(row_tok=row_tok, dest=dest.astype(I32), ve=ve, vs=vs, vn=vn, used=used, n_rows=n_rows,
                n_pass=n_pass)


def _for_each_paired(n, compute, emit):
    def pair(q, carry):
        first = compute(2 * q)
        second = compute(2 * q + 1)
        emit(2 * q, first)
        emit(2 * q + 1, second)
        return carry

    lax.fori_loop(0, lax.shift_right_logical(n, 1), pair, 0)

    @pl.when((n & 1) == 1)
    def _():
        emit(n - 1, compute(n - 1))


def _zero_tail(obuf, n_used, n_sb, n_j, out_copy):
    tile = obuf.shape[2]
    obuf[0, 0:MOE_BLK, :] = jnp.zeros((MOE_BLK, tile), obuf.dtype)

    def start(sb, carry):
        for jj in range(n_j):
            out_copy(0, 0, pl.multiple_of(sb * MOE_BLK, MOE_BLK), jj * tile).start()
        return carry

    def wait(sb, carry):
        for jj in range(n_j):
            out_copy(0, 0, 0, 0).wait()
        return carry

    lax.fori_loop(n_used, n_sb, start, 0)
    lax.fori_loop(n_used, n_sb, wait, 0)


def _moe_up_kernel(ve_ref, vs_ref, vn_ref, used_ref, tok_ref, xp_hbm, wg_ref, wl_ref, bg_ref, bl_ref, act_hbm,
                   xbuf, wc_ref, obuf, xsem, osem, ocnt, *, tf):
    v = pl.program_id(0)
    j = pl.program_id(1)
    n_pass = pl.num_programs(0)
    n_j = pl.num_programs(1)
    half = xbuf.shape[2]
    slot = v % 2
    n_sub = vn_ref[v]

    def row_copy(tok, sl, r):
        return pltpu.make_async_copy(xp_hbm.at[pl.ds(tok, 1), :], xbuf.at[sl, pl.ds(r, 1), :], xsem.at[sl])

    def gather_start(vv, sl):
        base = vs_ref[vv] * MOE_BLK

        def body(q, carry):
            r0 = pl.multiple_of(q * ISSUE_UNROLL, ISSUE_UNROLL)
            for u in range(ISSUE_UNROLL):
                row_copy(tok_ref[base + r0 + u], sl, r0 + u).start()
            return carry

        lax.fori_loop(0, vn_ref[vv] * (MOE_BLK // ISSUE_UNROLL), body, 0)

    def gather_wait(vv, sl):
        def body(i, carry):
            r0 = pl.multiple_of(i * MOE_BLK, MOE_BLK)
            pltpu.make_async_copy(xp_hbm.at[pl.ds(0, MOE_BLK), :], xbuf.at[sl, pl.ds(r0, MOE_BLK), :],
                                  xsem.at[sl]).wait()
            return carry

        lax.fori_loop(0, vn_ref[vv], body, 0)

    def out_copy(p, r0, row, col):
        return pltpu.make_async_copy(obuf.at[p, pl.ds(r0, MOE_BLK), :],
                                     act_hbm.at[pl.ds(row, MOE_BLK), pl.ds(col, tf)], osem.at[p])

    def drain(p):
        def body(i, carry):
            out_copy(p, 0, 0, 0).wait()
            return carry

        lax.fori_loop(0, ocnt[p], body, 0)
        ocnt[p] = 0

    @pl.when(j == 0)
    def _():
        @pl.when(v == 0)
        def _():
            ocnt[0] = 0
            ocnt[1] = 0
            gather_start(0, 0)

        @pl.when(v + 1 < n_pass)
        def _():
            gather_start(v + 1, 1 - slot)

        gather_wait(v, slot)

    step = v * n_j + j
    p = step % 2
    drain(p)

    @pl.when(n_sub > 0)
    def _():
        wc_ref[:, 0:tf] = wg_ref[...].astype(BF16)
        wc_ref[:, tf:2 * tf] = wl_ref[...].astype(BF16)

        def compute(i):
            r0 = pl.multiple_of(i * MOE_BLK, MOE_BLK)
            lo, hi = _unpack_pair(xbuf[slot, pl.ds(r0, MOE_BLK), :])
            h = jnp.dot(lo.astype(BF16), wc_ref[0:half, :], preferred_element_type=F32)
            h = h + jnp.dot(hi.astype(BF16), wc_ref[half:2 * half, :], preferred_element_type=F32)
            glu = jnp.minimum(h[:, 0:tf] + bg_ref[...], SWIGLU_LIMIT)
            lin = jnp.clip(h[:, tf:2 * tf] + bl_ref[...], -SWIGLU_LIMIT, SWIGLU_LIMIT)
            return (glu * jax.nn.sigmoid(SWIGLU_ALPHA * glu) * (lin + 1.0)).astype(BF16)

        def emit(i, act):
            r0 = pl.multiple_of(i * MOE_BLK, MOE_BLK)
            obuf[p, pl.ds(r0, MOE_BLK), :] = act
            row = pl.multiple_of((vs_ref[v] + i) * MOE_BLK, MOE_BLK)
            out_copy(p, r0, row, pl.multiple_of(j * tf, tf)).start()

        _for_each_paired(n_sub, compute, emit)
        ocnt[p] = n_sub

    @pl.when(step == n_pass * n_j - 1)
    def _():
        drain(0)
        drain(1)
        _zero_tail(obuf, used_ref[0], act_hbm.shape[0] // MOE_BLK, n_j, out_copy)


def _moe_up(xp, plan, w_gate_up, b_gate_up):
    n_exp, d, two_f = w_gate_up.shape
    f = two_f // 2
    half = xp.shape[1]
    assert half * 2 == d
    tf = _largest_tile(f, 256)
    n_j = f // tf
    rows_pass = MOE_NSB * MOE_BLK

    def w_map(off):
        def index_map(v, j, ve, vs, vn, used, tok):
            jj = jnp.where(vn[v] > 0, j, n_j - 1)
            return (ve[v], 0, off + jj)
        return index_map

    grid_spec = pltpu.PrefetchScalarGridSpec(
        num_scalar_prefetch=5,
        grid=(plan["n_pass"], n_j),
        in_specs=[
            pl.BlockSpec(memory_space=pl.ANY),
            pl.BlockSpec((None, d, tf), w_map(0)),
            pl.BlockSpec((None, d, tf), w_map(n_j)),
            pl.BlockSpec((None, 1, tf), w_map(0)),
            pl.BlockSpec((None, 1, tf), w_map(n_j)),
        ],
        out_specs=pl.BlockSpec(memory_space=pl.ANY),
        scratch_shapes=[
            pltpu.VMEM((2, rows_pass, half), U32),
            pltpu.VMEM((d, 2 * tf), BF16),
            pltpu.VMEM((2, rows_pass, tf), BF16),
            pltpu.SemaphoreType.DMA((2,)),
            pltpu.SemaphoreType.DMA((2,)),
            pltpu.SMEM((2,), I32),
        ],
    )
    b3 = b_gate_up.reshape(n_exp, 1, two_f)
    return pl.pallas_call(
        functools.partial(_moe_up_kernel, tf=tf),
        grid_spec=grid_spec,
        out_shape=jax.ShapeDtypeStruct((plan["n_rows"], f), BF16),
        compiler_params=_params(("arbitrary", "arbitrary")),
        name="moe_up",
    )(plan["ve"], plan["vs"], plan["vn"], plan["used"], plan["row_tok"], xp, w_gate_up, w_gate_up, b3, b3)


def _moe_down_kernel(ve_ref, vs_ref, vn_ref, used_ref, act_hbm, wa_ref, wb_ref, ba_ref, bb_ref, y_hbm,
                     abuf, wc_ref, obuf, asem, osem, ocnt, *, tn):
    v = pl.program_id(0)
    j = pl.program_id(1)
    n_pass = pl.num_programs(0)
    n_j = pl.num_programs(1)
    slot = v % 2
    n_sub = vn_ref[v]

    def in_copy(row, sl, r0):
        return pltpu.make_async_copy(act_hbm.at[pl.ds(row, MOE_BLK), :],
                                     abuf.at[sl, pl.ds(r0, MOE_BLK), :], asem.at[sl])

    def load_start(vv, sl):
        def body(i, carry):
            r0 = pl.multiple_of(i * MOE_BLK, MOE_BLK)
            in_copy(pl.multiple_of((vs_ref[vv] + i) * MOE_BLK, MOE_BLK), sl, r0).start()
            return carry

        lax.fori_loop(0, vn_ref[vv], body, 0)

    def load_wait(vv, sl):
        def body(i, carry):
            in_copy(0, sl, pl.multiple_of(i * MOE_BLK, MOE_BLK)).wait()
            return carry

        lax.fori_loop(0, vn_ref[vv], body, 0)

    def out_copy(p, r0, row, col):
        return pltpu.make_async_copy(obuf.at[p, pl.ds(r0, MOE_BLK), :],
                                     y_hbm.at[pl.ds(row, MOE_BLK), pl.ds(col, tn)], osem.at[p])

    def drain(p):
        def body(i, carry):
            out_copy(p, 0, 0, 0).wait()
            return carry

        lax.fori_loop(0, ocnt[p], body, 0)
        ocnt[p] = 0

    @pl.when(j == 0)
    def _():
        @pl.when(v == 0)
        def _():
            ocnt[0] = 0
            ocnt[1] = 0
            load_start(0, 0)

        @pl.when(v + 1 < n_pass)
        def _():
            load_start(v + 1, 1 - slot)

        load_wait(v, slot)

    step = v * n_j + j
    p = step % 2
    drain(p)

    @pl.when(n_sub > 0)
    def _():
        wc_ref[:, 0:tn] = wa_ref[...].astype(BF16)
        wc_ref[:, tn:2 * tn] = wb_ref[...].astype(BF16)

        def compute(i):
            r0 = pl.multiple_of(i * MOE_BLK, MOE_BLK)
            y = jnp.dot(abuf[slot, pl.ds(r0, MOE_BLK), :], wc_ref[...], preferred_element_type=F32)
            return _pack_pair(y[:, 0:tn] + ba_ref[...], y[:, tn:2 * tn] + bb_ref[...])

        def emit(i, packed):
            r0 = pl.multiple_of(i * MOE_BLK, MOE_BLK)
            obuf[p, pl.ds(r0, MOE_BLK), :] = packed
            row = pl.multiple_of((vs_ref[v] + i) * MOE_BLK, MOE_BLK)
            out_copy(p, r0, row, pl.multiple_of(j * tn, tn)).start()

        _for_each_paired(n_sub, compute, emit)
        ocnt[p] = n_sub

    @pl.when(step == n_pass * n_j - 1)
    def _():
        drain(0)
        drain(1)
        _zero_tail(obuf, used_ref[0], y_hbm.shape[0] // MOE_BLK, n_j, out_copy)


def _moe_down(act, plan, w_down, b_down):
    n_exp, f, d = w_down.shape
    half = d // 2
    tn = _largest_tile(half, 256)
    n_j = half // tn
    rows_pass = MOE_NSB * MOE_BLK

    def w_map(off):
        def index_map(v, j, ve, vs, vn, used):
            jj = jnp.where(vn[v] > 0, j, n_j - 1)
            return (ve[v], 0, off + jj)
        return index_map

    grid_spec = pltpu.PrefetchScalarGridSpec(
        num_scalar_prefetch=4,
        grid=(plan["n_pass"], n_j),
        in_specs=[
            pl.BlockSpec(memory_space=pl.ANY),
            pl.BlockSpec((None, f, tn), w_map(0)),
            pl.BlockSpec((None, f, tn), w_map(n_j)),
            pl.BlockSpec((None, 1, tn), w_map(0)),
            pl.BlockSpec((None, 1, tn), w_map(n_j)),
        ],
        out_specs=pl.BlockSpec(memory_space=pl.ANY),
        scratch_shapes=[
            pltpu.VMEM((2, rows_pass, f), BF16),
            pltpu.VMEM((f, 2 * tn), BF16),
            pltpu.VMEM((2, rows_pass, tn), U32),
            pltpu.SemaphoreType.DMA((2,)),
            pltpu.SemaphoreType.DMA((2,)),
            pltpu.SMEM((2,), I32),
        ],
    )
    b3 = b_down.reshape(n_exp, 1, d)
    return pl.pallas_call(
        functools.partial(_moe_down_kernel, tn=tn),
        grid_spec=grid_spec,
        out_shape=jax.ShapeDtypeStruct((plan["n_rows"], half), U32),
        compiler_params=_params(("arbitrary", "arbitrary")),
        name="moe_down",
    )(plan["ve"], plan["vs"], plan["vn"], plan["used"], act, w_down, w_down, b3, b3)


def _combine_kernel(dest_ref, y_hbm, gate_ref, x_ref, g_ref, o_ref, ybuf, sem):
    i = pl.program_id(0)
    n_i = pl.num_programs(0)
    tm, d = x_ref.shape
    half = d // 2
    slot = i % 2

    def row_copy(row, sl, k, t):
        return pltpu.make_async_copy(y_hbm.at[pl.ds(row, 1), :], ybuf.at[sl, k, pl.ds(t, 1), :], sem.at[sl])

    def gather_start(ii, sl):
        def body(q, carry):
            t0 = pl.multiple_of(q * ISSUE_UNROLL, ISSUE_UNROLL)
            for u in range(ISSUE_UNROLL):
                for k in range(TOP_K):
                    row_copy(dest_ref[(ii * tm + t0 + u) * TOP_K + k], sl, k, t0 + u).start()
            return carry

        lax.fori_loop(0, tm // ISSUE_UNROLL, body, 0)

    def gather_wait(sl):
        for k in range(TOP_K):
            pltpu.make_async_copy(y_hbm.at[pl.ds(0, tm), :], ybuf.at[sl, k], sem.at[sl]).wait()

    @pl.when(i == 0)
    def _():
        gather_start(0, 0)

    @pl.when(i + 1 < n_i)
    def _():
        gather_start(i + 1, 1 - slot)

    gather_wait(slot)

    acc_lo = x_ref[:, 0:half]
    acc_hi = x_ref[:, half:d]
    for k in range(TOP_K):
        lo, hi = _unpack_pair(ybuf[slot, k])
        gk = gate_ref[:, k:k + 1]
        acc_lo = acc_lo + gk * lo
        acc_hi = acc_hi + gk * hi
    ms = (jnp.sum(acc_lo * acc_lo, axis=-1, keepdims=True)
          + jnp.sum(acc_hi * acc_hi, axis=-1, keepdims=True)) / d
    r = lax.rsqrt(ms + EPS)
    o_ref[:, 0:half] = acc_lo * r * g_ref[:, 0:half]
    o_ref[:, half:d] = acc_hi * r * g_ref[:, half:d]


def _combine(y_rows, dest, gate, x1, final_g):
    t_len, d = x1.shape
    half = d // 2
    tm = _largest_tile(t_len, 128, V7X_SUBLANES)
    grid_spec = pltpu.PrefetchScalarGridSpec(
        num_scalar_prefetch=1,
        grid=(t_len // tm,),
        in_specs=[
            pl.BlockSpec(memory_space=pl.ANY),
            pl.BlockSpec((tm, V7X_LANES), lambda i, dest: (i, 0)),
            pl.BlockSpec((tm, d), lambda i, dest: (i, 0)),
            pl.BlockSpec((1, d), lambda i, dest: (0, 0)),
        ],
        out_specs=pl.BlockSpec((tm, d), lambda i, dest: (i, 0)),
        scratch_shapes=[
            pltpu.VMEM((2, TOP_K, tm, half), U32),
            pltpu.SemaphoreType.DMA((2,)),
        ],
    )
    return pl.pallas_call(
        _combine_kernel,
        grid_spec=grid_spec,
        out_shape=jax.ShapeDtypeStruct((t_len, d), F32),
        compiler_params=_params(("arbitrary",)),
        name="combine",
    )(dest, y_rows, gate, x1, final_g.reshape(1, d))


def kernel(x, mix_norm_g, w_in, ssd_conv_w, ssd_conv_b, ssd_dt_bias, ssd_a_log, ssd_d, ssd_norm_g,
           gmlp_v_norm_g, gmlp_w_s, gmlp_b_s, gmlp_out_norm_g, w_out, ffn_norm_g, w_router, b_router,
           w_gate_up, b_gate_up, w_down, b_down, final_norm_g):
    bsz, t_len, d = x.shape
    assert bsz == 1
    depth = w_in.shape[0]
    ssd_w = ssd_norm_g.shape[-1]
    heads = ssd_dt_bias.shape[-1]
    gmlp_w = gmlp_out_norm_g.shape[-1]
    conv_dim = ssd_conv_b.shape[-1]
    assert heads <= V7X_LANES
    c_dt = ssd_w + conv_dim
    c_uv = c_dt + heads
    off_u, off_z, off_xs = 0, 2 * gmlp_w, 2 * gmlp_w + ssd_w

    h = x.reshape(t_len, d)
    out = None
    for layer in range(depth):
        w = w_in[layer]
        w_t = jnp.swapaxes(w, 0, 1).astype(BF16)
        w_dt = jnp.pad(w_t[c_dt:c_uv], ((0, V7X_LANES - heads), (0, 0)))
        p, dt_raw = _inproj(h, mix_norm_g[layer], w_t, w_dt, ssd_conv_w[layer], ssd_conv_b[layer],
                            2 * gmlp_w, ssd_w, c_uv)
        groups = (conv_dim - ssd_w) // (2 * SSD_STATE)
        y_ssd = _ssd(p, dt_raw, ssd_dt_bias[layer], ssd_a_log[layer], ssd_d[layer], ssd_norm_g[layer],
                     groups, off_z, off_xs)
        y_gmlp = _gmlp(p, gmlp_v_norm_g[layer], gmlp_w_s[layer], gmlp_b_s[layer],
                       gmlp_out_norm_g[layer], off_u)
        x1 = _outproj(y_ssd, y_gmlp, w_out[layer].astype(BF16), h)

        xp, idx_pad, gate_pad = _router(x1, ffn_norm_g[layer], w_router[layer], b_router[layer])
        plan = _route_plan(idx_pad[:, :TOP_K], w_router.shape[-1])
        act = _moe_up(xp, plan, w_gate_up[layer], b_gate_up[layer])
        y_rows = _moe_down(act, plan, w_down[layer], b_down[layer])
        last = layer == depth - 1
        assert last, "multi-layer stacking needs an un-normalised combine"
        out = _combine(y_rows, plan["dest"], gate_pad, x1, final_norm_g)
    return out.reshape(bsz, t_len, d)
```

```python
import functools
import math

import jax
import jax.numpy as jnp
from jax import lax
from jax.experimental import pallas as pl
from jax.experimental.pallas import tpu as pltpu

F32, BF16, U32, I32 = jnp.float32, jnp.bfloat16, jnp.uint32, jnp.int32

EPS = 1e-5
SSD_STATE = 128
SSD_CONV = 4
CHUNK = 128
TOP_K = 4
SSD_ROWS = 512
SWIGLU_ALPHA = 1.702
SWIGLU_LIMIT = 7.0

V7X_LANES = 128
V7X_SUBLANES = 8
V7X_VMEM_BYTES = 64 * 1024 * 1024
VMEM_LIMIT = V7X_VMEM_BYTES - 8 * 1024 * 1024

MOE_BLK = 256
MOE_NSB = 6
ISSUE_UNROLL = 8
HI_MASK = 0xFFFF0000


_gcd = math.gcd


def _largest_tile(total, cap, quantum=V7X_LANES):
    best = None
    t = quantum
    while t <= min(total, cap):
        if total % t == 0:
            best = t
        t += quantum
    assert best is not None, (total, cap, quantum)
    return best


def _params(semantics):
    return pltpu.CompilerParams(dimension_semantics=semantics, vmem_limit_bytes=VMEM_LIMIT)


def _silu(x):
    return x * jax.nn.sigmoid(x)


def _gelu(x):
    return 0.5 * x * (1.0 + lax.erf(x * (2.0 ** -0.5)))


def _softplus(x):
    return jnp.maximum(x, 0.0) + jnp.log1p(jnp.exp(-jnp.abs(x)))


def _pack_pair(lo, hi):
    lo_bits = lax.bitcast_convert_type(lo.astype(BF16).astype(F32), U32)
    hi_bits = lax.bitcast_convert_type(hi.astype(BF16).astype(F32), U32)
    return (lo_bits >> 16) | (hi_bits & jnp.uint32(HI_MASK))


def _unpack_pair(u):
    lo = lax.bitcast_convert_type(u << 16, F32)
    hi = lax.bitcast_convert_type(u & jnp.uint32(HI_MASK), F32)
    return lo, hi


NT_DIMS = (((1,), (1,)), ((), ()))


def _inproj_kernel(x_hbm, g_ref, w_ref, wdt_ref, cw_ref, cb_ref, p_ref, dt_ref, xbuf, xn_ref, halo_ref, sem,
                   *, n_gelu, n_silu):
    i = pl.program_id(0)
    j = pl.program_id(1)
    tm = xbuf.shape[0]
    rows = _largest_tile(tm, 128, V7X_SUBLANES)
    tail = halo_ref.shape[1]

    def x_copy(ii):
        return pltpu.make_async_copy(x_hbm.at[pl.ds(pl.multiple_of(ii * tm, tm), tm), :], xbuf, sem.at[0])

    @pl.when(j == 0)
    def _():
        @pl.when(i == 0)
        def _():
            x_copy(0).start()
            halo_ref[...] = jnp.zeros(halo_ref.shape, F32)

        x_copy(i).wait()

        def norm(r, carry):
            sl = pl.ds(pl.multiple_of(r * rows, rows), rows)
            x = xbuf[sl, :]
            ms = jnp.mean(x * x, axis=-1, keepdims=True)
            xn = (x * lax.rsqrt(ms + EPS) * g_ref[...]).astype(BF16)
            xn_ref[sl, :] = xn
            dt_ref[sl, :] = lax.dot_general(xn, wdt_ref[...], NT_DIMS, preferred_element_type=F32)
            return carry

        lax.fori_loop(0, tm // rows, norm, 0)

        @pl.when(i + 1 < pl.num_programs(0))
        def _():
            x_copy(i + 1).start()

    def project():
        return lax.dot_general(xn_ref[...], w_ref[...], NT_DIMS, preferred_element_type=F32)

    @pl.when(j < n_gelu)
    def _():
        p_ref[...] = _gelu(project()).astype(BF16)

    @pl.when((j >= n_gelu) & (j < n_gelu + n_silu))
    def _():
        p_ref[...] = _silu(project()).astype(BF16)

    @pl.when(j >= n_gelu + n_silu)
    def _():
        x = project()
        jc = j - (n_gelu + n_silu)
        ext = jnp.concatenate([halo_ref[jc], x], axis=0)
        acc = cb_ref[...] + cw_ref[SSD_CONV - 1:SSD_CONV, :] * ext
        for s in range(1, SSD_CONV):
            acc = acc + cw_ref[SSD_CONV - 1 - s:SSD_CONV - s, :] * pltpu.roll(ext, s, axis=0)
        p_ref[...] = _silu(acc[tail:, :]).astype(BF16)
        halo_ref[jc] = x[tm - tail:tm, :]


def _inproj(x, g, w_t, w_dt, conv_w, conv_b, n_gelu_cols, n_silu_cols, row_gelu):
    t_len, d = x.shape
    n_conv_cols = conv_b.shape[0]
    n_out = n_gelu_cols + n_silu_cols + n_conv_cols
    assert row_gelu + n_gelu_cols == w_t.shape[0] and row_gelu >= n_silu_cols + n_conv_cols
    tm = _largest_tile(t_len, 1024, V7X_SUBLANES)
    tn = _largest_tile(_gcd(_gcd(n_gelu_cols, n_silu_cols), n_conv_cols), 512)
    n_gelu, n_silu, n_conv = n_gelu_cols // tn, n_silu_cols // tn, n_conv_cols // tn
    tail = 2 * V7X_SUBLANES
    assert row_gelu % tail == 0 and tn % tail == 0
    conv_map = lambda i, j: (0, jnp.maximum(j - (n_gelu + n_silu), 0))
    w_map = lambda i, j: (
        tail * jnp.where(j < n_gelu, row_gelu // tail + j * (tn // tail), (j - n_gelu) * (tn // tail)), 0)
    return pl.pallas_call(
        functools.partial(_inproj_kernel, n_gelu=n_gelu, n_silu=n_silu),
        grid=(t_len // tm, n_out // tn),
        in_specs=[
            pl.BlockSpec(memory_space=pl.ANY),
            pl.BlockSpec((1, d), lambda i, j: (0, 0)),
            pl.BlockSpec((pl.Element(tn), pl.Element(d)), w_map),
            pl.BlockSpec((V7X_LANES, d), lambda i, j: (0, 0)),
            pl.BlockSpec((SSD_CONV, tn), conv_map),
            pl.BlockSpec((1, tn), conv_map),
        ],
        out_specs=[
            pl.BlockSpec((tm, tn), lambda i, j: (i, j)),
            pl.BlockSpec((tm, V7X_LANES), lambda i, j: (i, 0)),
        ],
        out_shape=[
            jax.ShapeDtypeStruct((t_len, n_out), BF16),
            jax.ShapeDtypeStruct((t_len, V7X_LANES), F32),
        ],
        scratch_shapes=[
            pltpu.VMEM((tm, d), F32),
            pltpu.VMEM((tm, d), BF16),
            pltpu.VMEM((n_conv, tail, tn), F32),
            pltpu.SemaphoreType.DMA((1,)),
        ],
        compiler_params=_params(("arbitrary", "arbitrary")),
        name="inproj",
    )(x, g.reshape(1, d), w_t, w_dt, conv_w, conv_b.reshape(1, n_conv_cols))


def _split3(x):
    hi = x.astype(BF16)
    r1 = x - hi.astype(F32)
    mid = r1.astype(BF16)
    lo = (r1 - mid.astype(F32)).astype(BF16)
    return hi, mid, lo


def _ssd_kernel(sz_ref, xs_ref, b_ref, c_ref, dt_ref, dtb_ref, alog_ref, dsk_ref, ng_ref, y_ref, state_ref,
                *, hpg, hd):
    width = hpg * hd
    lanes = V7X_LANES
    per = lanes // hd

    @pl.when(pl.program_id(1) == 0)
    def _():
        state_ref[...] = jnp.zeros(state_ref.shape, F32)

    ri = lax.broadcasted_iota(I32, (CHUNK, CHUNK), 0)
    ci = lax.broadcasted_iota(I32, (CHUNK, CHUNK), 1)
    causal = ri >= ci
    tril = causal.astype(BF16)
    neg_a = -jnp.exp(alog_ref[...])

    def expand(q):
        cols = []
        for j in range(width // lanes):
            blk = jnp.broadcast_to(q[:, j * per:j * per + 1], (CHUNK, lanes))
            for t in range(1, per):
                nxt = jnp.broadcast_to(q[:, j * per + t:j * per + t + 1], (CHUNK, lanes))
                blk = jnp.where(ci >= t * hd, nxt, blk)
            cols.append(blk)
        return jnp.concatenate(cols, axis=1)

    def chunk(k, carry):
        rows = pl.ds(pl.multiple_of(k * CHUNK, CHUNK), CHUNK)
        xs = xs_ref[rows, :].astype(F32)
        bm = b_ref[rows, :]
        cm = c_ref[rows, :]

        dt = _softplus(dt_ref[rows, :] + dtb_ref[...])
        a_dt = dt * neg_a
        acum = sum(jnp.dot(tril, piece, preferred_element_type=F32) for piece in _split3(a_dt))
        acum_t = acum.T
        acum_f = expand(acum)
        last = acum_f[CHUNK - 1:CHUNK, :]
        xdt = xs * expand(dt)

        cb = lax.dot_general(cm, bm, (((1,), (1,)), ((), ())), preferred_element_type=F32)
        y_cols = []
        for j in range(width // lanes):
            x_col = xdt[:, j * lanes:(j + 1) * lanes]
            y_col = jnp.zeros((CHUNK, lanes), F32)
            for t in range(per):
                h = j * per + t
                seg = acum[:, h:h + 1] - acum_t[h:h + 1, :]
                lmat = jnp.exp(jnp.where(causal, seg, -jnp.inf))
                m_h = (cb * lmat).astype(BF16)
                mask = (ci >= t * hd) & (ci < (t + 1) * hd)
                x_h = jnp.where(mask, x_col, 0.0).astype(BF16)
                y_col = y_col + jnp.dot(m_h, x_h, preferred_element_type=F32)
            y_cols.append(y_col)
        y_diag = jnp.concatenate(y_cols, axis=1)

        state = state_ref[...]
        y_off = jnp.dot(cm, state.astype(BF16), preferred_element_type=F32) * jnp.exp(acum_f)
        x_dec = (xdt * jnp.exp(last - acum_f)).astype(BF16)
        s_new = lax.dot_general(bm, x_dec, (((0,), (0,)), ((), ())), preferred_element_type=F32)
        state_ref[...] = state * jnp.exp(last) + s_new

        yg = (y_diag + y_off + xs * dsk_ref[...]) * sz_ref[rows, :].astype(F32)
        yn = yg * lax.rsqrt(jnp.mean(yg * yg, axis=-1, keepdims=True) + EPS) * ng_ref[...]
        y_ref[rows, :] = yn.astype(BF16)
        return carry

    lax.fori_loop(0, y_ref.shape[0] // CHUNK, chunk, 0, unroll=2)


def _ssd(p, dt_raw, dt_bias, a_log, d_skip, norm_g, groups, off_z, off_xs):
    t_len = p.shape[0]
    heads = dt_bias.shape[0]
    width = norm_g.shape[0]
    hd = width // heads
    hpg = heads // groups
    gw = hpg * hd
    n_st = SSD_STATE
    rows = _largest_tile(t_len, SSD_ROWS, CHUNK)
    assert V7X_LANES % hd == 0 and gw % V7X_LANES == 0 and hpg <= V7X_LANES
    assert width // groups == gw
    off_b = off_xs + width
    off_c = off_b + groups * n_st
    assert off_z % gw == 0 and off_xs % gw == 0 and off_b % n_st == 0

    def per_group_lanes(v):
        v = v.reshape(groups, 1, hpg)
        return jnp.pad(v, ((0, 0), (0, 0), (0, V7X_LANES - hpg)))

    dt_g = dt_raw[:, :heads].reshape(t_len, groups, hpg).transpose(1, 0, 2)
    dt_g = jnp.pad(dt_g, ((0, 0), (0, 0), (0, V7X_LANES - hpg)))

    grp = lambda g, c: (g, 0, 0)
    return pl.pallas_call(
        functools.partial(_ssd_kernel, hpg=hpg, hd=hd),
        grid=(groups, t_len // rows),
        in_specs=[
            pl.BlockSpec((rows, gw), lambda g, c: (c, off_z // gw + g)),
            pl.BlockSpec((rows, gw), lambda g, c: (c, off_xs // gw + g)),
            pl.BlockSpec((rows, n_st), lambda g, c: (c, off_b // n_st + g)),
            pl.BlockSpec((rows, n_st), lambda g, c: (c, off_c // n_st + g)),
            pl.BlockSpec((None, rows, V7X_LANES), lambda g, c: (g, c, 0)),
            pl.BlockSpec((None, 1, V7X_LANES), grp),
            pl.BlockSpec((None, 1, V7X_LANES), grp),
            pl.BlockSpec((1, gw), lambda g, c: (0, g)),
            pl.BlockSpec((1, gw), lambda g, c: (0, g)),
        ],
        out_specs=pl.BlockSpec((rows, gw), lambda g, c: (c, g)),
        out_shape=jax.ShapeDtypeStruct((t_len, width), BF16),
        scratch_shapes=[pltpu.VMEM((n_st, gw), F32)],
        compiler_params=_params(("arbitrary", "arbitrary")),
        name="ssd",
    )(p, p, p, p, dt_g, per_group_lanes(dt_bias), per_group_lanes(a_log),
      jnp.repeat(d_skip, hd).reshape(1, width), norm_g.reshape(1, width))


def _row_sums(x):
    ones = jnp.ones((x.shape[1], V7X_LANES), BF16)
    hi = x.astype(BF16)
    lo = (x - hi.astype(F32)).astype(BF16)
    return (jnp.dot(hi, ones, preferred_element_type=F32) + jnp.dot(lo, ones, preferred_element_type=F32))


def _gmlp_kernel(u_ref, v_ref, vg_ref, ws_ref, bs_ref, og_ref, y_ref, yscr_ref, *, heads, hd):
    lanes = V7X_LANES
    reps = hd // lanes
    ri = lax.broadcasted_iota(I32, (CHUNK, CHUNK), 0)
    ci = lax.broadcasted_iota(I32, (CHUNK, CHUNK), 1)
    causal = ri >= ci
    ssq = jnp.zeros((CHUNK, lanes), F32)
    invs = []
    for h in range(heads):
        v_h = v_ref[:, h * hd:(h + 1) * hd].astype(F32)
        invs.append(lax.rsqrt(_row_sums(v_h * v_h) / hd + EPS))
    for h in range(heads):
        cols = slice(h * hd, (h + 1) * hd)
        u_h = u_ref[:, cols].astype(F32)
        v_h = v_ref[:, cols].astype(F32)
        v_n = v_h * jnp.tile(invs[h], (1, reps)) * vg_ref[:, cols]
        w_h = jnp.where(causal, ws_ref[h], 0.0).astype(BF16)
        sv = jnp.dot(w_h, v_n.astype(BF16), preferred_element_type=F32) + jnp.tile(bs_ref[h], (1, reps))
        y_h = u_h * sv
        yscr_ref[:, cols] = y_h
        y_sq = y_h * y_h
        for r in range(reps):
            ssq = ssq + y_sq[:, r * lanes:(r + 1) * lanes]
    scale = lax.rsqrt(_row_sums(ssq) / (heads * hd) + EPS)
    y_ref[...] = (yscr_ref[...] * jnp.tile(scale, (1, heads * reps)) * og_ref[...]).astype(BF16)


def _gmlp(p, v_norm_g, w_s, b_s, out_norm_g, off_u):
    t_len = p.shape[0]
    heads = w_s.shape[0]
    width = out_norm_g.shape[0]
    hd = width // heads
    assert w_s.shape[1] == CHUNK and off_u % width == 0 and hd % V7X_LANES == 0
    b_lanes = jnp.broadcast_to(b_s[:, :, None], (heads, CHUNK, V7X_LANES))
    return pl.pallas_call(
        functools.partial(_gmlp_kernel, heads=heads, hd=hd),
        grid=(t_len // CHUNK,),
        in_specs=[
            pl.BlockSpec((CHUNK, width), lambda c: (c, off_u // width)),
            pl.BlockSpec((CHUNK, width), lambda c: (c, off_u // width + 1)),
            pl.BlockSpec((1, width), lambda c: (0, 0)),
            pl.BlockSpec((heads, CHUNK, CHUNK), lambda c: (0, 0, 0)),
            pl.BlockSpec((heads, CHUNK, V7X_LANES), lambda c: (0, 0, 0)),
            pl.BlockSpec((1, width), lambda c: (0, 0)),
        ],
        out_specs=pl.BlockSpec((CHUNK, width), lambda c: (c, 0)),
        out_shape=jax.ShapeDtypeStruct((t_len, width), BF16),
        scratch_shapes=[pltpu.VMEM((CHUNK, width), F32)],
        compiler_params=_params(("arbitrary",)),
        name="gmlp",
    )(p, p, v_norm_g.reshape(1, width), w_s, b_lanes, out_norm_g.reshape(1, width))


def _outproj_kernel(ys_ref, yg_ref, wa_ref, wb_ref, x_ref, o_ref):
    acc = jnp.dot(ys_ref[...], wa_ref[...], preferred_element_type=F32)
    acc = acc + jnp.dot(yg_ref[...], wb_ref[...], preferred_element_type=F32)
    o_ref[...] = x_ref[...] + acc


def _outproj(y_ssd, y_gmlp, w_out, x):
    t_len, d = x.shape
    ka, kb = y_ssd.shape[1], y_gmlp.shape[1]
    assert ka == kb and w_out.shape[0] == ka + kb
    tm = _largest_tile(t_len, 512, V7X_SUBLANES)
    tn = _largest_tile(d, 512)
    return pl.pallas_call(
        _outproj_kernel,
        grid=(t_len // tm, d // tn),
        in_specs=[
            pl.BlockSpec((tm, ka), lambda i, j: (i, 0)),
            pl.BlockSpec((tm, kb), lambda i, j: (i, 0)),
            pl.BlockSpec((ka, tn), lambda i, j: (0, j)),
            pl.BlockSpec((kb, tn), lambda i, j: (1, j)),
            pl.BlockSpec((tm, tn), lambda i, j: (i, j)),
        ],
        out_specs=pl.BlockSpec((tm, tn), lambda i, j: (i, j)),
        out_shape=jax.ShapeDtypeStruct((t_len, d), F32),
        compiler_params=_params(("arbitrary", "arbitrary")),
        name="outproj",
    )(y_ssd, y_gmlp, w_out, w_out, x)


def _router_kernel(x_ref, g_ref, wr_ref, br_ref, xp_ref, idx_ref, gate_ref, *, n_exp):
    x = x_ref[...]
    tm, d = x.shape
    half = d // 2
    ms = jnp.mean(x * x, axis=-1, keepdims=True)
    xn = x * lax.rsqrt(ms + EPS) * g_ref[...]
    xp_ref[...] = _pack_pair(xn[:, :half], xn[:, half:])

    logits = jnp.dot(xn.astype(BF16), wr_ref[...], preferred_element_type=F32) + br_ref[...]
    lane = lax.broadcasted_iota(I32, (tm, V7X_LANES), 1)
    vals = jnp.where(lane < n_exp, logits, -jnp.inf)
    idx_out = jnp.zeros((tm, V7X_LANES), I32)
    val_out = jnp.full((tm, V7X_LANES), -jnp.inf, F32)
    for k in range(TOP_K):
        m = jnp.max(vals, axis=-1, keepdims=True)
        sel = jnp.min(jnp.where(vals == m, lane, V7X_LANES), axis=-1, keepdims=True)
        idx_out = jnp.where(lane == k, sel, idx_out)
        val_out = jnp.where(lane == k, m, val_out)
        vals = jnp.where(lane == sel, -jnp.inf, vals)
    e = jnp.exp(val_out - jnp.max(val_out, axis=-1, keepdims=True))
    idx_ref[...] = idx_out
    gate_ref[...] = e / jnp.sum(e, axis=-1, keepdims=True)


def _router(x1, g, w_router, b_router):
    t_len, d = x1.shape
    n_exp = w_router.shape[1]
    assert n_exp <= V7X_LANES and TOP_K <= n_exp
    tm = _largest_tile(t_len, 256, V7X_SUBLANES)
    wr = jnp.pad(w_router, ((0, 0), (0, V7X_LANES - n_exp))).astype(BF16)
    br = jnp.pad(b_router, (0, V7X_LANES - n_exp)).reshape(1, V7X_LANES)
    return pl.pallas_call(
        functools.partial(_router_kernel, n_exp=n_exp),
        grid=(t_len // tm,),
        in_specs=[
            pl.BlockSpec((tm, d), lambda i: (i, 0)),
            pl.BlockSpec((1, d), lambda i: (0, 0)),
            pl.BlockSpec((d, V7X_LANES), lambda i: (0, 0)),
            pl.BlockSpec((1, V7X_LANES), lambda i: (0, 0)),
        ],
        out_specs=[
            pl.BlockSpec((tm, d // 2), lambda i: (i, 0)),
            pl.BlockSpec((tm, V7X_LANES), lambda i: (i, 0)),
            pl.BlockSpec((tm, V7X_LANES), lambda i: (i, 0)),
        ],
        out_shape=[
            jax.ShapeDtypeStruct((t_len, d // 2), U32),
            jax.ShapeDtypeStruct((t_len, V7X_LANES), I32),
            jax.ShapeDtypeStruct((t_len, V7X_LANES), F32),
        ],
        compiler_params=_params(("arbitrary",)),
        name="router",
    )(x1, g.reshape(1, d), wr, br)


def _route_plan(top_idx, n_exp):
    t_len, top_k = top_idx.shape
    n_assign = t_len * top_k
    e = top_idx.reshape(n_assign)
    onehot = (e[:, None] == jnp.arange(n_exp, dtype=I32)[None, :]).astype(I32)
    csum = jnp.cumsum(onehot, axis=0)
    rank = jnp.sum(onehot * csum, axis=1) - 1
    counts = csum[-1]
    nb = (counts + MOE_BLK - 1) // MOE_BLK
    sb_end = jnp.cumsum(nb)
    sb_start = sb_end - nb
    dest = sb_start[e] * MOE_BLK + rank
    n_sb = n_assign // MOE_BLK + n_exp
    n_rows = n_sb * MOE_BLK
    row_tok = jnp.zeros((n_rows,), I32).at[dest].set(jnp.arange(n_assign, dtype=I32) // top_k)

    nv = (nb + MOE_NSB - 1) // MOE_NSB
    v_end = jnp.cumsum(nv)
    v_start = v_end - nv
    n_pass = -(-n_sb // MOE_NSB) + n_exp
    vid = jnp.arange(n_pass, dtype=I32)
    ve = jnp.minimum(jnp.searchsorted(v_end, vid, side='right'), n_exp - 1).astype(I32)
    q = vid - v_start[ve]
    valid = vid < v_end[-1]
    vs = jnp.where(valid, sb_start[ve] + q * MOE_NSB, 0).astype(I32)
    vn = jnp.where(valid, jnp.clip(nb[ve] - q * MOE_NSB, 0, MOE_NSB), 0).astype(I32)
    last_e = ve[jnp.maximum(v_end[-1] - 1, 0)]
    ve = jnp.where(valid, ve, last_e).astype(I32)
    used = sb_end[-1:].astype(I32)
    return dict(row_tok=row_tok, dest=dest.astype(I32), ve=ve, vs=vs, vn=vn, used=used, n_rows=n_rows,
                n_pass=n_pass)


def _for_each_paired(n, compute, emit):
    def pair(q, carry):
        first = compute(2 * q)
        second = compute(2 * q + 1)
        emit(2 * q, first)
        emit(2 * q + 1, second)
        return carry

    lax.fori_loop(0, lax.shift_right_logical(n, 1), pair, 0)

    @pl.when((n & 1) == 1)
    def _():
        emit(n - 1, compute(n - 1))


def _zero_tail(obuf, n_used, n_sb, n_j, out_copy):
    tile = obuf.shape[2]
    obuf[0, 0:MOE_BLK, :] = jnp.zeros((MOE_BLK, tile), obuf.dtype)

    def start(sb, carry):
        for jj in range(n_j):
            out_copy(0, 0, pl.multiple_of(sb * MOE_BLK, MOE_BLK), jj * tile).start()
        return carry

    def wait(sb, carry):
        for jj in range(n_j):
            out_copy(0, 0, 0, 0).wait()
        return carry

    lax.fori_loop(n_used, n_sb, start, 0)
    lax.fori_loop(n_used, n_sb, wait, 0)


def _moe_up_kernel(ve_ref, vs_ref, vn_ref, used_ref, tok_ref, xp_hbm, wg_ref, wl_ref, bg_ref, bl_ref, act_hbm,
                   xbuf, wc_ref, obuf, xsem, osem, ocnt, *, tf):
    v = pl.program_id(0)
    j = pl.program_id(1)
    n_pass = pl.num_programs(0)
    n_j = pl.num_programs(1)
    half = xbuf.shape[2]
    slot = v % 2
    n_sub = vn_ref[v]

    def row_copy(tok, sl, r):
        return pltpu.make_async_copy(xp_hbm.at[pl.ds(tok, 1), :], xbuf.at[sl, pl.ds(r, 1), :], xsem.at[sl])

    def gather_start(vv, sl):
        base = vs_ref[vv] * MOE_BLK

        def body(q, carry):
            r0 = pl.multiple_of(q * ISSUE_UNROLL, ISSUE_UNROLL)
            for u in range(ISSUE_UNROLL):
                row_copy(tok_ref[base + r0 + u], sl, r0 + u).start()
            return carry

        lax.fori_loop(0, vn_ref[vv] * (MOE_BLK // ISSUE_UNROLL), body, 0)

    def gather_wait(vv, sl):
        def body(i, carry):
            r0 = pl.multiple_of(i * MOE_BLK, MOE_BLK)
            pltpu.make_async_copy(xp_hbm.at[pl.ds(0, MOE_BLK), :], xbuf.at[sl, pl.ds(r0, MOE_BLK), :],
                                  xsem.at[sl]).wait()
            return carry

        lax.fori_loop(0, vn_ref[vv], body, 0)

    def out_copy(p, r0, row, col):
        return pltpu.make_async_copy(obuf.at[p, pl.ds(r0, MOE_BLK), :],
                                     act_hbm.at[pl.ds(row, MOE_BLK), pl.ds(col, tf)], osem.at[p])

    def drain(p):
        def body(i, carry):
            out_copy(p, 0, 0, 0).wait()
            return carry

        lax.fori_loop(0, ocnt[p], body, 0)
        ocnt[p] = 0

    @pl.when(j == 0)
    def _():
        @pl.when(v == 0)
        def _():
            ocnt[0] = 0
            ocnt[1] = 0
            gather_start(0, 0)

        @pl.when(v + 1 < n_pass)
        def _():
            gather_start(v + 1, 1 - slot)

        gather_wait(v, slot)

    step = v * n_j + j
    p = step % 2
    drain(p)

    @pl.when(n_sub > 0)
    def _():
        wc_ref[:, 0:tf] = wg_ref[...].astype(BF16)
        wc_ref[:, tf:2 * tf] = wl_ref[...].astype(BF16)

        def compute(i):
            r0 = pl.multiple_of(i * MOE_BLK, MOE_BLK)
            lo, hi = _unpack_pair(xbuf[slot, pl.ds(r0, MOE_BLK), :])
            h = jnp.dot(lo.astype(BF16), wc_ref[0:half, :], preferred_element_type=F32)
            h = h + jnp.dot(hi.astype(BF16), wc_ref[half:2 * half, :], preferred_element_type=F32)
            glu = jnp.minimum(h[:, 0:tf] + bg_ref[...], SWIGLU_LIMIT)
            lin = jnp.clip(h[:, tf:2 * tf] + bl_ref[...], -SWIGLU_LIMIT, SWIGLU_LIMIT)
            return (glu * jax.nn.sigmoid(SWIGLU_ALPHA * glu) * (lin + 1.0)).astype(BF16)

        def emit(i, act):
            r0 = pl.multiple_of(i * MOE_BLK, MOE_BLK)
            obuf[p, pl.ds(r0, MOE_BLK), :] = act
            row = pl.multiple_of((vs_ref[v] + i) * MOE_BLK, MOE_BLK)
            out_copy(p, r0, row, pl.multiple_of(j * tf, tf)).start()

        _for_each_paired(n_sub, compute, emit)
        ocnt[p] = n_sub

    @pl.when(step == n_pass * n_j - 1)
    def _():
        drain(0)
        drain(1)
        _zero_tail(obuf, used_ref[0], act_hbm.shape[0] // MOE_BLK, n_j, out_copy)


def _moe_up(xp, plan, w_gate_up, b_gate_up):
    n_exp, d, two_f = w_gate_up.shape
    f = two_f // 2
    half = xp.shape[1]
    assert half * 2 == d
    tf = _largest_tile(f, 256)
    n_j = f // tf
    rows_pass = MOE_NSB * MOE_BLK

    def w_map(off):
        def index_map(v, j, ve, vs, vn, used, tok):
            jj = jnp.where(vn[v] > 0, j, n_j - 1)
            return (ve[v], 0, off + jj)
        return index_map

    grid_spec = pltpu.PrefetchScalarGridSpec(
        num_scalar_prefetch=5,
        grid=(plan["n_pass"], n_j),
        in_specs=[
            pl.BlockSpec(memory_space=pl.ANY),
            pl.BlockSpec((None, d, tf), w_map(0)),
            pl.BlockSpec((None, d, tf), w_map(n_j)),
            pl.BlockSpec((None, 1, tf), w_map(0)),
            pl.BlockSpec((None, 1, tf), w_map(n_j)),
        ],
        out_specs=pl.BlockSpec(memory_space=pl.ANY),
        scratch_shapes=[
            pltpu.VMEM((2, rows_pass, half), U32),
            pltpu.VMEM((d, 2 * tf), BF16),
            pltpu.VMEM((2, rows_pass, tf), BF16),
            pltpu.SemaphoreType.DMA((2,)),
            pltpu.SemaphoreType.DMA((2,)),
            pltpu.SMEM((2,), I32),
        ],
    )
    b3 = b_gate_up.reshape(n_exp, 1, two_f)
    return pl.pallas_call(
        functools.partial(_moe_up_kernel, tf=tf),
        grid_spec=grid_spec,
        out_shape=jax.ShapeDtypeStruct((plan["n_rows"], f), BF16),
        compiler_params=_params(("arbitrary", "arbitrary")),
        name="moe_up",
    )(plan["ve"], plan["vs"], plan["vn"], plan["used"], plan["row_tok"], xp, w_gate_up, w_gate_up, b3, b3)


def _moe_down_kernel(ve_ref, vs_ref, vn_ref, used_ref, act_hbm, wa_ref, wb_ref, ba_ref, bb_ref, y_hbm,
                     abuf, wc_ref, obuf, asem, osem, ocnt, *, tn):
    v = pl.program_id(0)
    j = pl.program_id(1)
    n_pass = pl.num_programs(0)
    n_j = pl.num_programs(1)
    slot = v % 2
    n_sub = vn_ref[v]

    def in_copy(row, sl, r0):
        return pltpu.make_async_copy(act_hbm.at[pl.ds(row, MOE_BLK), :],
                                     abuf.at[sl, pl.ds(r0, MOE_BLK), :], asem.at[sl])

    def load_start(vv, sl):
        def body(i, carry):
            r0 = pl.multiple_of(i * MOE_BLK, MOE_BLK)
            in_copy(pl.multiple_of((vs_ref[vv] + i) * MOE_BLK, MOE_BLK), sl, r0).start()
            return carry

        lax.fori_loop(0, vn_ref[vv], body, 0)

    def load_wait(vv, sl):
        def body(i, carry):
            in_copy(0, sl, pl.multiple_of(i * MOE_BLK, MOE_BLK)).wait()
            return carry

        lax.fori_loop(0, vn_ref[vv], body, 0)

    def out_copy(p, r0, row, col):
        return pltpu.make_async_copy(obuf.at[p, pl.ds(r0, MOE_BLK), :],
                                     y_hbm.at[pl.ds(row, MOE_BLK), pl.ds(col, tn)], osem.at[p])

    def drain(p):
        def body(i, carry):
            out_copy(p, 0, 0, 0).wait()
            return carry

        lax.fori_loop(0, ocnt[p], body, 0)
        ocnt[p] = 0

    @pl.when(j == 0)
    def _():
        @pl.when(v == 0)
        def _():
            ocnt[0] = 0
            ocnt[1] = 0
            load_start(0, 0)

        @pl.when(v + 1 < n_pass)
        def _():
            load_start(v + 1, 1 - slot)

        load_wait(v, slot)

    step = v * n_j + j
    p = step % 2
    drain(p)

    @pl.when(n_sub > 0)
    def _():
        wc_ref[:, 0:tn] = wa_ref[...].astype(BF16)
        wc_ref[:, tn:2 * tn] = wb_ref[...].astype(BF16)

        def compute(i):
            r0 = pl.multiple_of(i * MOE_BLK, MOE_BLK)
            y = jnp.dot(abuf[slot, pl.ds(r0, MOE_BLK), :], wc_ref[...], preferred_element_type=F32)
            return _pack_pair(y[:, 0:tn] + ba_ref[...], y[:, tn:2 * tn] + bb_ref[...])

        def emit(i, packed):
            r0 = pl.multiple_of(i * MOE_BLK, MOE_BLK)
            obuf[p, pl.ds(r0, MOE_BLK), :] = packed
            row = pl.multiple_of((vs_ref[v] + i) * MOE_BLK, MOE_BLK)
            out_copy(p, r0, row, pl.multiple_of(j * tn, tn)).start()

        _for_each_paired(n_sub, compute, emit)
        ocnt[p] = n_sub

    @pl.when(step == n_pass * n_j - 1)
    def _():
        drain(0)
        drain(1)
        _zero_tail(obuf, used_ref[0], y_hbm.shape[0] // MOE_BLK, n_j, out_copy)


def _moe_down(act, plan, w_down, b_down):
    n_exp, f, d = w_down.shape
    half = d // 2
    tn = _largest_tile(half, 256)
    n_j = half // tn
    rows_pass = MOE_NSB * MOE_BLK

    def w_map(off):
        def index_map(v, j, ve, vs, vn, used):
            jj = jnp.where(vn[v] > 0, j, n_j - 1)
            return (ve[v], 0, off + jj)
        return index_map

    grid_spec = pltpu.PrefetchScalarGridSpec(
        num_scalar_prefetch=4,
        grid=(plan["n_pass"], n_j),
        in_specs=[
            pl.BlockSpec(memory_space=pl.ANY),
            pl.BlockSpec((None, f, tn), w_map(0)),
            pl.BlockSpec((None, f, tn), w_map(n_j)),
            pl.BlockSpec((None, 1, tn), w_map(0)),
            pl.BlockSpec((None, 1, tn), w_map(n_j)),
        ],
        out_specs=pl.BlockSpec(memory_space=pl.ANY),
        scratch_shapes=[
            pltpu.VMEM((2, rows_pass, f), BF16),
            pltpu.VMEM((f, 2 * tn), BF16),
            pltpu.VMEM((2, rows_pass, tn), U32),
            pltpu.SemaphoreType.DMA((2,)),
            pltpu.SemaphoreType.DMA((2,)),
            pltpu.SMEM((2,), I32),
        ],
    )
    b3 = b_down.reshape(n_exp, 1, d)
    return pl.pallas_call(
        functools.partial(_moe_down_kernel, tn=tn),
        grid_spec=grid_spec,
        out_shape=jax.ShapeDtypeStruct((plan["n_rows"], half), U32),
        compiler_params=_params(("arbitrary", "arbitrary")),
        name="moe_down",
    )(plan["ve"], plan["vs"], plan["vn"], plan["used"], act, w_down, w_down, b3, b3)


def _combine_kernel(dest_ref, y_hbm, gate_ref, x_ref, g_ref, o_ref, ybuf, sem):
    i = pl.program_id(0)
    n_i = pl.num_programs(0)
    tm, d = x_ref.shape
    half = d // 2
    slot = i % 2

    def row_copy(row, sl, k, t):
        return pltpu.make_async_copy(y_hbm.at[pl.ds(row, 1), :], ybuf.at[sl, k, pl.ds(t, 1), :], sem.at[sl])

    def gather_start(ii, sl):
        def body(q, carry):
            t0 = pl.multiple_of(q * ISSUE_UNROLL, ISSUE_UNROLL)
            for u in range(ISSUE_UNROLL):
                for k in range(TOP_K):
                    row_copy(dest_ref[(ii * tm + t0 + u) * TOP_K + k], sl, k, t0 + u).start()
            return carry

        lax.fori_loop(0, tm // ISSUE_UNROLL, body, 0)

    def gather_wait(sl):
        for k in range(TOP_K):
            pltpu.make_async_copy(y_hbm.at[pl.ds(0, tm), :], ybuf.at[sl, k], sem.at[sl]).wait()

    @pl.when(i == 0)
    def _():
        gather_start(0, 0)

    @pl.when(i + 1 < n_i)
    def _():
        gather_start(i + 1, 1 - slot)

    gather_wait(slot)

    acc_lo = x_ref[:, 0:half]
    acc_hi = x_ref[:, half:d]
    for k in range(TOP_K):
        lo, hi = _unpack_pair(ybuf[slot, k])
        gk = gate_ref[:, k:k + 1]
        acc_lo = acc_lo + gk * lo
        acc_hi = acc_hi + gk * hi
    ms = (jnp.sum(acc_lo * acc_lo, axis=-1, keepdims=True)
          + jnp.sum(acc_hi * acc_hi, axis=-1, keepdims=True)) / d
    r = lax.rsqrt(ms + EPS)
    o_ref[:, 0:half] = acc_lo * r * g_ref[:, 0:half]
    o_ref[:, half:d] = acc_hi * r * g_ref[:, half:d]


def _combine(y_rows, dest, gate, x1, final_g):
    t_len, d = x1.shape
    half = d // 2
    tm = _largest_tile(t_len, 128, V7X_SUBLANES)
    grid_spec = pltpu.PrefetchScalarGridSpec(
        num_scalar_prefetch=1,
        grid=(t_len // tm,),
        in_specs=[
            pl.BlockSpec(memory_space=pl.ANY),
            pl.BlockSpec((tm, V7X_LANES), lambda i, dest: (i, 0)),
            pl.BlockSpec((tm, d), lambda i, dest: (i, 0)),
            pl.BlockSpec((1, d), lambda i, dest: (0, 0)),
        ],
        out_specs=pl.BlockSpec((tm, d), lambda i, dest: (i, 0)),
        scratch_shapes=[
            pltpu.VMEM((2, TOP_K, tm, half), U32),
            pltpu.SemaphoreType.DMA((2,)),
        ],
    )
    return pl.pallas_call(
        _combine_kernel,
        grid_spec=grid_spec,
        out_shape=jax.ShapeDtypeStruct((t_len, d), F32),
        compiler_params=_params(("arbitrary",)),
        name="combine",
    )(dest, y_rows, gate, x1, final_g.reshape(1, d))


def kernel(x, mix_norm_g, w_in, ssd_conv_w, ssd_conv_b, ssd_dt_bias, ssd_a_log, ssd_d, ssd_norm_g,
           gmlp_v_norm_g, gmlp_w_s, gmlp_b_s, gmlp_out_norm_g, w_out, ffn_norm_g, w_router, b_router,
           w_gate_up, b_gate_up, w_down, b_down, final_norm_g):
    bsz, t_len, d = x.shape
    assert bsz == 1
    depth = w_in.shape[0]
    ssd_w = ssd_norm_g.shape[-1]
    heads = ssd_dt_bias.shape[-1]
    gmlp_w = gmlp_out_norm_g.shape[-1]
    conv_dim = ssd_conv_b.shape[-1]
    assert heads <= V7X_LANES
    c_dt = ssd_w + conv_dim
    c_uv = c_dt + heads
    off_u, off_z, off_xs = 0, 2 * gmlp_w, 2 * gmlp_w + ssd_w

    h = x.reshape(t_len, d)
    out = None
    for layer in range(depth):
        w = w_in[layer]
        w_t = jnp.swapaxes(w, 0, 1).astype(BF16)
        w_dt = jnp.pad(w_t[c_dt:c_uv], ((0, V7X_LANES - heads), (0, 0)))
        p, dt_raw = _inproj(h, mix_norm_g[layer], w_t, w_dt, ssd_conv_w[layer], ssd_conv_b[layer],
                            2 * gmlp_w, ssd_w, c_uv)
        groups = (conv_dim - ssd_w) // (2 * SSD_STATE)
        y_ssd = _ssd(p, dt_raw, ssd_dt_bias[layer], ssd_a_log[layer], ssd_d[layer], ssd_norm_g[layer],
                     groups, off_z, off_xs)
        y_gmlp = _gmlp(p, gmlp_v_norm_g[layer], gmlp_w_s[layer], gmlp_b_s[layer],
                       gmlp_out_norm_g[layer], off_u)
        x1 = _outproj(y_ssd, y_gmlp, w_out[layer].astype(BF16), h)

        xp, idx_pad, gate_pad = _router(x1, ffn_norm_g[layer], w_router[layer], b_router[layer])
        plan = _route_plan(idx_pad[:, :TOP_K], w_router.shape[-1])
        act = _moe_up(xp, plan, w_gate_up[layer], b_gate_up[layer])
        y_rows = _moe_down(act, plan, w_down[layer], b_down[layer])
        last = layer == depth - 1
        assert last, "multi-layer stacking needs an un-normalised combine"
        out = _combine(y_rows, plan["dest"], gate_pad, x1, final_norm_g)
    return out.reshape(bsz, t_len, d)
```

```python
import functools
import math

import jax
import jax.numpy as jnp
from jax import lax
from jax.experimental import pallas as pl
from jax.experimental.pallas import tpu as pltpu

F32, BF16, U32, I32 = jnp.float32, jnp.bfloat16, jnp.uint32, jnp.int32

EPS = 1e-5
SSD_STATE = 128
SSD_CONV = 4
CHUNK = 128
TOP_K = 4
SSD_ROWS = 512
SWIGLU_ALPHA = 1.702
SWIGLU_LIMIT = 7.0

V7X_LANES = 128
V7X_SUBLANES = 8
V7X_VMEM_BYTES = 64 * 1024 * 1024
VMEM_LIMIT = V7X_VMEM_BYTES - 8 * 1024 * 1024

MOE_BLK = 256
MOE_NSB = 6
ISSUE_UNROLL = 8
CAST_ROWS = 128
HI_MASK = 0xFFFF0000


_gcd = math.gcd


def _largest_tile(total, cap, quantum=V7X_LANES):
    best = None
    t = quantum
    while t <= min(total, cap):
        if total % t == 0:
            best = t
        t += quantum
    assert best is not None, (total, cap, quantum)
    return best


def _params(semantics):
    return pltpu.CompilerParams(dimension_semantics=semantics, vmem_limit_bytes=VMEM_LIMIT)


def _silu(x):
    return x * jax.nn.sigmoid(x)


def _gelu(x):
    return 0.5 * x * (1.0 + lax.erf(x * (2.0 ** -0.5)))


def _softplus(x):
    return jnp.maximum(x, 0.0) + jnp.log1p(jnp.exp(-jnp.abs(x)))


def _pack_pair(lo, hi):
    lo_bits = lax.bitcast_convert_type(lo.astype(BF16).astype(F32), U32)
    hi_bits = lax.bitcast_convert_type(hi.astype(BF16).astype(F32), U32)
    return (lo_bits >> 16) | (hi_bits & jnp.uint32(HI_MASK))


def _unpack_pair(u):
    lo = lax.bitcast_convert_type(u << 16, F32)
    hi = lax.bitcast_convert_type(u & jnp.uint32(HI_MASK), F32)
    return lo, hi


NT_DIMS = (((1,), (1,)), ((), ()))


def _inproj_kernel(x_hbm, g_ref, w_ref, wdt_ref, cw_ref, cb_ref, p_ref, dt_ref, xbuf, xn_ref, halo_ref, sem,
                   *, n_gelu, n_silu):
    i = pl.program_id(0)
    j = pl.program_id(1)
    tm = xbuf.shape[0]
    rows = _largest_tile(tm, 128, V7X_SUBLANES)
    tail = halo_ref.shape[1]

    def x_copy(ii):
        return pltpu.make_async_copy(x_hbm.at[pl.ds(pl.multiple_of(ii * tm, tm), tm), :], xbuf, sem.at[0])

    @pl.when(j == 0)
    def _():
        @pl.when(i == 0)
        def _():
            x_copy(0).start()
            halo_ref[...] = jnp.zeros(halo_ref.shape, F32)

        x_copy(i).wait()

        def norm(r, carry):
            sl = pl.ds(pl.multiple_of(r * rows, rows), rows)
            x = xbuf[sl, :]
            ms = jnp.mean(x * x, axis=-1, keepdims=True)
            xn = (x * lax.rsqrt(ms + EPS) * g_ref[...]).astype(BF16)
            xn_ref[sl, :] = xn
            dt_ref[sl, :] = lax.dot_general(xn, wdt_ref[...], NT_DIMS, preferred_element_type=F32)
            return carry

        lax.fori_loop(0, tm // rows, norm, 0)

        @pl.when(i + 1 < pl.num_programs(0))
        def _():
            x_copy(i + 1).start()

    def project():
        return lax.dot_general(xn_ref[...], w_ref[...], NT_DIMS, preferred_element_type=F32)

    @pl.when(j < n_gelu)
    def _():
        p_ref[...] = _gelu(project()).astype(BF16)

    @pl.when((j >= n_gelu) & (j < n_gelu + n_silu))
    def _():
        p_ref[...] = _silu(project()).astype(BF16)

    @pl.when(j >= n_gelu + n_silu)
    def _():
        x = project()
        jc = j - (n_gelu + n_silu)
        ext = jnp.concatenate([halo_ref[jc], x], axis=0)
        acc = cb_ref[...] + cw_ref[SSD_CONV - 1:SSD_CONV, :] * ext
        for s in range(1, SSD_CONV):
            acc = acc + cw_ref[SSD_CONV - 1 - s:SSD_CONV - s, :] * pltpu.roll(ext, s, axis=0)
        p_ref[...] = _silu(acc[tail:, :]).astype(BF16)
        halo_ref[jc] = x[tm - tail:tm, :]


def _inproj(x, g, w_t, w_dt, conv_w, conv_b, n_gelu_cols, n_silu_cols, row_gelu):
    t_len, d = x.shape
    n_conv_cols = conv_b.shape[0]
    n_out = n_gelu_cols + n_silu_cols + n_conv_cols
    assert row_gelu + n_gelu_cols == w_t.shape[0] and row_gelu >= n_silu_cols + n_conv_cols
    tm = _largest_tile(t_len, 1024, V7X_SUBLANES)
    tn = _largest_tile(_gcd(_gcd(n_gelu_cols, n_silu_cols), n_conv_cols), 512)
    n_gelu, n_silu, n_conv = n_gelu_cols // tn, n_silu_cols // tn, n_conv_cols // tn
    tail = 2 * V7X_SUBLANES
    assert row_gelu % tail == 0 and tn % tail == 0
    conv_map = lambda i, j: (0, jnp.maximum(j - (n_gelu + n_silu), 0))
    w_map = lambda i, j: (
        tail * jnp.where(j < n_gelu, row_gelu // tail + j * (tn // tail), (j - n_gelu) * (tn // tail)), 0)
    return pl.pallas_call(
        functools.partial(_inproj_kernel, n_gelu=n_gelu, n_silu=n_silu),
        grid=(t_len // tm, n_out // tn),
        in_specs=[
            pl.BlockSpec(memory_space=pl.ANY),
            pl.BlockSpec((1, d), lambda i, j: (0, 0)),
            pl.BlockSpec((pl.Element(tn), pl.Element(d)), w_map),
            pl.BlockSpec((V7X_LANES, d), lambda i, j: (0, 0)),
            pl.BlockSpec((SSD_CONV, tn), conv_map),
            pl.BlockSpec((1, tn), conv_map),
        ],
        out_specs=[
            pl.BlockSpec((tm, tn), lambda i, j: (i, j)),
            pl.BlockSpec((tm, V7X_LANES), lambda i, j: (i, 0)),
        ],
        out_shape=[
            jax.ShapeDtypeStruct((t_len, n_out), BF16),
            jax.ShapeDtypeStruct((t_len, V7X_LANES), F32),
        ],
        scratch_shapes=[
            pltpu.VMEM((tm, d), F32),
            pltpu.VMEM((tm, d), BF16),
            pltpu.VMEM((n_conv, tail, tn), F32),
            pltpu.SemaphoreType.DMA((1,)),
        ],
        compiler_params=_params(("arbitrary", "arbitrary")),
        name="inproj",
    )(x, g.reshape(1, d), w_t, w_dt, conv_w, conv_b.reshape(1, n_conv_cols))


def _split3(x):
    hi = x.astype(BF16)
    r1 = x - hi.astype(F32)
    mid = r1.astype(BF16)
    lo = (r1 - mid.astype(F32)).astype(BF16)
    return hi, mid, lo


def _ssd_kernel(sz_ref, xs_ref, b_ref, c_ref, dt_ref, dtb_ref, alog_ref, dsk_ref, ng_ref, y_ref, state_ref,
                *, hpg, hd):
    width = hpg * hd
    lanes = V7X_LANES
    per = lanes // hd

    @pl.when(pl.program_id(1) == 0)
    def _():
        state_ref[...] = jnp.zeros(state_ref.shape, F32)

    ri = lax.broadcasted_iota(I32, (CHUNK, CHUNK), 0)
    ci = lax.broadcasted_iota(I32, (CHUNK, CHUNK), 1)
    causal = ri >= ci
    tril = causal.astype(BF16)
    neg_a = -jnp.exp(alog_ref[...])

    def expand(q):
        cols = []
        for j in range(width // lanes):
            blk = jnp.broadcast_to(q[:, j * per:j * per + 1], (CHUNK, lanes))
            for t in range(1, per):
                nxt = jnp.broadcast_to(q[:, j * per + t:j * per + t + 1], (CHUNK, lanes))
                blk = jnp.where(ci >= t * hd, nxt, blk)
            cols.append(blk)
        return jnp.concatenate(cols, axis=1)

    def chunk(k, carry):
        rows = pl.ds(pl.multiple_of(k * CHUNK, CHUNK), CHUNK)
        xs = xs_ref[rows, :].astype(F32)
        bm = b_ref[rows, :]
        cm = c_ref[rows, :]

        dt = _softplus(dt_ref[rows, :] + dtb_ref[...])
        a_dt = dt * neg_a
        acum = sum(jnp.dot(tril, piece, preferred_element_type=F32) for piece in _split3(a_dt))
        acum_t = acum.T
        acum_f = expand(acum)
        last = acum_f[CHUNK - 1:CHUNK, :]
        xdt = xs * expand(dt)

        cb = lax.dot_general(cm, bm, (((1,), (1,)), ((), ())), preferred_element_type=F32)
        y_cols = []
        for j in range(width // lanes):
            x_col = xdt[:, j * lanes:(j + 1) * lanes]
            y_col = jnp.zeros((CHUNK, lanes), F32)
            for t in range(per):
                h = j * per + t
                seg = acum[:, h:h + 1] - acum_t[h:h + 1, :]
                lmat = jnp.exp(jnp.where(causal, seg, -jnp.inf))
                m_h = (cb * lmat).astype(BF16)
                mask = (ci >= t * hd) & (ci < (t + 1) * hd)
                x_h = jnp.where(mask, x_col, 0.0).astype(BF16)
                y_col = y_col + jnp.dot(m_h, x_h, preferred_element_type=F32)
            y_cols.append(y_col)
        y_diag = jnp.concatenate(y_cols, axis=1)

        state = state_ref[...]
        y_off = jnp.dot(cm, state.astype(BF16), preferred_element_type=F32) * jnp.exp(acum_f)
        x_dec = (xdt * jnp.exp(last - acum_f)).astype(BF16)
        s_new = lax.dot_general(bm, x_dec, (((0,), (0,)), ((), ())), preferred_element_type=F32)
        state_ref[...] = state * jnp.exp(last) + s_new

        yg = (y_diag + y_off + xs * dsk_ref[...]) * sz_ref[rows, :].astype(F32)
        yn = yg * lax.rsqrt(jnp.mean(yg * yg, axis=-1, keepdims=True) + EPS) * ng_ref[...]
        y_ref[rows, :] = yn.astype(BF16)
        return carry

    lax.fori_loop(0, y_ref.shape[0] // CHUNK, chunk, 0, unroll=2)


def _ssd(p, dt_raw, dt_bias, a_log, d_skip, norm_g, groups, off_z, off_xs):
    t_len = p.shape[0]
    heads = dt_bias.shape[0]
    width = norm_g.shape[0]
    hd = width // heads
    hpg = heads // groups
    gw = hpg * hd
    n_st = SSD_STATE
    rows = _largest_tile(t_len, SSD_ROWS, CHUNK)
    assert V7X_LANES % hd == 0 and gw % V7X_LANES == 0 and hpg <= V7X_LANES
    assert width // groups == gw
    off_b = off_xs + width
    off_c = off_b + groups * n_st
    assert off_z % gw == 0 and off_xs % gw == 0 and off_b % n_st == 0

    def per_group_lanes(v):
        v = v.reshape(groups, 1, hpg)
        return jnp.pad(v, ((0, 0), (0, 0), (0, V7X_LANES - hpg)))

    dt_g = dt_raw[:, :heads].reshape(t_len, groups, hpg).transpose(1, 0, 2)
    dt_g = jnp.pad(dt_g, ((0, 0), (0, 0), (0, V7X_LANES - hpg)))

    grp = lambda g, c: (g, 0, 0)
    return pl.pallas_call(
        functools.partial(_ssd_kernel, hpg=hpg, hd=hd),
        grid=(groups, t_len // rows),
        in_specs=[
            pl.BlockSpec((rows, gw), lambda g, c: (c, off_z // gw + g)),
            pl.BlockSpec((rows, gw), lambda g, c: (c, off_xs // gw + g)),
            pl.BlockSpec((rows, n_st), lambda g, c: (c, off_b // n_st + g)),
            pl.BlockSpec((rows, n_st), lambda g, c: (c, off_c // n_st + g)),
            pl.BlockSpec((None, rows, V7X_LANES), lambda g, c: (g, c, 0)),
            pl.BlockSpec((None, 1, V7X_LANES), grp),
            pl.BlockSpec((None, 1, V7X_LANES), grp),
            pl.BlockSpec((1, gw), lambda g, c: (0, g)),
            pl.BlockSpec((1, gw), lambda g, c: (0, g)),
        ],
        out_specs=pl.BlockSpec((rows, gw), lambda g, c: (c, g)),
        out_shape=jax.ShapeDtypeStruct((t_len, width), BF16),
        scratch_shapes=[pltpu.VMEM((n_st, gw), F32)],
        compiler_params=_params(("arbitrary", "arbitrary")),
        name="ssd",
    )(p, p, p, p, dt_g, per_group_lanes(dt_bias), per_group_lanes(a_log),
      jnp.repeat(d_skip, hd).reshape(1, width), norm_g.reshape(1, width))


def _row_sums(x):
    ones = jnp.ones((x.shape[1], V7X_LANES), BF16)
    hi = x.astype(BF16)
    lo = (x - hi.astype(F32)).astype(BF16)
    return (jnp.dot(hi, ones, preferred_element_type=F32) + jnp.dot(lo, ones, preferred_element_type=F32))


def _gmlp_kernel(u_ref, v_ref, vg_ref, ws_ref, bs_ref, og_ref, y_ref, yscr_ref, *, heads, hd):
    lanes = V7X_LANES
    reps = hd // lanes
    ri = lax.broadcasted_iota(I32, (CHUNK, CHUNK), 0)
    ci = lax.broadcasted_iota(I32, (CHUNK, CHUNK), 1)
    causal = ri >= ci
    ssq = jnp.zeros((CHUNK, lanes), F32)
    invs = []
    for h in range(heads):
        v_h = v_ref[:, h * hd:(h + 1) * hd].astype(F32)
        invs.append(lax.rsqrt(_row_sums(v_h * v_h) / hd + EPS))
    for h in range(heads):
        cols = slice(h * hd, (h + 1) * hd)
        u_h = u_ref[:, cols].astype(F32)
        v_h = v_ref[:, cols].astype(F32)
        v_n = v_h * jnp.tile(invs[h], (1, reps)) * vg_ref[:, cols]
        w_h = jnp.where(causal, ws_ref[h], 0.0).astype(BF16)
        sv = jnp.dot(w_h, v_n.astype(BF16), preferred_element_type=F32) + jnp.tile(bs_ref[h], (1, reps))
        y_h = u_h * sv
        yscr_ref[:, cols] = y_h
        y_sq = y_h * y_h
        for r in range(reps):
            ssq = ssq + y_sq[:, r * lanes:(r + 1) * lanes]
    scale = lax.rsqrt(_row_sums(ssq) / (heads * hd) + EPS)
    y_ref[...] = (yscr_ref[...] * jnp.tile(scale, (1, heads * reps)) * og_ref[...]).astype(BF16)


def _gmlp(p, v_norm_g, w_s, b_s, out_norm_g, off_u):
    t_len = p.shape[0]
    heads = w_s.shape[0]
    width = out_norm_g.shape[0]
    hd = width // heads
    assert w_s.shape[1] == CHUNK and off_u % width == 0 and hd % V7X_LANES == 0
    b_lanes = jnp.broadcast_to(b_s[:, :, None], (heads, CHUNK, V7X_LANES))
    return pl.pallas_call(
        functools.partial(_gmlp_kernel, heads=heads, hd=hd),
        grid=(t_len // CHUNK,),
        in_specs=[
            pl.BlockSpec((CHUNK, width), lambda c: (c, off_u // width)),
            pl.BlockSpec((CHUNK, width), lambda c: (c, off_u // width + 1)),
            pl.BlockSpec((1, width), lambda c: (0, 0)),
            pl.BlockSpec((heads, CHUNK, CHUNK), lambda c: (0, 0, 0)),
            pl.BlockSpec((heads, CHUNK, V7X_LANES), lambda c: (0, 0, 0)),
            pl.BlockSpec((1, width), lambda c: (0, 0)),
        ],
        out_specs=pl.BlockSpec((CHUNK, width), lambda c: (c, 0)),
        out_shape=jax.ShapeDtypeStruct((t_len, width), BF16),
        scratch_shapes=[pltpu.VMEM((CHUNK, width), F32)],
        compiler_params=_params(("arbitrary",)),
        name="gmlp",
    )(p, p, v_norm_g.reshape(1, width), w_s, b_lanes, out_norm_g.reshape(1, width))


def _outproj_kernel(ys_ref, yg_ref, wa_ref, wb_ref, x_ref, o_ref):
    acc = jnp.dot(ys_ref[...], wa_ref[...], preferred_element_type=F32)
    acc = acc + jnp.dot(yg_ref[...], wb_ref[...], preferred_element_type=F32)
    o_ref[...] = x_ref[...] + acc


def _outproj(y_ssd, y_gmlp, w_out, x):
    t_len, d = x.shape
    ka, kb = y_ssd.shape[1], y_gmlp.shape[1]
    assert ka == kb and w_out.shape[0] == ka + kb
    tm = _largest_tile(t_len, 512, V7X_SUBLANES)
    tn = _largest_tile(d, 512)
    return pl.pallas_call(
        _outproj_kernel,
        grid=(t_len // tm, d // tn),
        in_specs=[
            pl.BlockSpec((tm, ka), lambda i, j: (i, 0)),
            pl.BlockSpec((tm, kb), lambda i, j: (i, 0)),
            pl.BlockSpec((ka, tn), lambda i, j: (0, j)),
            pl.BlockSpec((kb, tn), lambda i, j: (1, j)),
            pl.BlockSpec((tm, tn), lambda i, j: (i, j)),
        ],
        out_specs=pl.BlockSpec((tm, tn), lambda i, j: (i, j)),
        out_shape=jax.ShapeDtypeStruct((t_len, d), F32),
        compiler_params=_params(("arbitrary", "arbitrary")),
        name="outproj",
    )(y_ssd, y_gmlp, w_out, w_out, x)


def _router_kernel(x_ref, g_ref, wr_ref, br_ref, xp_ref, idx_ref, gate_ref, *, n_exp):
    x = x_ref[...]
    tm, d = x.shape
    half = d // 2
    ms = jnp.mean(x * x, axis=-1, keepdims=True)
    xn = x * lax.rsqrt(ms + EPS) * g_ref[...]
    xp_ref[...] = _pack_pair(xn[:, :half], xn[:, half:])

    logits = jnp.dot(xn.astype(BF16), wr_ref[...], preferred_element_type=F32) + br_ref[...]
    lane = lax.broadcasted_iota(I32, (tm, V7X_LANES), 1)
    vals = jnp.where(lane < n_exp, logits, -jnp.inf)
    idx_out = jnp.zeros((tm, V7X_LANES), I32)
    val_out = jnp.full((tm, V7X_LANES), -jnp.inf, F32)
    for k in range(TOP_K):
        m = jnp.max(vals, axis=-1, keepdims=True)
        sel = jnp.min(jnp.where(vals == m, lane, V7X_LANES), axis=-1, keepdims=True)
        idx_out = jnp.where(lane == k, sel, idx_out)
        val_out = jnp.where(lane == k, m, val_out)
        vals = jnp.where(lane == sel, -jnp.inf, vals)
    e = jnp.exp(val_out - jnp.max(val_out, axis=-1, keepdims=True))
    idx_ref[...] = idx_out
    gate_ref[...] = e / jnp.sum(e, axis=-1, keepdims=True)


def _router(x1, g, w_router, b_router):
    t_len, d = x1.shape
    n_exp = w_router.shape[1]
    assert n_exp <= V7X_LANES and TOP_K <= n_exp
    tm = _largest_tile(t_len, 256, V7X_SUBLANES)
    wr = jnp.pad(w_router, ((0, 0), (0, V7X_LANES - n_exp))).astype(BF16)
    br = jnp.pad(b_router, (0, V7X_LANES - n_exp)).reshape(1, V7X_LANES)
    return pl.pallas_call(
        functools.partial(_router_kernel, n_exp=n_exp),
        grid=(t_len // tm,),
        in_specs=[
            pl.BlockSpec((tm, d), lambda i: (i, 0)),
            pl.BlockSpec((1, d), lambda i: (0, 0)),
            pl.BlockSpec((d, V7X_LANES), lambda i: (0, 0)),
            pl.BlockSpec((1, V7X_LANES), lambda i: (0, 0)),
        ],
        out_specs=[
            pl.BlockSpec((tm, d // 2), lambda i: (i, 0)),
            pl.BlockSpec((tm, V7X_LANES), lambda i: (i, 0)),
            pl.BlockSpec((tm, V7X_LANES), lambda i: (i, 0)),
        ],
        out_shape=[
            jax.ShapeDtypeStruct((t_len, d // 2), U32),
            jax.ShapeDtypeStruct((t_len, V7X_LANES), I32),
            jax.ShapeDtypeStruct((t_len, V7X_LANES), F32),
        ],
        compiler_params=_params(("arbitrary",)),
        name="router",
    )(x1, g.reshape(1, d), wr, br)


def _route_plan(top_idx, n_exp):
    t_len, top_k = top_idx.shape
    n_assign = t_len * top_k
    e = top_idx.reshape(n_assign)
    onehot = (e[:, None] == jnp.arange(n_exp, dtype=I32)[None, :]).astype(I32)
    csum = jnp.cumsum(onehot, axis=0)
    rank = jnp.sum(onehot * csum, axis=1) - 1
    counts = csum[-1]
    nb = (counts + MOE_BLK - 1) // MOE_BLK
    sb_end = jnp.cumsum(nb)
    sb_start = sb_end - nb
    dest = sb_start[e] * MOE_BLK + rank
    n_sb = n_assign // MOE_BLK + n_exp
    n_rows = n_sb * MOE_BLK
    row_tok = jnp.zeros((n_rows,), I32).at[dest].set(jnp.arange(n_assign, dtype=I32) // top_k)

    nv = (nb + MOE_NSB - 1) // MOE_NSB
    v_end = jnp.cumsum(nv)
    v_start = v_end - nv
    n_pass = -(-n_sb // MOE_NSB) + n_exp
    vid = jnp.arange(n_pass, dtype=I32)
    ve = jnp.minimum(jnp.searchsorted(v_end, vid, side='right'), n_exp - 1).astype(I32)
    q = vid - v_start[ve]
    valid = vid < v_end[-1]
    vs = jnp.where(valid, sb_start[ve] + q * MOE_NSB, 0).astype(I32)
    vn = jnp.where(valid, jnp.clip(nb[ve] - q * MOE_NSB, 0, MOE_NSB), 0).astype(I32)
    last_e = ve[jnp.maximum(v_end[-1] - 1, 0)]
    ve = jnp.where(valid, ve, last_e).astype(I32)
    used = sb_end[-1:].astype(I32)
    return dict(row_tok=row_tok, dest=dest.astype(I32), ve=ve, vs=vs, vn=vn, used=used, n_rows=n_rows,
                n_pass=n_pass)


def _for_each_paired(n, compute, emit):
    def pair(q, carry):
        first = compute(2 * q)
        second = compute(2 * q + 1)
        emit(2 * q, first)
        emit(2 * q + 1, second)
        return carry

    lax.fori_loop(0, lax.shift_right_logical(n, 1), pair, 0)

    @pl.when((n & 1) == 1)
    def _():
        emit(n - 1, compute(n - 1))


def _zero_tail(obuf, n_used, n_sb, n_j, out_copy):
    tile = obuf.shape[2]
    obuf[0, 0:MOE_BLK, :] = jnp.zeros((MOE_BLK, tile), obuf.dtype)

    def start(sb, carry):
        for jj in range(n_j):
            out_copy(0, 0, pl.multiple_of(sb * MOE_BLK, MOE_BLK), jj * tile).start()
        return carry

    def wait(sb, carry):
        for jj in range(n_j):
            out_copy(0, 0, 0, 0).wait()
        return carry

    lax.fori_loop(n_used, n_sb, start, 0)
    lax.fori_loop(n_used, n_sb, wait, 0)


def _moe_up_kernel(ve_ref, vs_ref, vn_ref, used_ref, tok_ref, xp_hbm, wg_ref, wl_ref, bg_ref, bl_ref, act_hbm,
                   xbuf, wc_ref, obuf, xsem, osem, ocnt, *, tf, n_j):
    v = pl.program_id(0)
    j = pl.program_id(1)
    n_pass = pl.num_programs(0)
    half = xbuf.shape[2]
    slot = v % 2
    n_sub = vn_ref[v]

    def row_copy(tok, sl, r0, u):
        window = xbuf.at[sl, pl.ds(r0, ISSUE_UNROLL), :]
        return pltpu.make_async_copy(xp_hbm.at[pl.ds(tok, 1), :], window.at[pl.ds(u, 1), :], xsem.at[sl])

    def gather_start(vv, sl):
        base = vs_ref[vv] * MOE_BLK

        def body(q, carry):
            r0 = pl.multiple_of(q * ISSUE_UNROLL, ISSUE_UNROLL)
            for u in range(ISSUE_UNROLL):
                row_copy(tok_ref[base + r0 + u], sl, r0, u).start()
            return carry

        lax.fori_loop(0, vn_ref[vv] * (MOE_BLK // ISSUE_UNROLL), body, 0)

    def gather_wait(vv, sl):
        def body(i, carry):
            r0 = pl.multiple_of(i * MOE_BLK, MOE_BLK)
            pltpu.make_async_copy(xp_hbm.at[pl.ds(0, MOE_BLK), :], xbuf.at[sl, pl.ds(r0, MOE_BLK), :],
                                  xsem.at[sl]).wait()
            return carry

        lax.fori_loop(0, vn_ref[vv], body, 0)

    def out_copy(p, r0, row, col):
        return pltpu.make_async_copy(obuf.at[p, pl.ds(r0, MOE_BLK), :],
                                     act_hbm.at[pl.ds(row, MOE_BLK), pl.ds(col, tf)], osem.at[p])

    def drain(p):
        def body(i, carry):
            out_copy(p, 0, 0, 0).wait()
            return carry

        lax.fori_loop(0, ocnt[p], body, 0)
        ocnt[p] = 0

    @pl.when(j == 0)
    def _():
        @pl.when(v == 0)
        def _():
            ocnt[0] = 0
            ocnt[1] = 0
            gather_start(0, 0)

        gather_wait(v, slot)

    step = v * n_j + j
    p = step % 2
    drain(p)

    @pl.when(n_sub > 0)
    def _():
        nxt = jnp.minimum(v + 1, n_pass - 1)
        n_next = jnp.where(v + 1 < n_pass, vn_ref[nxt], 0)
        groups = n_next * (MOE_BLK // (ISSUE_UNROLL * n_j))
        first = j * groups
        base = vs_ref[nxt] * MOE_BLK
        n_cast = wc_ref.shape[0] // CAST_ROWS
        both = jnp.minimum(groups, n_cast)

        def request(q):
            r0 = pl.multiple_of((first + q) * ISSUE_UNROLL, ISSUE_UNROLL)
            for u in range(ISSUE_UNROLL):
                row_copy(tok_ref[base + r0 + u], 1 - slot, r0, u).start()

        def cast(q, with_request):
            rows = pl.ds(pl.multiple_of(q * CAST_ROWS, CAST_ROWS), CAST_ROWS)
            g = wg_ref[rows, :]
            l = wl_ref[rows, :]
            if with_request:
                request(q)
            wc_ref[rows, 0:tf] = g.astype(BF16)
            wc_ref[rows, tf:2 * tf] = l.astype(BF16)

        def cast_and_request(q, carry):
            cast(q, True)
            return carry

        def cast_only(q, carry):
            cast(q, False)
            return carry

        def request_only(q, carry):
            request(q)
            return carry

        lax.fori_loop(0, both, cast_and_request, 0)
        lax.fori_loop(both, n_cast, cast_only, 0)
        lax.fori_loop(both, groups, request_only, 0)

        def compute(i):
            r0 = pl.multiple_of(i * MOE_BLK, MOE_BLK)
            lo, hi = _unpack_pair(xbuf[slot, pl.ds(r0, MOE_BLK), :])
            h = jnp.dot(lo.astype(BF16), wc_ref[0:half, :], preferred_element_type=F32)
            h = h + jnp.dot(hi.astype(BF16), wc_ref[half:2 * half, :], preferred_element_type=F32)
            glu = jnp.minimum(h[:, 0:tf] + bg_ref[...], SWIGLU_LIMIT)
            lin = jnp.clip(h[:, tf:2 * tf] + bl_ref[...], -SWIGLU_LIMIT, SWIGLU_LIMIT)
            return (glu * jax.nn.sigmoid(SWIGLU_ALPHA * glu) * (lin + 1.0)).astype(BF16)

        def emit(i, act):
            r0 = pl.multiple_of(i * MOE_BLK, MOE_BLK)
            obuf[p, pl.ds(r0, MOE_BLK), :] = act
            row = pl.multiple_of((vs_ref[v] + i) * MOE_BLK, MOE_BLK)
            out_copy(p, r0, row, pl.multiple_of(j * tf, tf)).start()

        _for_each_paired(n_sub, compute, emit)
        ocnt[p] = n_sub

    @pl.when(step == n_pass * n_j - 1)
    def _():
        drain(0)
        drain(1)
        _zero_tail(obuf, used_ref[0], act_hbm.shape[0] // MOE_BLK, n_j, out_copy)


def _moe_up(xp, plan, w_gate_up, b_gate_up):
    n_exp, d, two_f = w_gate_up.shape
    f = two_f // 2
    half = xp.shape[1]
    assert half * 2 == d
    tf = _largest_tile(f, 256)
    n_j = f // tf
    rows_pass = MOE_NSB * MOE_BLK
    assert MOE_BLK % (ISSUE_UNROLL * n_j) == 0 and d % CAST_ROWS == 0

    def w_map(off):
        def index_map(v, j, ve, vs, vn, used, tok):
            jj = jnp.where(vn[v] > 0, j, n_j - 1)
            return (ve[v], 0, off + jj)
        return index_map

    grid_spec = pltpu.PrefetchScalarGridSpec(
        num_scalar_prefetch=5,
        grid=(plan["n_pass"], n_j),
        in_specs=[
            pl.BlockSpec(memory_space=pl.ANY),
            pl.BlockSpec((None, d, tf), w_map(0)),
            pl.BlockSpec((None, d, tf), w_map(n_j)),
            pl.BlockSpec((None, 1, tf), w_map(0)),
            pl.BlockSpec((None, 1, tf), w_map(n_j)),
        ],
        out_specs=pl.BlockSpec(memory_space=pl.ANY),
        scratch_shapes=[
            pltpu.VMEM((2, rows_pass, half), U32),
            pltpu.VMEM((d, 2 * tf), BF16),
            pltpu.VMEM((2, rows_pass, tf), BF16),
            pltpu.SemaphoreType.DMA((2,)),
            pltpu.SemaphoreType.DMA((2,)),
            pltpu.SMEM((2,), I32),
        ],
    )
    b3 = b_gate_up.reshape(n_exp, 1, two_f)
    return pl.pallas_call(
        functools.partial(_moe_up_kernel, tf=tf, n_j=n_j),
        grid_spec=grid_spec,
        out_shape=jax.ShapeDtypeStruct((plan["n_rows"], f), BF16),
        compiler_params=_params(("arbitrary", "arbitrary")),
        name="moe_up",
    )(plan["ve"], plan["vs"], plan["vn"], plan["used"], plan["row_tok"], xp, w_gate_up, w_gate_up, b3, b3)


def _moe_down_kernel(ve_ref, vs_ref, vn_ref, used_ref, act_hbm, wa_ref, wb_ref, ba_ref, bb_ref, y_hbm,
                     abuf, wc_ref, obuf, asem, osem, ocnt, *, tn):
    v = pl.program_id(0)
    j = pl.program_id(1)
    n_pass = pl.num_programs(0)
    n_j = pl.num_programs(1)
    slot = v % 2
    n_sub = vn_ref[v]

    def in_copy(row, sl, r0):
        return pltpu.make_async_copy(act_hbm.at[pl.ds(row, MOE_BLK), :],
                                     abuf.at[sl, pl.ds(r0, MOE_BLK), :], asem.at[sl])

    def load_start(vv, sl):
        def body(i, carry):
            r0 = pl.multiple_of(i * MOE_BLK, MOE_BLK)
            in_copy(pl.multiple_of((vs_ref[vv] + i) * MOE_BLK, MOE_BLK), sl, r0).start()
            return carry

        lax.fori_loop(0, vn_ref[vv], body, 0)

    def load_wait(vv, sl):
        def body(i, carry):
            in_copy(0, sl, pl.multiple_of(i * MOE_BLK, MOE_BLK)).wait()
            return carry

        lax.fori_loop(0, vn_ref[vv], body, 0)

    def out_copy(p, r0, row, col):
        return pltpu.make_async_copy(obuf.at[p, pl.ds(r0, MOE_BLK), :],
                                     y_hbm.at[pl.ds(row, MOE_BLK), pl.ds(col, tn)], osem.at[p])

    def drain(p):
        def body(i, carry):
            out_copy(p, 0, 0, 0).wait()
            return carry

        lax.fori_loop(0, ocnt[p], body, 0)
        ocnt[p] = 0

    @pl.when(j == 0)
    def _():
        @pl.when(v == 0)
        def _():
            ocnt[0] = 0
            ocnt[1] = 0
            load_start(0, 0)

        @pl.when(v + 1 < n_pass)
        def _():
            load_start(v + 1, 1 - slot)

        load_wait(v, slot)

    step = v * n_j + j
    p = step % 2
    drain(p)

    @pl.when(n_sub > 0)
    def _():
        wc_ref[:, 0:tn] = wa_ref[...].astype(BF16)
        wc_ref[:, tn:2 * tn] = wb_ref[...].astype(BF16)

        def compute(i):
            r0 = pl.multiple_of(i * MOE_BLK, MOE_BLK)
            y = jnp.dot(abuf[slot, pl.ds(r0, MOE_BLK), :], wc_ref[...], preferred_element_type=F32)
            return _pack_pair(y[:, 0:tn] + ba_ref[...], y[:, tn:2 * tn] + bb_ref[...])

        def emit(i, packed):
            r0 = pl.multiple_of(i * MOE_BLK, MOE_BLK)
            obuf[p, pl.ds(r0, MOE_BLK), :] = packed
            row = pl.multiple_of((vs_ref[v] + i) * MOE_BLK, MOE_BLK)
            out_copy(p, r0, row, pl.multiple_of(j * tn, tn)).start()

        _for_each_paired(n_sub, compute, emit)
        ocnt[p] = n_sub

    @pl.when(step == n_pass * n_j - 1)
    def _():
        drain(0)
        drain(1)
        _zero_tail(obuf, used_ref[0], y_hbm.shape[0] // MOE_BLK, n_j, out_copy)


def _moe_down(act, plan, w_down, b_down):
    n_exp, f, d = w_down.shape
    half = d // 2
    tn = _largest_tile(half, 256)
    n_j = half // tn
    rows_pass = MOE_NSB * MOE_BLK

    def w_map(off):
        def index_map(v, j, ve, vs, vn, used):
            jj = jnp.where(vn[v] > 0, j, n_j - 1)
            return (ve[v], 0, off + jj)
        return index_map

    grid_spec = pltpu.PrefetchScalarGridSpec(
        num_scalar_prefetch=4,
        grid=(plan["n_pass"], n_j),
        in_specs=[
            pl.BlockSpec(memory_space=pl.ANY),
            pl.BlockSpec((None, f, tn), w_map(0)),
            pl.BlockSpec((None, f, tn), w_map(n_j)),
            pl.BlockSpec((None, 1, tn), w_map(0)),
            pl.BlockSpec((None, 1, tn), w_map(n_j)),
        ],
        out_specs=pl.BlockSpec(memory_space=pl.ANY),
        scratch_shapes=[
            pltpu.VMEM((2, rows_pass, f), BF16),
            pltpu.VMEM((f, 2 * tn), BF16),
            pltpu.VMEM((2, rows_pass, tn), U32),
            pltpu.SemaphoreType.DMA((2,)),
            pltpu.SemaphoreType.DMA((2,)),
            pltpu.SMEM((2,), I32),
        ],
    )
    b3 = b_down.reshape(n_exp, 1, d)
    return pl.pallas_call(
        functools.partial(_moe_down_kernel, tn=tn),
        grid_spec=grid_spec,
        out_shape=jax.ShapeDtypeStruct((plan["n_rows"], half), U32),
        compiler_params=_params(("arbitrary", "arbitrary")),
        name="moe_down",
    )(plan["ve"], plan["vs"], plan["vn"], plan["used"], act, w_down, w_down, b3, b3)


def _combine_kernel(dest_ref, y_hbm, gate_ref, x_ref, g_ref, o_ref, ybuf, sem):
    i = pl.program_id(0)
    n_i = pl.num_programs(0)
    tm, d = x_ref.shape
    half = d // 2
    slot = i % 2

    def row_copy(row, sl, k, t0, u):
        window = ybuf.at[sl, k, pl.ds(t0, ISSUE_UNROLL), :]
        return pltpu.make_async_copy(y_hbm.at[pl.ds(row, 1), :], window.at[pl.ds(u, 1), :], sem.at[sl])

    def request(ii, sl, q):
        t0 = pl.multiple_of(q * ISSUE_UNROLL, ISSUE_UNROLL)
        for u in range(ISSUE_UNROLL):
            for k in range(TOP_K):
                row_copy(dest_ref[(ii * tm + t0 + u) * TOP_K + k], sl, k, t0, u).start()

    def gather_wait(sl):
        for k in range(TOP_K):
            pltpu.make_async_copy(y_hbm.at[pl.ds(0, tm), :], ybuf.at[sl, k], sem.at[sl]).wait()

    def combine(q, prefetch):
        rows = pl.ds(pl.multiple_of(q * ISSUE_UNROLL, ISSUE_UNROLL), ISSUE_UNROLL)
        acc_lo = x_ref[rows, 0:half]
        acc_hi = x_ref[rows, half:d]
        for k in range(TOP_K):
            lo, hi = _unpack_pair(ybuf[slot, k, rows, :])
            gk = gate_ref[rows, k:k + 1]
            acc_lo = acc_lo + gk * lo
            acc_hi = acc_hi + gk * hi
        if prefetch:
            request(i + 1, 1 - slot, q)
        ms = (jnp.sum(acc_lo * acc_lo, axis=-1, keepdims=True)
              + jnp.sum(acc_hi * acc_hi, axis=-1, keepdims=True)) / d
        r = lax.rsqrt(ms + EPS)
        o_ref[rows, 0:half] = acc_lo * r * g_ref[:, 0:half]
        o_ref[rows, half:d] = acc_hi * r * g_ref[:, half:d]

    n_groups = tm // ISSUE_UNROLL

    @pl.when(i == 0)
    def _():
        lax.fori_loop(0, n_groups, lambda q, c: (request(0, 0, q), c)[1], 0)

    gather_wait(slot)

    @pl.when(i + 1 < n_i)
    def _():
        lax.fori_loop(0, n_groups, lambda q, c: (combine(q, True), c)[1], 0, unroll=2)

    @pl.when(i + 1 == n_i)
    def _():
        lax.fori_loop(0, n_groups, lambda q, c: (combine(q, False), c)[1], 0, unroll=2)


def _combine(y_rows, dest, gate, x1, final_g):
    t_len, d = x1.shape
    half = d // 2
    tm = _largest_tile(t_len, 128, V7X_SUBLANES)
    grid_spec = pltpu.PrefetchScalarGridSpec(
        num_scalar_prefetch=1,
        grid=(t_len // tm,),
        in_specs=[
            pl.BlockSpec(memory_space=pl.ANY),
            pl.BlockSpec((tm, V7X_LANES), lambda i, dest: (i, 0)),
            pl.BlockSpec((tm, d), lambda i, dest: (i, 0)),
            pl.BlockSpec((1, d), lambda i, dest: (0, 0)),
        ],
        out_specs=pl.BlockSpec((tm, d), lambda i, dest: (i, 0)),
        scratch_shapes=[
            pltpu.VMEM((2, TOP_K, tm, half), U32),
            pltpu.SemaphoreType.DMA((2,)),
        ],
    )
    return pl.pallas_call(
        _combine_kernel,
        grid_spec=grid_spec,
        out_shape=jax.ShapeDtypeStruct((t_len, d), F32),
        compiler_params=_params(("arbitrary",)),
        name="combine",
    )(dest, y_rows, gate, x1, final_g.reshape(1, d))


def kernel(x, mix_norm_g, w_in, ssd_conv_w, ssd_conv_b, ssd_dt_bias, ssd_a_log, ssd_d, ssd_norm_g,
           gmlp_v_norm_g, gmlp_w_s, gmlp_b_s, gmlp_out_norm_g, w_out, ffn_norm_g, w_router, b_router,
           w_gate_up, b_gate_up, w_down, b_down, final_norm_g):
    bsz, t_len, d = x.shape
    assert bsz == 1
    depth = w_in.shape[0]
    ssd_w = ssd_norm_g.shape[-1]
    heads = ssd_dt_bias.shape[-1]
    gmlp_w = gmlp_out_norm_g.shape[-1]
    conv_dim = ssd_conv_b.shape[-1]
    assert heads <= V7X_LANES
    c_dt = ssd_w + conv_dim
    c_uv = c_dt + heads
    off_u, off_z, off_xs = 0, 2 * gmlp_w, 2 * gmlp_w + ssd_w

    h = x.reshape(t_len, d)
    out = None
    for layer in range(depth):
        w = w_in[layer]
        w_t = jnp.swapaxes(w, 0, 1).astype(BF16)
        w_dt = jnp.pad(w_t[c_dt:c_uv], ((0, V7X_LANES - heads), (0, 0)))
        p, dt_raw = _inproj(h, mix_norm_g[layer], w_t, w_dt, ssd_conv_w[layer], ssd_conv_b[layer],
                            2 * gmlp_w, ssd_w, c_uv)
        groups = (conv_dim - ssd_w) // (2 * SSD_STATE)
        y_ssd = _ssd(p, dt_raw, ssd_dt_bias[layer], ssd_a_log[layer], ssd_d[layer], ssd_norm_g[layer],
                     groups, off_z, off_xs)
        y_gmlp = _gmlp(p, gmlp_v_norm_g[layer], gmlp_w_s[layer], gmlp_b_s[layer],
                       gmlp_out_norm_g[layer], off_u)
        x1 = _outproj(y_ssd, y_gmlp, w_out[layer].astype(BF16), h)

        xp, idx_pad, gate_pad = _router(x1, ffn_norm_g[layer], w_router[layer], b_router[layer])
        plan = _route_plan(idx_pad[:, :TOP_K], w_router.shape[-1])
        act = _moe_up(xp, plan, w_gate_up[layer], b_gate_up[layer])
        y_rows = _moe_down(act, plan, w_down[layer], b_down[layer])
        last = layer == depth - 1
        assert last, "multi-layer stacking needs an un-normalised combine"
        out = _combine(y_rows, plan["dest"], gate_pad, x1, final_norm_g)
    return out.reshape(bsz, t_len, d)
```

```python
import functools
import math

import jax
import jax.numpy as jnp
from jax import lax
from jax.experimental import pallas as pl
from jax.experimental.pallas import tpu as pltpu

F32, BF16, U32, I32 = jnp.float32, jnp.bfloat16, jnp.uint32, jnp.int32

EPS = 1e-5
SSD_STATE = 128
SSD_CONV = 4
CHUNK = 128
TOP_K = 4
SSD_ROWS = 512
SWIGLU_ALPHA = 1.702
SWIGLU_LIMIT = 7.0

V7X_LANES = 128
V7X_SUBLANES = 8
V7X_VMEM_BYTES = 64 * 1024 * 1024
VMEM_LIMIT = V7X_VMEM_BYTES - 8 * 1024 * 1024

MOE_BLK = 128
MOE_NSB = 12
ISSUE_UNROLL = 8
CAST_ROWS = 128
HI_MASK = 0xFFFF0000


_gcd = math.gcd


def _largest_tile(total, cap, quantum=V7X_LANES):
    best = None
    t = quantum
    while t <= min(total, cap):
        if total % t == 0:
            best = t
        t += quantum
    assert best is not None, (total, cap, quantum)
    return best


def _params(semantics):
    return pltpu.CompilerParams(dimension_semantics=semantics, vmem_limit_bytes=VMEM_LIMIT)


def _silu(x):
    return x * jax.nn.sigmoid(x)


def _gelu(x):
    return 0.5 * x * (1.0 + lax.erf(x * (2.0 ** -0.5)))


def _softplus(x):
    return jnp.maximum(x, 0.0) + jnp.log1p(jnp.exp(-jnp.abs(x)))


def _pack_pair(lo, hi):
    lo_bits = lax.bitcast_convert_type(lo.astype(BF16).astype(F32), U32)
    hi_bits = lax.bitcast_convert_type(hi.astype(BF16).astype(F32), U32)
    return (lo_bits >> 16) | (hi_bits & jnp.uint32(HI_MASK))


def _unpack_pair(u):
    lo = lax.bitcast_convert_type(u << 16, F32)
    hi = lax.bitcast_convert_type(u & jnp.uint32(HI_MASK), F32)
    return lo, hi


NT_DIMS = (((1,), (1,)), ((), ()))


def _inproj_kernel(x_hbm, g_ref, w_ref, wdt_ref, cw_ref, cb_ref, p_ref, dt_ref, xbuf, xn_ref, halo_ref, sem,
                   *, n_gelu, n_silu):
    i = pl.program_id(0)
    j = pl.program_id(1)
    tm = xbuf.shape[0]
    rows = _largest_tile(tm, 128, V7X_SUBLANES)
    tail = halo_ref.shape[1]

    def x_copy(ii):
        return pltpu.make_async_copy(x_hbm.at[pl.ds(pl.multiple_of(ii * tm, tm), tm), :], xbuf, sem.at[0])

    @pl.when(j == 0)
    def _():
        @pl.when(i == 0)
        def _():
            x_copy(0).start()
            halo_ref[...] = jnp.zeros(halo_ref.shape, F32)

        x_copy(i).wait()

        def norm(r, carry):
            sl = pl.ds(pl.multiple_of(r * rows, rows), rows)
            x = xbuf[sl, :]
            ms = jnp.mean(x * x, axis=-1, keepdims=True)
            xn = (x * lax.rsqrt(ms + EPS) * g_ref[...]).astype(BF16)
            xn_ref[sl, :] = xn
            dt_ref[sl, :] = lax.dot_general(xn, wdt_ref[...], NT_DIMS, preferred_element_type=F32)
            return carry

        lax.fori_loop(0, tm // rows, norm, 0)

        @pl.when(i + 1 < pl.num_programs(0))
        def _():
            x_copy(i + 1).start()

    def project():
        return lax.dot_general(xn_ref[...], w_ref[...], NT_DIMS, preferred_element_type=F32)

    @pl.when(j < n_gelu)
    def _():
        p_ref[...] = _gelu(project()).astype(BF16)

    @pl.when((j >= n_gelu) & (j < n_gelu + n_silu))
    def _():
        p_ref[...] = _silu(project()).astype(BF16)

    @pl.when(j >= n_gelu + n_silu)
    def _():
        x = project()
        jc = j - (n_gelu + n_silu)
        ext = jnp.concatenate([halo_ref[jc], x], axis=0)
        acc = cb_ref[...] + cw_ref[SSD_CONV - 1:SSD_CONV, :] * ext
        for s in range(1, SSD_CONV):
            acc = acc + cw_ref[SSD_CONV - 1 - s:SSD_CONV - s, :] * pltpu.roll(ext, s, axis=0)
        p_ref[...] = _silu(acc[tail:, :]).astype(BF16)
        halo_ref[jc] = x[tm - tail:tm, :]


def _inproj(x, g, w_t, w_dt, conv_w, conv_b, n_gelu_cols, n_silu_cols, row_gelu):
    t_len, d = x.shape
    n_conv_cols = conv_b.shape[0]
    n_out = n_gelu_cols + n_silu_cols + n_conv_cols
    assert row_gelu + n_gelu_cols == w_t.shape[0] and row_gelu >= n_silu_cols + n_conv_cols
    tm = _largest_tile(t_len, 1024, V7X_SUBLANES)
    tn = _largest_tile(_gcd(_gcd(n_gelu_cols, n_silu_cols), n_conv_cols), 512)
    n_gelu, n_silu, n_conv = n_gelu_cols // tn, n_silu_cols // tn, n_conv_cols // tn
    tail = 2 * V7X_SUBLANES
    assert row_gelu % tail == 0 and tn % tail == 0
    conv_map = lambda i, j: (0, jnp.maximum(j - (n_gelu + n_silu), 0))
    w_map = lambda i, j: (
        tail * jnp.where(j < n_gelu, row_gelu // tail + j * (tn // tail), (j - n_gelu) * (tn // tail)), 0)
    return pl.pallas_call(
        functools.partial(_inproj_kernel, n_gelu=n_gelu, n_silu=n_silu),
        grid=(t_len // tm, n_out // tn),
        in_specs=[
            pl.BlockSpec(memory_space=pl.ANY),
            pl.BlockSpec((1, d), lambda i, j: (0, 0)),
            pl.BlockSpec((pl.Element(tn), pl.Element(d)), w_map),
            pl.BlockSpec((V7X_LANES, d), lambda i, j: (0, 0)),
            pl.BlockSpec((SSD_CONV, tn), conv_map),
            pl.BlockSpec((1, tn), conv_map),
        ],
        out_specs=[
            pl.BlockSpec((tm, tn), lambda i, j: (i, j)),
            pl.BlockSpec((tm, V7X_LANES), lambda i, j: (i, 0)),
        ],
        out_shape=[
            jax.ShapeDtypeStruct((t_len, n_out), BF16),
            jax.ShapeDtypeStruct((t_len, V7X_LANES), F32),
        ],
        scratch_shapes=[
            pltpu.VMEM((tm, d), F32),
            pltpu.VMEM((tm, d), BF16),
            pltpu.VMEM((n_conv, tail, tn), F32),
            pltpu.SemaphoreType.DMA((1,)),
        ],
        compiler_params=_params(("arbitrary", "arbitrary")),
        name="inproj",
    )(x, g.reshape(1, d), w_t, w_dt, conv_w, conv_b.reshape(1, n_conv_cols))


def _split3(x):
    hi = x.astype(BF16)
    r1 = x - hi.astype(F32)
    mid = r1.astype(BF16)
    lo = (r1 - mid.astype(F32)).astype(BF16)
    return hi, mid, lo


def _ssd_kernel(sz_ref, xs_ref, b_ref, c_ref, dt_ref, dtb_ref, alog_ref, dsk_ref, ng_ref, y_ref, state_ref,
                *, hpg, hd):
    width = hpg * hd
    lanes = V7X_LANES
    per = lanes // hd

    @pl.when(pl.program_id(1) == 0)
    def _():
        state_ref[...] = jnp.zeros(state_ref.shape, F32)

    ri = lax.broadcasted_iota(I32, (CHUNK, CHUNK), 0)
    ci = lax.broadcasted_iota(I32, (CHUNK, CHUNK), 1)
    causal = ri >= ci
    tril = causal.astype(BF16)
    neg_a = -jnp.exp(alog_ref[...])

    def expand(q):
        cols = []
        for j in range(width // lanes):
            blk = jnp.broadcast_to(q[:, j * per:j * per + 1], (CHUNK, lanes))
            for t in range(1, per):
                nxt = jnp.broadcast_to(q[:, j * per + t:j * per + t + 1], (CHUNK, lanes))
                blk = jnp.where(ci >= t * hd, nxt, blk)
            cols.append(blk)
        return jnp.concatenate(cols, axis=1)

    def chunk(k, carry):
        rows = pl.ds(pl.multiple_of(k * CHUNK, CHUNK), CHUNK)
        xs = xs_ref[rows, :].astype(F32)
        bm = b_ref[rows, :]
        cm = c_ref[rows, :]

        dt = _softplus(dt_ref[rows, :] + dtb_ref[...])
        a_dt = dt * neg_a
        acum = sum(jnp.dot(tril, piece, preferred_element_type=F32) for piece in _split3(a_dt))
        acum_t = acum.T
        acum_f = expand(acum)
        last = acum_f[CHUNK - 1:CHUNK, :]
        xdt = xs * expand(dt)

        cb = lax.dot_general(cm, bm, (((1,), (1,)), ((), ())), preferred_element_type=F32)
        y_cols = []
        for j in range(width // lanes):
            x_col = xdt[:, j * lanes:(j + 1) * lanes]
            y_col = jnp.zeros((CHUNK, lanes), F32)
            for t in range(per):
                h = j * per + t
                seg = acum[:, h:h + 1] - acum_t[h:h + 1, :]
                lmat = jnp.exp(jnp.where(causal, seg, -jnp.inf))
                m_h = (cb * lmat).astype(BF16)
                mask = (ci >= t * hd) & (ci < (t + 1) * hd)
                x_h = jnp.where(mask, x_col, 0.0).astype(BF16)
                y_col = y_col + jnp.dot(m_h, x_h, preferred_element_type=F32)
            y_cols.append(y_col)
        y_diag = jnp.concatenate(y_cols, axis=1)

        state = state_ref[...]
        y_off = jnp.dot(cm, state.astype(BF16), preferred_element_type=F32) * jnp.exp(acum_f)
        x_dec = (xdt * jnp.exp(last - acum_f)).astype(BF16)
        s_new = lax.dot_general(bm, x_dec, (((0,), (0,)), ((), ())), preferred_element_type=F32)
        state_ref[...] = state * jnp.exp(last) + s_new

        yg = (y_diag + y_off + xs * dsk_ref[...]) * sz_ref[rows, :].astype(F32)
        yn = yg * lax.rsqrt(jnp.mean(yg * yg, axis=-1, keepdims=True) + EPS) * ng_ref[...]
        y_ref[rows, :] = yn.astype(BF16)
        return carry

    lax.fori_loop(0, y_ref.shape[0] // CHUNK, chunk, 0, unroll=2)


def _ssd(p, dt_raw, dt_bias, a_log, d_skip, norm_g, groups, off_z, off_xs):
    t_len = p.shape[0]
    heads = dt_bias.shape[0]
    width = norm_g.shape[0]
    hd = width // heads
    hpg = heads // groups
    gw = hpg * hd
    n_st = SSD_STATE
    rows = _largest_tile(t_len, SSD_ROWS, CHUNK)
    assert V7X_LANES % hd == 0 and gw % V7X_LANES == 0 and hpg <= V7X_LANES
    assert width // groups == gw
    off_b = off_xs + width
    off_c = off_b + groups * n_st
    assert off_z % gw == 0 and off_xs % gw == 0 and off_b % n_st == 0

    def per_group_lanes(v):
        v = v.reshape(groups, 1, hpg)
        return jnp.pad(v, ((0, 0), (0, 0), (0, V7X_LANES - hpg)))

    dt_g = dt_raw[:, :heads].reshape(t_len, groups, hpg).transpose(1, 0, 2)
    dt_g = jnp.pad(dt_g, ((0, 0), (0, 0), (0, V7X_LANES - hpg)))

    grp = lambda g, c: (g, 0, 0)
    return pl.pallas_call(
        functools.partial(_ssd_kernel, hpg=hpg, hd=hd),
        grid=(groups, t_len // rows),
        in_specs=[
            pl.BlockSpec((rows, gw), lambda g, c: (c, off_z // gw + g)),
            pl.BlockSpec((rows, gw), lambda g, c: (c, off_xs // gw + g)),
            pl.BlockSpec((rows, n_st), lambda g, c: (c, off_b // n_st + g)),
            pl.BlockSpec((rows, n_st), lambda g, c: (c, off_c // n_st + g)),
            pl.BlockSpec((None, rows, V7X_LANES), lambda g, c: (g, c, 0)),
            pl.BlockSpec((None, 1, V7X_LANES), grp),
            pl.BlockSpec((None, 1, V7X_LANES), grp),
            pl.BlockSpec((1, gw), lambda g, c: (0, g)),
            pl.BlockSpec((1, gw), lambda g, c: (0, g)),
        ],
        out_specs=pl.BlockSpec((rows, gw), lambda g, c: (c, g)),
        out_shape=jax.ShapeDtypeStruct((t_len, width), BF16),
        scratch_shapes=[pltpu.VMEM((n_st, gw), F32)],
        compiler_params=_params(("arbitrary", "arbitrary")),
        name="ssd",
    )(p, p, p, p, dt_g, per_group_lanes(dt_bias), per_group_lanes(a_log),
      jnp.repeat(d_skip, hd).reshape(1, width), norm_g.reshape(1, width))


def _row_sums(x):
    ones = jnp.ones((x.shape[1], V7X_LANES), BF16)
    hi = x.astype(BF16)
    lo = (x - hi.astype(F32)).astype(BF16)
    return (jnp.dot(hi, ones, preferred_element_type=F32) + jnp.dot(lo, ones, preferred_element_type=F32))


def _gmlp_kernel(u_ref, v_ref, vg_ref, ws_ref, bs_ref, og_ref, y_ref, yscr_ref, *, heads, hd):
    lanes = V7X_LANES
    reps = hd // lanes
    ri = lax.broadcasted_iota(I32, (CHUNK, CHUNK), 0)
    ci = lax.broadcasted_iota(I32, (CHUNK, CHUNK), 1)
    causal = ri >= ci
    ssq = jnp.zeros((CHUNK, lanes), F32)
    invs = []
    for h in range(heads):
        v_h = v_ref[:, h * hd:(h + 1) * hd].astype(F32)
        invs.append(lax.rsqrt(_row_sums(v_h * v_h) / hd + EPS))
    for h in range(heads):
        cols = slice(h * hd, (h + 1) * hd)
        u_h = u_ref[:, cols].astype(F32)
        v_h = v_ref[:, cols].astype(F32)
        v_n = v_h * jnp.tile(invs[h], (1, reps)) * vg_ref[:, cols]
        w_h = jnp.where(causal, ws_ref[h], 0.0).astype(BF16)
        sv = jnp.dot(w_h, v_n.astype(BF16), preferred_element_type=F32) + jnp.tile(bs_ref[h], (1, reps))
        y_h = u_h * sv
        yscr_ref[:, cols] = y_h
        y_sq = y_h * y_h
        for r in range(reps):
            ssq = ssq + y_sq[:, r * lanes:(r + 1) * lanes]
    scale = lax.rsqrt(_row_sums(ssq) / (heads * hd) + EPS)
    y_ref[...] = (yscr_ref[...] * jnp.tile(scale, (1, heads * reps)) * og_ref[...]).astype(BF16)


def _gmlp(p, v_norm_g, w_s, b_s, out_norm_g, off_u):
    t_len = p.shape[0]
    heads = w_s.shape[0]
    width = out_norm_g.shape[0]
    hd = width // heads
    assert w_s.shape[1] == CHUNK and off_u % width == 0 and hd % V7X_LANES == 0
    b_lanes = jnp.broadcast_to(b_s[:, :, None], (heads, CHUNK, V7X_LANES))
    return pl.pallas_call(
        functools.partial(_gmlp_kernel, heads=heads, hd=hd),
        grid=(t_len // CHUNK,),
        in_specs=[
            pl.BlockSpec((CHUNK, width), lambda c: (c, off_u // width)),
            pl.BlockSpec((CHUNK, width), lambda c: (c, off_u // width + 1)),
            pl.BlockSpec((1, width), lambda c: (0, 0)),
            pl.BlockSpec((heads, CHUNK, CHUNK), lambda c: (0, 0, 0)),
            pl.BlockSpec((heads, CHUNK, V7X_LANES), lambda c: (0, 0, 0)),
            pl.BlockSpec((1, width), lambda c: (0, 0)),
        ],
        out_specs=pl.BlockSpec((CHUNK, width), lambda c: (c, 0)),
        out_shape=jax.ShapeDtypeStruct((t_len, width), BF16),
        scratch_shapes=[pltpu.VMEM((CHUNK, width), F32)],
        compiler_params=_params(("arbitrary",)),
        name="gmlp",
    )(p, p, v_norm_g.reshape(1, width), w_s, b_lanes, out_norm_g.reshape(1, width))


def _outproj_kernel(ys_ref, yg_ref, wa_ref, wb_ref, x_ref, o_ref):
    acc = jnp.dot(ys_ref[...], wa_ref[...], preferred_element_type=F32)
    acc = acc + jnp.dot(yg_ref[...], wb_ref[...], preferred_element_type=F32)
    o_ref[...] = x_ref[...] + acc


def _outproj(y_ssd, y_gmlp, w_out, x):
    t_len, d = x.shape
    ka, kb = y_ssd.shape[1], y_gmlp.shape[1]
    assert ka == kb and w_out.shape[0] == ka + kb
    tm = _largest_tile(t_len, 512, V7X_SUBLANES)
    tn = _largest_tile(d, 512)
    return pl.pallas_call(
        _outproj_kernel,
        grid=(t_len // tm, d // tn),
        in_specs=[
            pl.BlockSpec((tm, ka), lambda i, j: (i, 0)),
            pl.BlockSpec((tm, kb), lambda i, j: (i, 0)),
            pl.BlockSpec((ka, tn), lambda i, j: (0, j)),
            pl.BlockSpec((kb, tn), lambda i, j: (1, j)),
            pl.BlockSpec((tm, tn), lambda i, j: (i, j)),
        ],
        out_specs=pl.BlockSpec((tm, tn), lambda i, j: (i, j)),
        out_shape=jax.ShapeDtypeStruct((t_len, d), F32),
        compiler_params=_params(("arbitrary", "arbitrary")),
        name="outproj",
    )(y_ssd, y_gmlp, w_out, w_out, x)


def _router_kernel(x_ref, g_ref, wr_ref, br_ref, xp_ref, idx_ref, gate_ref, *, n_exp):
    x = x_ref[...]
    tm, d = x.shape
    half = d // 2
    ms = jnp.mean(x * x, axis=-1, keepdims=True)
    xn = x * lax.rsqrt(ms + EPS) * g_ref[...]
    xp_ref[...] = _pack_pair(xn[:, :half], xn[:, half:])

    logits = jnp.dot(xn.astype(BF16), wr_ref[...], preferred_element_type=F32) + br_ref[...]
    lane = lax.broadcasted_iota(I32, (tm, V7X_LANES), 1)
    vals = jnp.where(lane < n_exp, logits, -jnp.inf)
    idx_out = jnp.zeros((tm, V7X_LANES), I32)
    val_out = jnp.full((tm, V7X_LANES), -jnp.inf, F32)
    for k in range(TOP_K):
        m = jnp.max(vals, axis=-1, keepdims=True)
        sel = jnp.min(jnp.where(vals == m, lane, V7X_LANES), axis=-1, keepdims=True)
        idx_out = jnp.where(lane == k, sel, idx_out)
        val_out = jnp.where(lane == k, m, val_out)
        vals = jnp.where(lane == sel, -jnp.inf, vals)
    e = jnp.exp(val_out - jnp.max(val_out, axis=-1, keepdims=True))
    idx_ref[...] = idx_out
    gate_ref[...] = e / jnp.sum(e, axis=-1, keepdims=True)


def _router(x1, g, w_router, b_router):
    t_len, d = x1.shape
    n_exp = w_router.shape[1]
    assert n_exp <= V7X_LANES and TOP_K <= n_exp
    tm = _largest_tile(t_len, 256, V7X_SUBLANES)
    wr = jnp.pad(w_router, ((0, 0), (0, V7X_LANES - n_exp))).astype(BF16)
    br = jnp.pad(b_router, (0, V7X_LANES - n_exp)).reshape(1, V7X_LANES)
    return pl.pallas_call(
        functools.partial(_router_kernel, n_exp=n_exp),
        grid=(t_len // tm,),
        in_specs=[
            pl.BlockSpec((tm, d), lambda i: (i, 0)),
            pl.BlockSpec((1, d), lambda i: (0, 0)),
            pl.BlockSpec((d, V7X_LANES), lambda i: (0, 0)),
            pl.BlockSpec((1, V7X_LANES), lambda i: (0, 0)),
        ],
        out_specs=[
            pl.BlockSpec((tm, d // 2), lambda i: (i, 0)),
            pl.BlockSpec((tm, V7X_LANES), lambda i: (i, 0)),
            pl.BlockSpec((tm, V7X_LANES), lambda i: (i, 0)),
        ],
        out_shape=[
            jax.ShapeDtypeStruct((t_len, d // 2), U32),
            jax.ShapeDtypeStruct((t_len, V7X_LANES), I32),
            jax.ShapeDtypeStruct((t_len, V7X_LANES), F32),
        ],
        compiler_params=_params(("arbitrary",)),
        name="router",
    )(x1, g.reshape(1, d), wr, br)


def _route_plan(top_idx, n_exp):
    t_len, top_k = top_idx.shape
    n_assign = t_len * top_k
    e = top_idx.reshape(n_assign)
    onehot = (e[:, None] == jnp.arange(n_exp, dtype=I32)[None, :]).astype(I32)
    csum = jnp.cumsum(onehot, axis=0)
    rank = jnp.sum(onehot * csum, axis=1) - 1
    counts = csum[-1]
    nb = (counts + MOE_BLK - 1) // MOE_BLK
    sb_end = jnp.cumsum(nb)
    sb_start = sb_end - nb
    dest = sb_start[e] * MOE_BLK + rank
    n_sb = n_assign // MOE_BLK + n_exp
    n_rows = n_sb * MOE_BLK
    row_tok = jnp.zeros((n_rows,), I32).at[dest].set(jnp.arange(n_assign, dtype=I32) // top_k)

    nv = (nb + MOE_NSB - 1) // MOE_NSB
    v_end = jnp.cumsum(nv)
    v_start = v_end - nv
    n_pass = -(-n_sb // MOE_NSB) + n_exp
    vid = jnp.arange(n_pass, dtype=I32)
    ve = jnp.minimum(jnp.searchsorted(v_end, vid, side='right'), n_exp - 1).astype(I32)
    q = vid - v_start[ve]
    valid = vid < v_end[-1]
    vs = jnp.where(valid, sb_start[ve] + q * MOE_NSB, 0).astype(I32)
    vn = jnp.where(valid, jnp.clip(nb[ve] - q * MOE_NSB, 0, MOE_NSB), 0).astype(I32)
    last_e = ve[jnp.maximum(v_end[-1] - 1, 0)]
    ve = jnp.where(valid, ve, last_e).astype(I32)
    used = sb_end[-1:].astype(I32)
    return dict(row_tok=row_tok, dest=dest.astype(I32), ve=ve, vs=vs, vn=vn, used=used, n_rows=n_rows,
                n_pass=n_pass)


def _for_each_block(n, compute, emit):
    def quad(q, carry):
        b = 4 * q
        first = compute(b, 2)
        second = compute(b + 2, 2)
        emit(b, 2, first)
        emit(b + 2, 2, second)
        return carry

    quads = lax.shift_right_logical(n, 2)
    lax.fori_loop(0, quads, quad, 0)

    @pl.when((n & 2) != 0)
    def _():
        emit(4 * quads, 2, compute(4 * quads, 2))

    @pl.when((n & 1) != 0)
    def _():
        emit(n - 1, 1, compute(n - 1, 1))


def _zero_tail(obuf, n_used, n_sb, n_j, out_copy):
    tile = obuf.shape[2]
    obuf[0, 0:MOE_BLK, :] = jnp.zeros((MOE_BLK, tile), obuf.dtype)

    def start(sb, carry):
        for jj in range(n_j):
            out_copy(0, 0, pl.multiple_of(sb * MOE_BLK, MOE_BLK), jj * tile).start()
        return carry

    def wait(sb, carry):
        for jj in range(n_j):
            out_copy(0, 0, 0, 0).wait()
        return carry

    lax.fori_loop(n_used, n_sb, start, 0)
    lax.fori_loop(n_used, n_sb, wait, 0)


def _moe_up_kernel(ve_ref, vs_ref, vn_ref, used_ref, tok_ref, xp_hbm, wg_ref, wl_ref, bg_ref, bl_ref, act_hbm,
                   xbuf, wc_ref, obuf, xsem, osem, ocnt, *, tf, n_j):
    v = pl.program_id(0)
    j = pl.program_id(1)
    n_pass = pl.num_programs(0)
    half = xbuf.shape[2]
    slot = v % 2
    n_sub = vn_ref[v]

    def row_copy(tok, sl, r0, u):
        window = xbuf.at[sl, pl.ds(r0, ISSUE_UNROLL), :]
        return pltpu.make_async_copy(xp_hbm.at[pl.ds(tok, 1), :], window.at[pl.ds(u, 1), :], xsem.at[sl])

    def gather_start(vv, sl):
        base = vs_ref[vv] * MOE_BLK

        def body(q, carry):
            r0 = pl.multiple_of(q * ISSUE_UNROLL, ISSUE_UNROLL)
            for u in range(ISSUE_UNROLL):
                row_copy(tok_ref[base + r0 + u], sl, r0, u).start()
            return carry

        lax.fori_loop(0, vn_ref[vv] * (MOE_BLK // ISSUE_UNROLL), body, 0)

    def gather_wait(vv, sl):
        def body(i, carry):
            r0 = pl.multiple_of(i * MOE_BLK, MOE_BLK)
            pltpu.make_async_copy(xp_hbm.at[pl.ds(0, MOE_BLK), :], xbuf.at[sl, pl.ds(r0, MOE_BLK), :],
                                  xsem.at[sl]).wait()
            return carry

        lax.fori_loop(0, vn_ref[vv], body, 0)

    def out_copy(p, r0, row, col):
        return pltpu.make_async_copy(obuf.at[p, pl.ds(r0, MOE_BLK), :],
                                     act_hbm.at[pl.ds(row, MOE_BLK), pl.ds(col, tf)], osem.at[p])

    def drain(p):
        def body(i, carry):
            out_copy(p, 0, 0, 0).wait()
            return carry

        lax.fori_loop(0, ocnt[p], body, 0)
        ocnt[p] = 0

    @pl.when(j == 0)
    def _():
        @pl.when(v == 0)
        def _():
            ocnt[0] = 0
            ocnt[1] = 0
            gather_start(0, 0)

        gather_wait(v, slot)

    step = v * n_j + j
    p = step % 2
    drain(p)

    @pl.when(n_sub > 0)
    def _():
        nxt = jnp.minimum(v + 1, n_pass - 1)
        n_next = jnp.where(v + 1 < n_pass, vn_ref[nxt], 0)
        groups = n_next * (MOE_BLK // (ISSUE_UNROLL * n_j))
        first = j * groups
        base = vs_ref[nxt] * MOE_BLK
        n_cast = wc_ref.shape[0] // CAST_ROWS
        both = jnp.minimum(groups, n_cast)

        def request(q):
            r0 = pl.multiple_of((first + q) * ISSUE_UNROLL, ISSUE_UNROLL)
            for u in range(ISSUE_UNROLL):
                row_copy(tok_ref[base + r0 + u], 1 - slot, r0, u).start()

        def cast(q, with_request):
            rows = pl.ds(pl.multiple_of(q * CAST_ROWS, CAST_ROWS), CAST_ROWS)
            g = wg_ref[rows, :]
            l = wl_ref[rows, :]
            if with_request:
                request(q)
            wc_ref[rows, 0:tf] = g.astype(BF16)
            wc_ref[rows, tf:2 * tf] = l.astype(BF16)

        def cast_and_request(q, carry):
            cast(q, True)
            return carry

        def cast_only(q, carry):
            cast(q, False)
            return carry

        def request_only(q, carry):
            request(q)
            return carry

        lax.fori_loop(0, both, cast_and_request, 0)
        lax.fori_loop(both, n_cast, cast_only, 0)
        lax.fori_loop(both, groups, request_only, 0)

        def compute(b, nb):
            r0 = pl.multiple_of(b * MOE_BLK, MOE_BLK)
            lo, hi = _unpack_pair(xbuf[slot, pl.ds(r0, nb * MOE_BLK), :])
            h = jnp.dot(lo.astype(BF16), wc_ref[0:half, :], preferred_element_type=F32)
            h = h + jnp.dot(hi.astype(BF16), wc_ref[half:2 * half, :], preferred_element_type=F32)
            glu = jnp.minimum(h[:, 0:tf] + bg_ref[...], SWIGLU_LIMIT)
            lin = jnp.clip(h[:, tf:2 * tf] + bl_ref[...], -SWIGLU_LIMIT, SWIGLU_LIMIT)
            return (glu * jax.nn.sigmoid(SWIGLU_ALPHA * glu) * (lin + 1.0)).astype(BF16)

        def emit(b, nb, act):
            r0 = pl.multiple_of(b * MOE_BLK, MOE_BLK)
            obuf[p, pl.ds(r0, nb * MOE_BLK), :] = act
            for t in range(nb):
                row = pl.multiple_of((vs_ref[v] + b + t) * MOE_BLK, MOE_BLK)
                out_copy(p, r0 + t * MOE_BLK, row, pl.multiple_of(j * tf, tf)).start()

        _for_each_block(n_sub, compute, emit)
        ocnt[p] = n_sub

    @pl.when(step == n_pass * n_j - 1)
    def _():
        drain(0)
        drain(1)
        _zero_tail(obuf, used_ref[0], act_hbm.shape[0] // MOE_BLK, n_j, out_copy)


def _moe_up(xp, plan, w_gate_up, b_gate_up):
    n_exp, d, two_f = w_gate_up.shape
    f = two_f // 2
    half = xp.shape[1]
    assert half * 2 == d
    tf = _largest_tile(f, 256)
    n_j = f // tf
    rows_pass = MOE_NSB * MOE_BLK
    assert MOE_BLK % (ISSUE_UNROLL * n_j) == 0 and d % CAST_ROWS == 0

    def w_map(off):
        def index_map(v, j, ve, vs, vn, used, tok):
            jj = jnp.where(vn[v] > 0, j, n_j - 1)
            return (ve[v], 0, off + jj)
        return index_map

    grid_spec = pltpu.PrefetchScalarGridSpec(
        num_scalar_prefetch=5,
        grid=(plan["n_pass"], n_j),
        in_specs=[
            pl.BlockSpec(memory_space=pl.ANY),
            pl.BlockSpec((None, d, tf), w_map(0)),
            pl.BlockSpec((None, d, tf), w_map(n_j)),
            pl.BlockSpec((None, 1, tf), w_map(0)),
            pl.BlockSpec((None, 1, tf), w_map(n_j)),
        ],
        out_specs=pl.BlockSpec(memory_space=pl.ANY),
        scratch_shapes=[
            pltpu.VMEM((2, rows_pass, half), U32),
            pltpu.VMEM((d, 2 * tf), BF16),
            pltpu.VMEM((2, rows_pass, tf), BF16),
            pltpu.SemaphoreType.DMA((2,)),
            pltpu.SemaphoreType.DMA((2,)),
            pltpu.SMEM((2,), I32),
        ],
    )
    b3 = b_gate_up.reshape(n_exp, 1, two_f)
    return pl.pallas_call(
        functools.partial(_moe_up_kernel, tf=tf, n_j=n_j),
        grid_spec=grid_spec,
        out_shape=jax.ShapeDtypeStruct((plan["n_rows"], f), BF16),
        compiler_params=_params(("arbitrary", "arbitrary")),
        name="moe_up",
    )(plan["ve"], plan["vs"], plan["vn"], plan["used"], plan["row_tok"], xp, w_gate_up, w_gate_up, b3, b3)


def _moe_down_kernel(ve_ref, vs_ref, vn_ref, used_ref, act_hbm, wa_ref, wb_ref, ba_ref, bb_ref, y_hbm,
                     abuf, wc_ref, obuf, asem, osem, ocnt, *, tn):
    v = pl.program_id(0)
    j = pl.program_id(1)
    n_pass = pl.num_programs(0)
    n_j = pl.num_programs(1)
    slot = v % 2
    n_sub = vn_ref[v]

    def in_copy(row, sl, r0):
        return pltpu.make_async_copy(act_hbm.at[pl.ds(row, MOE_BLK), :],
                                     abuf.at[sl, pl.ds(r0, MOE_BLK), :], asem.at[sl])

    def load_start(vv, sl):
        def body(i, carry):
            r0 = pl.multiple_of(i * MOE_BLK, MOE_BLK)
            in_copy(pl.multiple_of((vs_ref[vv] + i) * MOE_BLK, MOE_BLK), sl, r0).start()
            return carry

        lax.fori_loop(0, vn_ref[vv], body, 0)

    def load_wait(vv, sl):
        def body(i, carry):
            in_copy(0, sl, pl.multiple_of(i * MOE_BLK, MOE_BLK)).wait()
            return carry

        lax.fori_loop(0, vn_ref[vv], body, 0)

    def out_copy(p, r0, row, col):
        return pltpu.make_async_copy(obuf.at[p, pl.ds(r0, MOE_BLK), :],
                                     y_hbm.at[pl.ds(row, MOE_BLK), pl.ds(col, tn)], osem.at[p])

    def drain(p):
        def body(i, carry):
            out_copy(p, 0, 0, 0).wait()
            return carry

        lax.fori_loop(0, ocnt[p], body, 0)
        ocnt[p] = 0

    @pl.when(j == 0)
    def _():
        @pl.when(v == 0)
        def _():
            ocnt[0] = 0
            ocnt[1] = 0
            load_start(0, 0)

        @pl.when(v + 1 < n_pass)
        def _():
            load_start(v + 1, 1 - slot)

        load_wait(v, slot)

    step = v * n_j + j
    p = step % 2
    drain(p)

    @pl.when(n_sub > 0)
    def _():
        wc_ref[:, 0:tn] = wa_ref[...].astype(BF16)
        wc_ref[:, tn:2 * tn] = wb_ref[...].astype(BF16)

        def compute(b, nb):
            r0 = pl.multiple_of(b * MOE_BLK, MOE_BLK)
            y = jnp.dot(abuf[slot, pl.ds(r0, nb * MOE_BLK), :], wc_ref[...], preferred_element_type=F32)
            return _pack_pair(y[:, 0:tn] + ba_ref[...], y[:, tn:2 * tn] + bb_ref[...])

        def emit(b, nb, packed):
            r0 = pl.multiple_of(b * MOE_BLK, MOE_BLK)
            obuf[p, pl.ds(r0, nb * MOE_BLK), :] = packed
            for t in range(nb):
                row = pl.multiple_of((vs_ref[v] + b + t) * MOE_BLK, MOE_BLK)
                out_copy(p, r0 + t * MOE_BLK, row, pl.multiple_of(j * tn, tn)).start()

        _for_each_block(n_sub, compute, emit)
        ocnt[p] = n_sub

    @pl.when(step == n_pass * n_j - 1)
    def _():
        drain(0)
        drain(1)
        _zero_tail(obuf, used_ref[0], y_hbm.shape[0] // MOE_BLK, n_j, out_copy)


def _moe_down(act, plan, w_down, b_down):
    n_exp, f, d = w_down.shape
    half = d // 2
    tn = _largest_tile(half, 256)
    n_j = half // tn
    rows_pass = MOE_NSB * MOE_BLK

    def w_map(off):
        def index_map(v, j, ve, vs, vn, used):
            jj = jnp.where(vn[v] > 0, j, n_j - 1)
            return (ve[v], 0, off + jj)
        return index_map

    grid_spec = pltpu.PrefetchScalarGridSpec(
        num_scalar_prefetch=4,
        grid=(plan["n_pass"], n_j),
        in_specs=[
            pl.BlockSpec(memory_space=pl.ANY),
            pl.BlockSpec((None, f, tn), w_map(0)),
            pl.BlockSpec((None, f, tn), w_map(n_j)),
            pl.BlockSpec((None, 1, tn), w_map(0)),
            pl.BlockSpec((None, 1, tn), w_map(n_j)),
        ],
        out_specs=pl.BlockSpec(memory_space=pl.ANY),
        scratch_shapes=[
            pltpu.VMEM((2, rows_pass, f), BF16),
            pltpu.VMEM((f, 2 * tn), BF16),
            pltpu.VMEM((2, rows_pass, tn), U32),
            pltpu.SemaphoreType.DMA((2,)),
            pltpu.SemaphoreType.DMA((2,)),
            pltpu.SMEM((2,), I32),
        ],
    )
    b3 = b_down.reshape(n_exp, 1, d)
    return pl.pallas_call(
        functools.partial(_moe_down_kernel, tn=tn),
        grid_spec=grid_spec,
        out_shape=jax.ShapeDtypeStruct((plan["n_rows"], half), U32),
        compiler_params=_params(("arbitrary", "arbitrary")),
        name="moe_down",
    )(plan["ve"], plan["vs"], plan["vn"], plan["used"], act, w_down, w_down, b3, b3)


def _combine_kernel(dest_ref, y_hbm, gate_ref, x_ref, g_ref, o_ref, ybuf, sem):
    i = pl.program_id(0)
    n_i = pl.num_programs(0)
    tm, d = x_ref.shape
    half = d // 2
    slot = i % 2

    def row_copy(row, sl, k, t0, u):
        window = ybuf.at[sl, k, pl.ds(t0, ISSUE_UNROLL), :]
        return pltpu.make_async_copy(y_hbm.at[pl.ds(row, 1), :], window.at[pl.ds(u, 1), :], sem.at[sl])

    def request(ii, sl, q):
        t0 = pl.multiple_of(q * ISSUE_UNROLL, ISSUE_UNROLL)
        for u in range(ISSUE_UNROLL):
            for k in range(TOP_K):
                row_copy(dest_ref[(ii * tm + t0 + u) * TOP_K + k], sl, k, t0, u).start()

    def gather_wait(sl):
        for k in range(TOP_K):
            pltpu.make_async_copy(y_hbm.at[pl.ds(0, tm), :], ybuf.at[sl, k], sem.at[sl]).wait()

    def combine(q, prefetch):
        rows = pl.ds(pl.multiple_of(q * ISSUE_UNROLL, ISSUE_UNROLL), ISSUE_UNROLL)
        acc_lo = x_ref[rows, 0:half]
        acc_hi = x_ref[rows, half:d]
        for k in range(TOP_K):
            lo, hi = _unpack_pair(ybuf[slot, k, rows, :])
            gk = gate_ref[rows, k:k + 1]
            acc_lo = acc_lo + gk * lo
            acc_hi = acc_hi + gk * hi
        if prefetch:
            request(i + 1, 1 - slot, q)
        ms = (jnp.sum(acc_lo * acc_lo, axis=-1, keepdims=True)
              + jnp.sum(acc_hi * acc_hi, axis=-1, keepdims=True)) / d
        r = lax.rsqrt(ms + EPS)
        o_ref[rows, 0:half] = acc_lo * r * g_ref[:, 0:half]
        o_ref[rows, half:d] = acc_hi * r * g_ref[:, half:d]

    n_groups = tm // ISSUE_UNROLL

    @pl.when(i == 0)
    def _():
        lax.fori_loop(0, n_groups, lambda q, c: (request(0, 0, q), c)[1], 0)

    gather_wait(slot)

    @pl.when(i + 1 < n_i)
    def _():
        lax.fori_loop(0, n_groups, lambda q, c: (combine(q, True), c)[1], 0, unroll=2)

    @pl.when(i + 1 == n_i)
    def _():
        lax.fori_loop(0, n_groups, lambda q, c: (combine(q, False), c)[1], 0, unroll=2)


def _combine(y_rows, dest, gate, x1, final_g):
    t_len, d = x1.shape
    half = d // 2
    tm = _largest_tile(t_len, 128, V7X_SUBLANES)
    grid_spec = pltpu.PrefetchScalarGridSpec(
        num_scalar_prefetch=1,
        grid=(t_len // tm,),
        in_specs=[
            pl.BlockSpec(memory_space=pl.ANY),
            pl.BlockSpec((tm, V7X_LANES), lambda i, dest: (i, 0)),
            pl.BlockSpec((tm, d), lambda i, dest: (i, 0)),
            pl.BlockSpec((1, d), lambda i, dest: (0, 0)),
        ],
        out_specs=pl.BlockSpec((tm, d), lambda i, dest: (i, 0)),
        scratch_shapes=[
            pltpu.VMEM((2, TOP_K, tm, half), U32),
            pltpu.SemaphoreType.DMA((2,)),
        ],
    )
    return pl.pallas_call(
        _combine_kernel,
        grid_spec=grid_spec,
        out_shape=jax.ShapeDtypeStruct((t_len, d), F32),
        compiler_params=_params(("arbitrary",)),
        name="combine",
    )(dest, y_rows, gate, x1, final_g.reshape(1, d))


def kernel(x, mix_norm_g, w_in, ssd_conv_w, ssd_conv_b, ssd_dt_bias, ssd_a_log, ssd_d, ssd_norm_g,
           gmlp_v_norm_g, gmlp_w_s, gmlp_b_s, gmlp_out_norm_g, w_out, ffn_norm_g, w_router, b_router,
           w_gate_up, b_gate_up, w_down, b_down, final_norm_g):
    bsz, t_len, d = x.shape
    assert bsz == 1
    depth = w_in.shape[0]
    ssd_w = ssd_norm_g.shape[-1]
    heads = ssd_dt_bias.shape[-1]
    gmlp_w = gmlp_out_norm_g.shape[-1]
    conv_dim = ssd_conv_b.shape[-1]
    assert heads <= V7X_LANES
    c_dt = ssd_w + conv_dim
    c_uv = c_dt + heads
    off_u, off_z, off_xs = 0, 2 * gmlp_w, 2 * gmlp_w + ssd_w

    h = x.reshape(t_len, d)
    out = None
    for layer in range(depth):
        w = w_in[layer]
        w_t = jnp.swapaxes(w, 0, 1).astype(BF16)
        w_dt = jnp.pad(w_t[c_dt:c_uv], ((0, V7X_LANES - heads), (0, 0)))
        p, dt_raw = _inproj(h, mix_norm_g[layer], w_t, w_dt, ssd_conv_w[layer], ssd_conv_b[layer],
                            2 * gmlp_w, ssd_w, c_uv)
        groups = (conv_dim - ssd_w) // (2 * SSD_STATE)
        y_ssd = _ssd(p, dt_raw, ssd_dt_bias[layer], ssd_a_log[layer], ssd_d[layer], ssd_norm_g[layer],
                     groups, off_z, off_xs)
        y_gmlp = _gmlp(p, gmlp_v_norm_g[layer], gmlp_w_s[layer], gmlp_b_s[layer],
                       gmlp_out_norm_g[layer], off_u)
        x1 = _outproj(y_ssd, y_gmlp, w_out[layer].astype(BF16), h)

        xp, idx_pad, gate_pad = _router(x1, ffn_norm_g[layer], w_router[layer], b_router[layer])
        plan = _route_plan(idx_pad[:, :TOP_K], w_router.shape[-1])
        act = _moe_up(xp, plan, w_gate_up[layer], b_gate_up[layer])
        y_rows = _moe_down(act, plan, w_down[layer], b_down[layer])
        last = layer == depth - 1
        assert last, "multi-layer stacking needs an un-normalised combine"
        out = _combine(y_rows, plan["dest"], gate_pad, x1, final_norm_g)
    return out.reshape(bsz, t_len, d)
```

```python
import functools
import math

import jax
import jax.numpy as jnp
from jax import lax
from jax.experimental import pallas as pl
from jax.experimental.pallas import tpu as pltpu

F32, BF16, U32, I32 = jnp.float32, jnp.bfloat16, jnp.uint32, jnp.int32

EPS = 1e-5
SSD_STATE = 128
SSD_CONV = 4
CHUNK = 128
TOP_K = 4
SSD_ROWS = 512
SWIGLU_ALPHA = 1.702
SWIGLU_LIMIT = 7.0

V7X_LANES = 128
V7X_SUBLANES = 8
V7X_VMEM_BYTES = 64 * 1024 * 1024
VMEM_LIMIT = V7X_VMEM_BYTES - 8 * 1024 * 1024

MOE_BLK = 128
MOE_NSB = 12
ISSUE_UNROLL = 8
CAST_ROWS = 128
W_SPLIT = 4
HI_MASK = 0xFFFF0000


_gcd = math.gcd


def _largest_tile(total, cap, quantum=V7X_LANES):
    best = None
    t = quantum
    while t <= min(total, cap):
        if total % t == 0:
            best = t
        t += quantum
    assert best is not None, (total, cap, quantum)
    return best


def _params(semantics):
    return pltpu.CompilerParams(dimension_semantics=semantics, vmem_limit_bytes=VMEM_LIMIT)


def _silu(x):
    return x * jax.nn.sigmoid(x)


def _gelu(x):
    return 0.5 * x * (1.0 + lax.erf(x * (2.0 ** -0.5)))


def _softplus(x):
    return jnp.maximum(x, 0.0) + jnp.log1p(jnp.exp(-jnp.abs(x)))


def _pack_pair(lo, hi):
    lo_bits = lax.bitcast_convert_type(lo.astype(BF16).astype(F32), U32)
    hi_bits = lax.bitcast_convert_type(hi.astype(BF16).astype(F32), U32)
    return (lo_bits >> 16) | (hi_bits & jnp.uint32(HI_MASK))


def _unpack_pair(u):
    lo = lax.bitcast_convert_type(u << 16, F32)
    hi = lax.bitcast_convert_type(u & jnp.uint32(HI_MASK), F32)
    return lo, hi


NT_DIMS = (((1,), (1,)), ((), ()))


def _inproj_kernel(x_hbm, g_ref, w_ref, wdt_ref, cw_ref, cb_ref, p_ref, dt_ref, xbuf, xn_ref, halo_ref, sem,
                   *, n_gelu, n_silu):
    i = pl.program_id(0)
    j = pl.program_id(1)
    tm = xbuf.shape[0]
    rows = _largest_tile(tm, 128, V7X_SUBLANES)
    tail = halo_ref.shape[1]

    def x_copy(ii):
        return pltpu.make_async_copy(x_hbm.at[pl.ds(pl.multiple_of(ii * tm, tm), tm), :], xbuf, sem.at[0])

    @pl.when(j == 0)
    def _():
        @pl.when(i == 0)
        def _():
            x_copy(0).start()
            halo_ref[...] = jnp.zeros(halo_ref.shape, F32)

        x_copy(i).wait()

        def norm(r, carry):
            sl = pl.ds(pl.multiple_of(r * rows, rows), rows)
            x = xbuf[sl, :]
            ms = jnp.mean(x * x, axis=-1, keepdims=True)
            xn = (x * lax.rsqrt(ms + EPS) * g_ref[...]).astype(BF16)
            xn_ref[sl, :] = xn
            dt_ref[sl, :] = lax.dot_general(xn, wdt_ref[...], NT_DIMS, preferred_element_type=F32)
            return carry

        lax.fori_loop(0, tm // rows, norm, 0)

        @pl.when(i + 1 < pl.num_programs(0))
        def _():
            x_copy(i + 1).start()

    def project():
        return lax.dot_general(xn_ref[...], w_ref[...], NT_DIMS, preferred_element_type=F32)

    @pl.when(j < n_gelu)
    def _():
        p_ref[...] = _gelu(project()).astype(BF16)

    @pl.when((j >= n_gelu) & (j < n_gelu + n_silu))
    def _():
        p_ref[...] = _silu(project()).astype(BF16)

    @pl.when(j >= n_gelu + n_silu)
    def _():
        x = project()
        jc = j - (n_gelu + n_silu)
        ext = jnp.concatenate([halo_ref[jc], x], axis=0)
        acc = cb_ref[...] + cw_ref[SSD_CONV - 1:SSD_CONV, :] * ext
        for s in range(1, SSD_CONV):
            acc = acc + cw_ref[SSD_CONV - 1 - s:SSD_CONV - s, :] * pltpu.roll(ext, s, axis=0)
        p_ref[...] = _silu(acc[tail:, :]).astype(BF16)
        halo_ref[jc] = x[tm - tail:tm, :]


def _inproj(x, g, w_t, w_dt, conv_w, conv_b, n_gelu_cols, n_silu_cols, row_gelu):
    t_len, d = x.shape
    n_conv_cols = conv_b.shape[0]
    n_out = n_gelu_cols + n_silu_cols + n_conv_cols
    assert row_gelu + n_gelu_cols == w_t.shape[0] and row_gelu >= n_silu_cols + n_conv_cols
    tm = _largest_tile(t_len, 1024, V7X_SUBLANES)
    tn = _largest_tile(_gcd(_gcd(n_gelu_cols, n_silu_cols), n_conv_cols), 512)
    n_gelu, n_silu, n_conv = n_gelu_cols // tn, n_silu_cols // tn, n_conv_cols // tn
    tail = 2 * V7X_SUBLANES
    assert row_gelu % tail == 0 and tn % tail == 0
    conv_map = lambda i, j: (0, jnp.maximum(j - (n_gelu + n_silu), 0))
    w_map = lambda i, j: (
        tail * jnp.where(j < n_gelu, row_gelu // tail + j * (tn // tail), (j - n_gelu) * (tn // tail)), 0)
    return pl.pallas_call(
        functools.partial(_inproj_kernel, n_gelu=n_gelu, n_silu=n_silu),
        grid=(t_len // tm, n_out // tn),
        in_specs=[
            pl.BlockSpec(memory_space=pl.ANY),
            pl.BlockSpec((1, d), lambda i, j: (0, 0)),
            pl.BlockSpec((pl.Element(tn), pl.Element(d)), w_map),
            pl.BlockSpec((V7X_LANES, d), lambda i, j: (0, 0)),
            pl.BlockSpec((SSD_CONV, tn), conv_map),
            pl.BlockSpec((1, tn), conv_map),
        ],
        out_specs=[
            pl.BlockSpec((tm, tn), lambda i, j: (i, j)),
            pl.BlockSpec((tm, V7X_LANES), lambda i, j: (i, 0)),
        ],
        out_shape=[
            jax.ShapeDtypeStruct((t_len, n_out), BF16),
            jax.ShapeDtypeStruct((t_len, V7X_LANES), F32),
        ],
        scratch_shapes=[
            pltpu.VMEM((tm, d), F32),
            pltpu.VMEM((tm, d), BF16),
            pltpu.VMEM((n_conv, tail, tn), F32),
            pltpu.SemaphoreType.DMA((1,)),
        ],
        compiler_params=_params(("arbitrary", "arbitrary")),
        name="inproj",
    )(x, g.reshape(1, d), w_t, w_dt, conv_w, conv_b.reshape(1, n_conv_cols))


def _split3(x):
    hi = x.astype(BF16)
    r1 = x - hi.astype(F32)
    mid = r1.astype(BF16)
    lo = (r1 - mid.astype(F32)).astype(BF16)
    return hi, mid, lo


def _ssd_kernel(sz_ref, xs_ref, b_ref, c_ref, dt_ref, dtb_ref, alog_ref, dsk_ref, ng_ref, y_ref, state_ref,
                *, hpg, hd):
    width = hpg * hd
    lanes = V7X_LANES
    per = lanes // hd

    @pl.when(pl.program_id(1) == 0)
    def _():
        state_ref[...] = jnp.zeros(state_ref.shape, F32)

    ri = lax.broadcasted_iota(I32, (CHUNK, CHUNK), 0)
    ci = lax.broadcasted_iota(I32, (CHUNK, CHUNK), 1)
    causal = ri >= ci
    tril = causal.astype(BF16)
    neg_a = -jnp.exp(alog_ref[...])

    def expand(q):
        cols = []
        for j in range(width // lanes):
            blk = jnp.broadcast_to(q[:, j * per:j * per + 1], (CHUNK, lanes))
            for t in range(1, per):
                nxt = jnp.broadcast_to(q[:, j * per + t:j * per + t + 1], (CHUNK, lanes))
                blk = jnp.where(ci >= t * hd, nxt, blk)
            cols.append(blk)
        return jnp.concatenate(cols, axis=1)

    def chunk(k, carry):
        rows = pl.ds(pl.multiple_of(k * CHUNK, CHUNK), CHUNK)
        xs = xs_ref[rows, :].astype(F32)
        bm = b_ref[rows, :]
        cm = c_ref[rows, :]

        dt = _softplus(dt_ref[rows, :] + dtb_ref[...])
        a_dt = dt * neg_a
        acum = sum(jnp.dot(tril, piece, preferred_element_type=F32) for piece in _split3(a_dt))
        acum_t = acum.T
        acum_f = expand(acum)
        last = acum_f[CHUNK - 1:CHUNK, :]
        xdt = xs * expand(dt)

        cb = lax.dot_general(cm, bm, (((1,), (1,)), ((), ())), preferred_element_type=F32)
        y_cols = []
        for j in range(width // lanes):
            x_col = xdt[:, j * lanes:(j + 1) * lanes]
            y_col = jnp.zeros((CHUNK, lanes), F32)
            for t in range(per):
                h = j * per + t
                seg = acum[:, h:h + 1] - acum_t[h:h + 1, :]
                lmat = jnp.exp(jnp.where(causal, seg, -jnp.inf))
                m_h = (cb * lmat).astype(BF16)
                mask = (ci >= t * hd) & (ci < (t + 1) * hd)
                x_h = jnp.where(mask, x_col, 0.0).astype(BF16)
                y_col = y_col + jnp.dot(m_h, x_h, preferred_element_type=F32)
            y_cols.append(y_col)
        y_diag = jnp.concatenate(y_cols, axis=1)

        state = state_ref[...]
        y_off = jnp.dot(cm, state.astype(BF16), preferred_element_type=F32) * jnp.exp(acum_f)
        x_dec = (xdt * jnp.exp(last - acum_f)).astype(BF16)
        s_new = lax.dot_general(bm, x_dec, (((0,), (0,)), ((), ())), preferred_element_type=F32)
        state_ref[...] = state * jnp.exp(last) + s_new

        yg = (y_diag + y_off + xs * dsk_ref[...]) * sz_ref[rows, :].astype(F32)
        yn = yg * lax.rsqrt(jnp.mean(yg * yg, axis=-1, keepdims=True) + EPS) * ng_ref[...]
        y_ref[rows, :] = yn.astype(BF16)
        return carry

    lax.fori_loop(0, y_ref.shape[0] // CHUNK, chunk, 0, unroll=2)


def _ssd(p, dt_raw, dt_bias, a_log, d_skip, norm_g, groups, off_z, off_xs):
    t_len = p.shape[0]
    heads = dt_bias.shape[0]
    width = norm_g.shape[0]
    hd = width // heads
    hpg = heads // groups
    gw = hpg * hd
    n_st = SSD_STATE
    rows = _largest_tile(t_len, SSD_ROWS, CHUNK)
    assert V7X_LANES % hd == 0 and gw % V7X_LANES == 0 and hpg <= V7X_LANES
    assert width // groups == gw
    off_b = off_xs + width
    off_c = off_b + groups * n_st
    assert off_z % gw == 0 and off_xs % gw == 0 and off_b % n_st == 0

    def per_group_lanes(v):
        v = v.reshape(groups, 1, hpg)
        return jnp.pad(v, ((0, 0), (0, 0), (0, V7X_LANES - hpg)))

    dt_g = dt_raw[:, :heads].reshape(t_len, groups, hpg).transpose(1, 0, 2)
    dt_g = jnp.pad(dt_g, ((0, 0), (0, 0), (0, V7X_LANES - hpg)))

    grp = lambda g, c: (g, 0, 0)
    return pl.pallas_call(
        functools.partial(_ssd_kernel, hpg=hpg, hd=hd),
        grid=(groups, t_len // rows),
        in_specs=[
            pl.BlockSpec((rows, gw), lambda g, c: (c, off_z // gw + g)),
            pl.BlockSpec((rows, gw), lambda g, c: (c, off_xs // gw + g)),
            pl.BlockSpec((rows, n_st), lambda g, c: (c, off_b // n_st + g)),
            pl.BlockSpec((rows, n_st), lambda g, c: (c, off_c // n_st + g)),
            pl.BlockSpec((None, rows, V7X_LANES), lambda g, c: (g, c, 0)),
            pl.BlockSpec((None, 1, V7X_LANES), grp),
            pl.BlockSpec((None, 1, V7X_LANES), grp),
            pl.BlockSpec((1, gw), lambda g, c: (0, g)),
            pl.BlockSpec((1, gw), lambda g, c: (0, g)),
        ],
        out_specs=pl.BlockSpec((rows, gw), lambda g, c: (c, g)),
        out_shape=jax.ShapeDtypeStruct((t_len, width), BF16),
        scratch_shapes=[pltpu.VMEM((n_st, gw), F32)],
        compiler_params=_params(("arbitrary", "arbitrary")),
        name="ssd",
    )(p, p, p, p, dt_g, per_group_lanes(dt_bias), per_group_lanes(a_log),
      jnp.repeat(d_skip, hd).reshape(1, width), norm_g.reshape(1, width))


def _row_sums(x):
    ones = jnp.ones((x.shape[1], V7X_LANES), BF16)
    hi = x.astype(BF16)
    lo = (x - hi.astype(F32)).astype(BF16)
    return (jnp.dot(hi, ones, preferred_element_type=F32) + jnp.dot(lo, ones, preferred_element_type=F32))


def _gmlp_kernel(u_ref, v_ref, vg_ref, ws_ref, bs_ref, og_ref, y_ref, yscr_ref, *, heads, hd):
    lanes = V7X_LANES
    reps = hd // lanes
    ri = lax.broadcasted_iota(I32, (CHUNK, CHUNK), 0)
    ci = lax.broadcasted_iota(I32, (CHUNK, CHUNK), 1)
    causal = ri >= ci
    ssq = jnp.zeros((CHUNK, lanes), F32)
    invs = []
    for h in range(heads):
        v_h = v_ref[:, h * hd:(h + 1) * hd].astype(F32)
        invs.append(lax.rsqrt(_row_sums(v_h * v_h) / hd + EPS))
    for h in range(heads):
        cols = slice(h * hd, (h + 1) * hd)
        u_h = u_ref[:, cols].astype(F32)
        v_h = v_ref[:, cols].astype(F32)
        v_n = v_h * jnp.tile(invs[h], (1, reps)) * vg_ref[:, cols]
        w_h = jnp.where(causal, ws_ref[h], 0.0).astype(BF16)
        sv = jnp.dot(w_h, v_n.astype(BF16), preferred_element_type=F32) + jnp.tile(bs_ref[h], (1, reps))
        y_h = u_h * sv
        yscr_ref[:, cols] = y_h
        y_sq = y_h * y_h
        for r in range(reps):
            ssq = ssq + y_sq[:, r * lanes:(r + 1) * lanes]
    scale = lax.rsqrt(_row_sums(ssq) / (heads * hd) + EPS)
    y_ref[...] = (yscr_ref[...] * jnp.tile(scale, (1, heads * reps)) * og_ref[...]).astype(BF16)


def _gmlp(p, v_norm_g, w_s, b_s, out_norm_g, off_u):
    t_len = p.shape[0]
    heads = w_s.shape[0]
    width = out_norm_g.shape[0]
    hd = width // heads
    assert w_s.shape[1] == CHUNK and off_u % width == 0 and hd % V7X_LANES == 0
    b_lanes = jnp.broadcast_to(b_s[:, :, None], (heads, CHUNK, V7X_LANES))
    return pl.pallas_call(
        functools.partial(_gmlp_kernel, heads=heads, hd=hd),
        grid=(t_len // CHUNK,),
        in_specs=[
            pl.BlockSpec((CHUNK, width), lambda c: (c, off_u // width)),
            pl.BlockSpec((CHUNK, width), lambda c: (c, off_u // width + 1)),
            pl.BlockSpec((1, width), lambda c: (0, 0)),
            pl.BlockSpec((heads, CHUNK, CHUNK), lambda c: (0, 0, 0)),
            pl.BlockSpec((heads, CHUNK, V7X_LANES), lambda c: (0, 0, 0)),
            pl.BlockSpec((1, width), lambda c: (0, 0)),
        ],
        out_specs=pl.BlockSpec((CHUNK, width), lambda c: (c, 0)),
        out_shape=jax.ShapeDtypeStruct((t_len, width), BF16),
        scratch_shapes=[pltpu.VMEM((CHUNK, width), F32)],
        compiler_params=_params(("arbitrary",)),
        name="gmlp",
    )(p, p, v_norm_g.reshape(1, width), w_s, b_lanes, out_norm_g.reshape(1, width))


def _outproj_kernel(ys_ref, yg_ref, wa_ref, wb_ref, x_ref, o_ref):
    acc = jnp.dot(ys_ref[...], wa_ref[...], preferred_element_type=F32)
    acc = acc + jnp.dot(yg_ref[...], wb_ref[...], preferred_element_type=F32)
    o_ref[...] = x_ref[...] + acc


def _outproj(y_ssd, y_gmlp, w_out, x):
    t_len, d = x.shape
    ka, kb = y_ssd.shape[1], y_gmlp.shape[1]
    assert ka == kb and w_out.shape[0] == ka + kb
    tm = _largest_tile(t_len, 512, V7X_SUBLANES)
    tn = _largest_tile(d, 512)
    return pl.pallas_call(
        _outproj_kernel,
        grid=(t_len // tm, d // tn),
        in_specs=[
            pl.BlockSpec((tm, ka), lambda i, j: (i, 0)),
            pl.BlockSpec((tm, kb), lambda i, j: (i, 0)),
            pl.BlockSpec((ka, tn), lambda i, j: (0, j)),
            pl.BlockSpec((kb, tn), lambda i, j: (1, j)),
            pl.BlockSpec((tm, tn), lambda i, j: (i, j)),
        ],
        out_specs=pl.BlockSpec((tm, tn), lambda i, j: (i, j)),
        out_shape=jax.ShapeDtypeStruct((t_len, d), F32),
        compiler_params=_params(("arbitrary", "arbitrary")),
        name="outproj",
    )(y_ssd, y_gmlp, w_out, w_out, x)


def _router_kernel(x_ref, g_ref, wr_ref, br_ref, xp_ref, idx_ref, gate_ref, *, n_exp):
    x = x_ref[...]
    tm, d = x.shape
    half = d // 2
    ms = jnp.mean(x * x, axis=-1, keepdims=True)
    xn = x * lax.rsqrt(ms + EPS) * g_ref[...]
    xp_ref[...] = _pack_pair(xn[:, :half], xn[:, half:])

    logits = jnp.dot(xn.astype(BF16), wr_ref[...], preferred_element_type=F32) + br_ref[...]
    lane = lax.broadcasted_iota(I32, (tm, V7X_LANES), 1)
    vals = jnp.where(lane < n_exp, logits, -jnp.inf)
    idx_out = jnp.zeros((tm, V7X_LANES), I32)
    val_out = jnp.full((tm, V7X_LANES), -jnp.inf, F32)
    for k in range(TOP_K):
        m = jnp.max(vals, axis=-1, keepdims=True)
        sel = jnp.min(jnp.where(vals == m, lane, V7X_LANES), axis=-1, keepdims=True)
        idx_out = jnp.where(lane == k, sel, idx_out)
        val_out = jnp.where(lane == k, m, val_out)
        vals = jnp.where(lane == sel, -jnp.inf, vals)
    e = jnp.exp(val_out - jnp.max(val_out, axis=-1, keepdims=True))
    idx_ref[...] = idx_out
    gate_ref[...] = e / jnp.sum(e, axis=-1, keepdims=True)


def _router(x1, g, w_router, b_router):
    t_len, d = x1.shape
    n_exp = w_router.shape[1]
    assert n_exp <= V7X_LANES and TOP_K <= n_exp
    tm = _largest_tile(t_len, 256, V7X_SUBLANES)
    wr = jnp.pad(w_router, ((0, 0), (0, V7X_LANES - n_exp))).astype(BF16)
    br = jnp.pad(b_router, (0, V7X_LANES - n_exp)).reshape(1, V7X_LANES)
    return pl.pallas_call(
        functools.partial(_router_kernel, n_exp=n_exp),
        grid=(t_len // tm,),
        in_specs=[
            pl.BlockSpec((tm, d), lambda i: (i, 0)),
            pl.BlockSpec((1, d), lambda i: (0, 0)),
            pl.BlockSpec((d, V7X_LANES), lambda i: (0, 0)),
            pl.BlockSpec((1, V7X_LANES), lambda i: (0, 0)),
        ],
        out_specs=[
            pl.BlockSpec((tm, d // 2), lambda i: (i, 0)),
            pl.BlockSpec((tm, V7X_LANES), lambda i: (i, 0)),
            pl.BlockSpec((tm, V7X_LANES), lambda i: (i, 0)),
        ],
        out_shape=[
            jax.ShapeDtypeStruct((t_len, d // 2), U32),
            jax.ShapeDtypeStruct((t_len, V7X_LANES), I32),
            jax.ShapeDtypeStruct((t_len, V7X_LANES), F32),
        ],
        compiler_params=_params(("arbitrary",)),
        name="router",
    )(x1, g.reshape(1, d), wr, br)


def _route_plan(top_idx, n_exp):
    t_len, top_k = top_idx.shape
    n_assign = t_len * top_k
    e = top_idx.reshape(n_assign)
    onehot = (e[:, None] == jnp.arange(n_exp, dtype=I32)[None, :]).astype(I32)
    csum = jnp.cumsum(onehot, axis=0)
    rank = jnp.sum(onehot * csum, axis=1) - 1
    counts = csum[-1]
    nb = (counts + MOE_BLK - 1) // MOE_BLK
    sb_end = jnp.cumsum(nb)
    sb_start = sb_end - nb
    dest = sb_start[e] * MOE_BLK + rank
    n_sb = n_assign // MOE_BLK + n_exp
    n_rows = n_sb * MOE_BLK
    row_tok = jnp.zeros((n_rows,), I32).at[dest].set(jnp.arange(n_assign, dtype=I32) // top_k)

    nv = (nb + MOE_NSB - 1) // MOE_NSB
    v_end = jnp.cumsum(nv)
    v_start = v_end - nv
    n_pass = -(-n_sb // MOE_NSB) + n_exp
    vid = jnp.arange(n_pass, dtype=I32)
    ve = jnp.minimum(jnp.searchsorted(v_end, vid, side='right'), n_exp - 1).astype(I32)
    q = vid - v_start[ve]
    valid = vid < v_end[-1]
    vs = jnp.where(valid, sb_start[ve] + q * MOE_NSB, 0).astype(I32)
    vn = jnp.where(valid, jnp.clip(nb[ve] - q * MOE_NSB, 0, MOE_NSB), 0).astype(I32)
    last_e = ve[jnp.maximum(v_end[-1] - 1, 0)]
    ve = jnp.where(valid, ve, last_e).astype(I32)
    used = sb_end[-1:].astype(I32)
    return dict(row_tok=row_tok, dest=dest.astype(I32), ve=ve, vs=vs, vn=vn, used=used, n_rows=n_rows,
                n_pass=n_pass)


def _for_each_block(n, compute, emit):
    def quad(q, carry):
        b = 4 * q
        first = compute(b, 2)
        second = compute(b + 2, 2)
        emit(b, 2, first)
        emit(b + 2, 2, second)
        return carry

    quads = lax.shift_right_logical(n, 2)
    lax.fori_loop(0, quads, quad, 0)

    @pl.when((n & 2) != 0)
    def _():
        emit(4 * quads, 2, compute(4 * quads, 2))

    @pl.when((n & 1) != 0)
    def _():
        emit(n - 1, 1, compute(n - 1, 1))


def _zero_tail(obuf, n_used, n_sb, n_j, out_copy):
    tile = obuf.shape[2]
    obuf[0, 0:MOE_BLK, :] = jnp.zeros((MOE_BLK, tile), obuf.dtype)

    def start(sb, carry):
        for jj in range(n_j):
            out_copy(0, 0, pl.multiple_of(sb * MOE_BLK, MOE_BLK), jj * tile).start()
        return carry

    def wait(sb, carry):
        for jj in range(n_j):
            out_copy(0, 0, 0, 0).wait()
        return carry

    lax.fori_loop(n_used, n_sb, start, 0)
    lax.fori_loop(n_used, n_sb, wait, 0)


def _moe_up_kernel(ve_ref, vs_ref, vn_ref, used_ref, tok_ref, xp_hbm, *rest, tf, n_j):
    wg_refs, wl_refs = rest[:W_SPLIT], rest[W_SPLIT:2 * W_SPLIT]
    bg_ref, bl_ref, act_hbm, xbuf, wc_ref, obuf, xsem, osem, ocnt = rest[2 * W_SPLIT:]
    v = pl.program_id(0)
    j = pl.program_id(1)
    n_pass = pl.num_programs(0)
    half = xbuf.shape[2]
    slot = v % 2
    n_sub = vn_ref[v]

    def row_copy(tok, sl, r0, u):
        window = xbuf.at[sl, pl.ds(r0, ISSUE_UNROLL), :]
        return pltpu.make_async_copy(xp_hbm.at[pl.ds(tok, 1), :], window.at[pl.ds(u, 1), :], xsem.at[sl])

    def gather_start(vv, sl):
        base = vs_ref[vv] * MOE_BLK

        def body(q, carry):
            r0 = pl.multiple_of(q * ISSUE_UNROLL, ISSUE_UNROLL)
            for u in range(ISSUE_UNROLL):
                row_copy(tok_ref[base + r0 + u], sl, r0, u).start()
            return carry

        lax.fori_loop(0, vn_ref[vv] * (MOE_BLK // ISSUE_UNROLL), body, 0)

    def gather_wait(vv, sl):
        def body(i, carry):
            r0 = pl.multiple_of(i * MOE_BLK, MOE_BLK)
            pltpu.make_async_copy(xp_hbm.at[pl.ds(0, MOE_BLK), :], xbuf.at[sl, pl.ds(r0, MOE_BLK), :],
                                  xsem.at[sl]).wait()
            return carry

        lax.fori_loop(0, vn_ref[vv], body, 0)

    def out_copy(p, r0, row, col):
        return pltpu.make_async_copy(obuf.at[p, pl.ds(r0, MOE_BLK), :],
                                     act_hbm.at[pl.ds(row, MOE_BLK), pl.ds(col, tf)], osem.at[p])

    def drain(p):
        def body(i, carry):
            out_copy(p, 0, 0, 0).wait()
            return carry

        lax.fori_loop(0, ocnt[p], body, 0)
        ocnt[p] = 0

    @pl.when(j == 0)
    def _():
        @pl.when(v == 0)
        def _():
            ocnt[0] = 0
            ocnt[1] = 0
            gather_start(0, 0)

        gather_wait(v, slot)

    step = v * n_j + j
    p = step % 2
    drain(p)

    @pl.when(n_sub > 0)
    def _():
        nxt = jnp.minimum(v + 1, n_pass - 1)
        n_next = jnp.where(v + 1 < n_pass, vn_ref[nxt], 0)
        groups = n_next * (MOE_BLK // (ISSUE_UNROLL * n_j))
        first = j * groups
        base = vs_ref[nxt] * MOE_BLK
        n_cast = wc_ref.shape[0] // (W_SPLIT * CAST_ROWS)

        def request(q):
            r0 = pl.multiple_of((first + q) * ISSUE_UNROLL, ISSUE_UNROLL)
            for u in range(ISSUE_UNROLL):
                row_copy(tok_ref[base + r0 + u], 1 - slot, r0, u).start()

        def cast(r, q, with_request):
            rows = pl.ds(pl.multiple_of(q * CAST_ROWS, CAST_ROWS), CAST_ROWS)
            dst = pl.ds(pl.multiple_of((r * n_cast + q) * CAST_ROWS, CAST_ROWS), CAST_ROWS)
            g = wg_refs[r][rows, :]
            l = wl_refs[r][rows, :]
            if with_request:
                request(r * n_cast + q)
            wc_ref[dst, 0:tf] = g.astype(BF16)
            wc_ref[dst, tf:2 * tf] = l.astype(BF16)

        for r in range(W_SPLIT):
            both = jnp.clip(groups - r * n_cast, 0, n_cast)
            lax.fori_loop(0, both, lambda q, c, r=r: (cast(r, q, True), c)[1], 0)
            lax.fori_loop(both, n_cast, lambda q, c, r=r: (cast(r, q, False), c)[1], 0)
        lax.fori_loop(W_SPLIT * n_cast, groups, lambda q, c: (request(q), c)[1], 0)

        def compute(b, nb):
            r0 = pl.multiple_of(b * MOE_BLK, MOE_BLK)
            lo, hi = _unpack_pair(xbuf[slot, pl.ds(r0, nb * MOE_BLK), :])
            h = jnp.dot(lo.astype(BF16), wc_ref[0:half, :], preferred_element_type=F32)
            h = h + jnp.dot(hi.astype(BF16), wc_ref[half:2 * half, :], preferred_element_type=F32)
            glu = jnp.minimum(h[:, 0:tf] + bg_ref[...], SWIGLU_LIMIT)
            lin = jnp.clip(h[:, tf:2 * tf] + bl_ref[...], -SWIGLU_LIMIT, SWIGLU_LIMIT)
            return (glu * jax.nn.sigmoid(SWIGLU_ALPHA * glu) * (lin + 1.0)).astype(BF16)

        def emit(b, nb, act):
            r0 = pl.multiple_of(b * MOE_BLK, MOE_BLK)
            obuf[p, pl.ds(r0, nb * MOE_BLK), :] = act
            for t in range(nb):
                row = pl.multiple_of((vs_ref[v] + b + t) * MOE_BLK, MOE_BLK)
                out_copy(p, r0 + t * MOE_BLK, row, pl.multiple_of(j * tf, tf)).start()

        _for_each_block(n_sub, compute, emit)
        ocnt[p] = n_sub

    @pl.when(step == n_pass * n_j - 1)
    def _():
        drain(0)
        drain(1)
        _zero_tail(obuf, used_ref[0], act_hbm.shape[0] // MOE_BLK, n_j, out_copy)


def _moe_up(xp, plan, w_gate_up, b_gate_up):
    n_exp, d, two_f = w_gate_up.shape
    f = two_f // 2
    half = xp.shape[1]
    assert half * 2 == d
    tf = _largest_tile(f, 256)
    n_j = f // tf
    rows_pass = MOE_NSB * MOE_BLK
    assert MOE_BLK % (ISSUE_UNROLL * n_j) == 0 and d % (W_SPLIT * CAST_ROWS) == 0

    def w_map(off, r=0):
        def index_map(v, j, ve, vs, vn, used, tok):
            jj = jnp.where(vn[v] > 0, j, n_j - 1)
            return (ve[v], r, off + jj)
        return index_map

    def w_specs(off):
        return [pl.BlockSpec((None, d // W_SPLIT, tf), w_map(off, r)) for r in range(W_SPLIT)]

    grid_spec = pltpu.PrefetchScalarGridSpec(
        num_scalar_prefetch=5,
        grid=(plan["n_pass"], n_j),
        in_specs=[
            pl.BlockSpec(memory_space=pl.ANY),
            *w_specs(0),
            *w_specs(n_j),
            pl.BlockSpec((None, 1, tf), w_map(0)),
            pl.BlockSpec((None, 1, tf), w_map(n_j)),
        ],
        out_specs=pl.BlockSpec(memory_space=pl.ANY),
        scratch_shapes=[
            pltpu.VMEM((2, rows_pass, half), U32),
            pltpu.VMEM((d, 2 * tf), BF16),
            pltpu.VMEM((2, rows_pass, tf), BF16),
            pltpu.SemaphoreType.DMA((2,)),
            pltpu.SemaphoreType.DMA((2,)),
            pltpu.SMEM((2,), I32),
        ],
    )
    b3 = b_gate_up.reshape(n_exp, 1, two_f)
    return pl.pallas_call(
        functools.partial(_moe_up_kernel, tf=tf, n_j=n_j),
        grid_spec=grid_spec,
        out_shape=jax.ShapeDtypeStruct((plan["n_rows"], f), BF16),
        compiler_params=_params(("arbitrary", "arbitrary")),
        name="moe_up",
    )(plan["ve"], plan["vs"], plan["vn"], plan["used"], plan["row_tok"], xp,
      *([w_gate_up] * (2 * W_SPLIT)), b3, b3)


def _moe_down_kernel(ve_ref, vs_ref, vn_ref, used_ref, act_hbm, *rest, tn):
    wa_refs, wb_refs = rest[:W_SPLIT], rest[W_SPLIT:2 * W_SPLIT]
    ba_ref, bb_ref, y_hbm, abuf, wc_ref, obuf, asem, osem, ocnt = rest[2 * W_SPLIT:]
    v = pl.program_id(0)
    j = pl.program_id(1)
    n_pass = pl.num_programs(0)
    n_j = pl.num_programs(1)
    slot = v % 2
    n_sub = vn_ref[v]

    def in_copy(row, sl, r0):
        return pltpu.make_async_copy(act_hbm.at[pl.ds(row, MOE_BLK), :],
                                     abuf.at[sl, pl.ds(r0, MOE_BLK), :], asem.at[sl])

    def load_start(vv, sl):
        def body(i, carry):
            r0 = pl.multiple_of(i * MOE_BLK, MOE_BLK)
            in_copy(pl.multiple_of((vs_ref[vv] + i) * MOE_BLK, MOE_BLK), sl, r0).start()
            return carry

        lax.fori_loop(0, vn_ref[vv], body, 0)

    def load_wait(vv, sl):
        def body(i, carry):
            in_copy(0, sl, pl.multiple_of(i * MOE_BLK, MOE_BLK)).wait()
            return carry

        lax.fori_loop(0, vn_ref[vv], body, 0)

    def out_copy(p, r0, row, col):
        return pltpu.make_async_copy(obuf.at[p, pl.ds(r0, MOE_BLK), :],
                                     y_hbm.at[pl.ds(row, MOE_BLK), pl.ds(col, tn)], osem.at[p])

    def drain(p):
        def body(i, carry):
            out_copy(p, 0, 0, 0).wait()
            return carry

        lax.fori_loop(0, ocnt[p], body, 0)
        ocnt[p] = 0

    @pl.when(j == 0)
    def _():
        @pl.when(v == 0)
        def _():
            ocnt[0] = 0
            ocnt[1] = 0
            load_start(0, 0)

        @pl.when(v + 1 < n_pass)
        def _():
            load_start(v + 1, 1 - slot)

        load_wait(v, slot)

    step = v * n_j + j
    p = step % 2
    drain(p)

    @pl.when(n_sub > 0)
    def _():
        piece = wc_ref.shape[0] // W_SPLIT
        for r in range(W_SPLIT):
            wc_ref[r * piece:(r + 1) * piece, 0:tn] = wa_refs[r][...].astype(BF16)
            wc_ref[r * piece:(r + 1) * piece, tn:2 * tn] = wb_refs[r][...].astype(BF16)

        def compute(b, nb):
            r0 = pl.multiple_of(b * MOE_BLK, MOE_BLK)
            y = jnp.dot(abuf[slot, pl.ds(r0, nb * MOE_BLK), :], wc_ref[...], preferred_element_type=F32)
            return _pack_pair(y[:, 0:tn] + ba_ref[...], y[:, tn:2 * tn] + bb_ref[...])

        def emit(b, nb, packed):
            r0 = pl.multiple_of(b * MOE_BLK, MOE_BLK)
            obuf[p, pl.ds(r0, nb * MOE_BLK), :] = packed
            for t in range(nb):
                row = pl.multiple_of((vs_ref[v] + b + t) * MOE_BLK, MOE_BLK)
                out_copy(p, r0 + t * MOE_BLK, row, pl.multiple_of(j * tn, tn)).start()

        _for_each_block(n_sub, compute, emit)
        ocnt[p] = n_sub

    @pl.when(step == n_pass * n_j - 1)
    def _():
        drain(0)
        drain(1)
        _zero_tail(obuf, used_ref[0], y_hbm.shape[0] // MOE_BLK, n_j, out_copy)


def _moe_down(act, plan, w_down, b_down):
    n_exp, f, d = w_down.shape
    half = d // 2
    tn = _largest_tile(half, 256)
    n_j = half // tn
    rows_pass = MOE_NSB * MOE_BLK

    def w_map(off, r=0):
        def index_map(v, j, ve, vs, vn, used):
            jj = jnp.where(vn[v] > 0, j, n_j - 1)
            return (ve[v], r, off + jj)
        return index_map

    def w_specs(off):
        return [pl.BlockSpec((None, f // W_SPLIT, tn), w_map(off, r)) for r in range(W_SPLIT)]

    grid_spec = pltpu.PrefetchScalarGridSpec(
        num_scalar_prefetch=4,
        grid=(plan["n_pass"], n_j),
        in_specs=[
            pl.BlockSpec(memory_space=pl.ANY),
            *w_specs(0),
            *w_specs(n_j),
            pl.BlockSpec((None, 1, tn), w_map(0)),
            pl.BlockSpec((None, 1, tn), w_map(n_j)),
        ],
        out_specs=pl.BlockSpec(memory_space=pl.ANY),
        scratch_shapes=[
            pltpu.VMEM((2, rows_pass, f), BF16),
            pltpu.VMEM((f, 2 * tn), BF16),
            pltpu.VMEM((2, rows_pass, tn), U32),
            pltpu.SemaphoreType.DMA((2,)),
            pltpu.SemaphoreType.DMA((2,)),
            pltpu.SMEM((2,), I32),
        ],
    )
    b3 = b_down.reshape(n_exp, 1, d)
    return pl.pallas_call(
        functools.partial(_moe_down_kernel, tn=tn),
        grid_spec=grid_spec,
        out_shape=jax.ShapeDtypeStruct((plan["n_rows"], half), U32),
        compiler_params=_params(("arbitrary", "arbitrary")),
        name="moe_down",
    )(plan["ve"], plan["vs"], plan["vn"], plan["used"], act, *([w_down] * (2 * W_SPLIT)), b3, b3)


def _combine_kernel(dest_ref, y_hbm, gate_ref, x_ref, g_ref, o_ref, ybuf, sem):
    i = pl.program_id(0)
    n_i = pl.num_programs(0)
    tm, d = x_ref.shape
    half = d // 2
    slot = i % 2

    def row_copy(row, sl, k, t0, u):
        window = ybuf.at[sl, k, pl.ds(t0, ISSUE_UNROLL), :]
        return pltpu.make_async_copy(y_hbm.at[pl.ds(row, 1), :], window.at[pl.ds(u, 1), :], sem.at[sl])

    def request(ii, sl, q):
        t0 = pl.multiple_of(q * ISSUE_UNROLL, ISSUE_UNROLL)
        for u in range(ISSUE_UNROLL):
            for k in range(TOP_K):
                row_copy(dest_ref[(ii * tm + t0 + u) * TOP_K + k], sl, k, t0, u).start()

    def gather_wait(sl):
        for k in range(TOP_K):
            pltpu.make_async_copy(y_hbm.at[pl.ds(0, tm), :], ybuf.at[sl, k], sem.at[sl]).wait()

    def combine(q, prefetch):
        rows = pl.ds(pl.multiple_of(q * ISSUE_UNROLL, ISSUE_UNROLL), ISSUE_UNROLL)
        acc_lo = x_ref[rows, 0:half]
        acc_hi = x_ref[rows, half:d]
        for k in range(TOP_K):
            lo, hi = _unpack_pair(ybuf[slot, k, rows, :])
            gk = gate_ref[rows, k:k + 1]
            acc_lo = acc_lo + gk * lo
            acc_hi = acc_hi + gk * hi
        if prefetch:
            request(i + 1, 1 - slot, q)
        ms = (jnp.sum(acc_lo * acc_lo, axis=-1, keepdims=True)
              + jnp.sum(acc_hi * acc_hi, axis=-1, keepdims=True)) / d
        r = lax.rsqrt(ms + EPS)
        o_ref[rows, 0:half] = acc_lo * r * g_ref[:, 0:half]
        o_ref[rows, half:d] = acc_hi * r * g_ref[:, half:d]

    n_groups = tm // ISSUE_UNROLL

    @pl.when(i == 0)
    def _():
        lax.fori_loop(0, n_groups, lambda q, c: (request(0, 0, q), c)[1], 0)

    gather_wait(slot)

    @pl.when(i + 1 < n_i)
    def _():
        lax.fori_loop(0, n_groups, lambda q, c: (combine(q, True), c)[1], 0, unroll=2)

    @pl.when(i + 1 == n_i)
    def _():
        lax.fori_loop(0, n_groups, lambda q, c: (combine(q, False), c)[1], 0, unroll=2)


def _combine(y_rows, dest, gate, x1, final_g):
    t_len, d = x1.shape
    half = d // 2
    tm = _largest_tile(t_len, 128, V7X_SUBLANES)
    grid_spec = pltpu.PrefetchScalarGridSpec(
        num_scalar_prefetch=1,
        grid=(t_len // tm,),
        in_specs=[
            pl.BlockSpec(memory_space=pl.ANY),
            pl.BlockSpec((tm, V7X_LANES), lambda i, dest: (i, 0)),
            pl.BlockSpec((tm, d), lambda i, dest: (i, 0)),
            pl.BlockSpec((1, d), lambda i, dest: (0, 0)),
        ],
        out_specs=pl.BlockSpec((tm, d), lambda i, dest: (i, 0)),
        scratch_shapes=[
            pltpu.VMEM((2, TOP_K, tm, half), U32),
            pltpu.SemaphoreType.DMA((2,)),
        ],
    )
    return pl.pallas_call(
        _combine_kernel,
        grid_spec=grid_spec,
        out_shape=jax.ShapeDtypeStruct((t_len, d), F32),
        compiler_params=_params(("arbitrary",)),
        name="combine",
    )(dest, y_rows, gate, x1, final_g.reshape(1, d))


def kernel(x, mix_norm_g, w_in, ssd_conv_w, ssd_conv_b, ssd_dt_bias, ssd_a_log, ssd_d, ssd_norm_g,
           gmlp_v_norm_g, gmlp_w_s, gmlp_b_s, gmlp_out_norm_g, w_out, ffn_norm_g, w_router, b_router,
           w_gate_up, b_gate_up, w_down, b_down, final_norm_g):
    bsz, t_len, d = x.shape
    assert bsz == 1
    depth = w_in.shape[0]
    ssd_w = ssd_norm_g.shape[-1]
    heads = ssd_dt_bias.shape[-1]
    gmlp_w = gmlp_out_norm_g.shape[-1]
    conv_dim = ssd_conv_b.shape[-1]
    assert heads <= V7X_LANES
    c_dt = ssd_w + conv_dim
    c_uv = c_dt + heads
    off_u, off_z, off_xs = 0, 2 * gmlp_w, 2 * gmlp_w + ssd_w

    h = x.reshape(t_len, d)
    out = None
    for layer in range(depth):
        w = w_in[layer]
        w_t = jnp.swapaxes(w, 0, 1).astype(BF16)
        w_dt = jnp.pad(w_t[c_dt:c_uv], ((0, V7X_LANES - heads), (0, 0)))
        p, dt_raw = _inproj(h, mix_norm_g[layer], w_t, w_dt, ssd_conv_w[layer], ssd_conv_b[layer],
                            2 * gmlp_w, ssd_w, c_uv)
        groups = (conv_dim - ssd_w) // (2 * SSD_STATE)
        y_ssd = _ssd(p, dt_raw, ssd_dt_bias[layer], ssd_a_log[layer], ssd_d[layer], ssd_norm_g[layer],
                     groups, off_z, off_xs)
        y_gmlp = _gmlp(p, gmlp_v_norm_g[layer], gmlp_w_s[layer], gmlp_b_s[layer],
                       gmlp_out_norm_g[layer], off_u)
        x1 = _outproj(y_ssd, y_gmlp, w_out[layer].astype(BF16), h)

        xp, idx_pad, gate_pad = _router(x1, ffn_norm_g[layer], w_router[layer], b_router[layer])
        plan = _route_plan(idx_pad[:, :TOP_K], w_router.shape[-1])
        act = _moe_up(xp, plan, w_gate_up[layer], b_gate_up[layer])
        y_rows = _moe_down(act, plan, w_down[layer], b_down[layer])
        last = layer == depth - 1
        assert last, "multi-layer stacking needs an un-normalised combine"
        out = _combine(y_rows, plan["dest"], gate_pad, x1, final_norm_g)
    return out.reshape(bsz, t_len, d)
```

```python
import functools
import math

import jax
import jax.numpy as jnp
from jax import lax
from jax.experimental import pallas as pl
from jax.experimental.pallas import tpu as pltpu

F32, BF16, U32, I32 = jnp.float32, jnp.bfloat16, jnp.uint32, jnp.int32

EPS = 1e-5
SSD_STATE = 128
SSD_CONV = 4
CHUNK = 128
TOP_K = 4
SSD_ROWS = 512
SWIGLU_ALPHA = 1.702
SWIGLU_LIMIT = 7.0

V7X_LANES = 128
V7X_SUBLANES = 8
V7X_VMEM_BYTES = 64 * 1024 * 1024
VMEM_LIMIT = V7X_VMEM_BYTES - 8 * 1024 * 1024

MOE_BLK = 128
MOE_NSB = 12
ISSUE_UNROLL = 8
HI_MASK = 0xFFFF0000


_gcd = math.gcd


def _largest_tile(total, cap, quantum=V7X_LANES):
    best = None
    t = quantum
    while t <= min(total, cap):
        if total % t == 0:
            best = t
        t += quantum
    assert best is not None, (total, cap, quantum)
    return best


def _params(semantics):
    return pltpu.CompilerParams(dimension_semantics=semantics, vmem_limit_bytes=VMEM_LIMIT)


def _silu(x):
    return x * jax.nn.sigmoid(x)


def _gelu(x):
    return 0.5 * x * (1.0 + lax.erf(x * (2.0 ** -0.5)))


def _softplus(x):
    return jnp.maximum(x, 0.0) + jnp.log1p(jnp.exp(-jnp.abs(x)))


def _pack_pair(lo, hi):
    lo_bits = lax.bitcast_convert_type(lo.astype(BF16).astype(F32), U32)
    hi_bits = lax.bitcast_convert_type(hi.astype(BF16).astype(F32), U32)
    return (lo_bits >> 16) | (hi_bits & jnp.uint32(HI_MASK))


def _unpack_pair(u):
    lo = lax.bitcast_convert_type(u << 16, F32)
    hi = lax.bitcast_convert_type(u & jnp.uint32(HI_MASK), F32)
    return lo, hi


NT_DIMS = (((1,), (1,)), ((), ()))


def _inproj_kernel(x_hbm, g_ref, w_ref, wdt_ref, cw_ref, cb_ref, p_ref, dt_ref, xbuf, xn_ref, halo_ref, sem,
                   *, n_gelu, n_silu):
    i = pl.program_id(0)
    j = pl.program_id(1)
    tm = xbuf.shape[0]
    rows = _largest_tile(tm, 128, V7X_SUBLANES)
    tail = halo_ref.shape[1]

    def x_copy(ii):
        return pltpu.make_async_copy(x_hbm.at[pl.ds(pl.multiple_of(ii * tm, tm), tm), :], xbuf, sem.at[0])

    @pl.when(j == 0)
    def _():
        @pl.when(i == 0)
        def _():
            x_copy(0).start()
            halo_ref[...] = jnp.zeros(halo_ref.shape, F32)

        x_copy(i).wait()

        def norm(r, carry):
            sl = pl.ds(pl.multiple_of(r * rows, rows), rows)
            x = xbuf[sl, :]
            ms = jnp.mean(x * x, axis=-1, keepdims=True)
            xn = (x * lax.rsqrt(ms + EPS) * g_ref[...]).astype(BF16)
            xn_ref[sl, :] = xn
            dt_ref[sl, :] = lax.dot_general(xn, wdt_ref[...], NT_DIMS, preferred_element_type=F32)
            return carry

        lax.fori_loop(0, tm // rows, norm, 0)

        @pl.when(i + 1 < pl.num_programs(0))
        def _():
            x_copy(i + 1).start()

    def project():
        return lax.dot_general(xn_ref[...], w_ref[...].astype(BF16), NT_DIMS, preferred_element_type=F32)

    @pl.when(j < n_gelu)
    def _():
        p_ref[...] = _gelu(project()).astype(BF16)

    @pl.when((j >= n_gelu) & (j < n_gelu + n_silu))
    def _():
        p_ref[...] = _silu(project()).astype(BF16)

    @pl.when(j >= n_gelu + n_silu)
    def _():
        x = project()
        jc = j - (n_gelu + n_silu)
        ext = jnp.concatenate([halo_ref[jc], x], axis=0)
        acc = cb_ref[...] + cw_ref[SSD_CONV - 1:SSD_CONV, :] * ext
        for s in range(1, SSD_CONV):
            acc = acc + cw_ref[SSD_CONV - 1 - s:SSD_CONV - s, :] * pltpu.roll(ext, s, axis=0)
        p_ref[...] = _silu(acc[tail:, :]).astype(BF16)
        halo_ref[jc] = x[tm - tail:tm, :]


def _inproj(x, g, w_t, w_dt, conv_w, conv_b, n_gelu_cols, n_silu_cols, row_gelu):
    t_len, d = x.shape
    n_conv_cols = conv_b.shape[0]
    n_out = n_gelu_cols + n_silu_cols + n_conv_cols
    assert row_gelu + n_gelu_cols == w_t.shape[0] and row_gelu >= n_silu_cols + n_conv_cols
    tm = _largest_tile(t_len, 1024, V7X_SUBLANES)
    tn = _largest_tile(_gcd(_gcd(n_gelu_cols, n_silu_cols), n_conv_cols), 512)
    n_gelu, n_silu, n_conv = n_gelu_cols // tn, n_silu_cols // tn, n_conv_cols // tn
    tail = 2 * V7X_SUBLANES
    assert row_gelu % tail == 0 and tn % tail == 0
    conv_map = lambda i, j: (0, jnp.maximum(j - (n_gelu + n_silu), 0))
    w_map = lambda i, j: (
        tail * jnp.where(j < n_gelu, row_gelu // tail + j * (tn // tail), (j - n_gelu) * (tn // tail)), 0)
    return pl.pallas_call(
        functools.partial(_inproj_kernel, n_gelu=n_gelu, n_silu=n_silu),
        grid=(t_len // tm, n_out // tn),
        in_specs=[
            pl.BlockSpec(memory_space=pl.ANY),
            pl.BlockSpec((1, d), lambda i, j: (0, 0)),
            pl.BlockSpec((pl.Element(tn), pl.Element(d)), w_map),
            pl.BlockSpec((V7X_LANES, d), lambda i, j: (0, 0)),
            pl.BlockSpec((SSD_CONV, tn), conv_map),
            pl.BlockSpec((1, tn), conv_map),
        ],
        out_specs=[
            pl.BlockSpec((tm, tn), lambda i, j: (i, j)),
            pl.BlockSpec((tm, V7X_LANES), lambda i, j: (i, 0)),
        ],
        out_shape=[
            jax.ShapeDtypeStruct((t_len, n_out), BF16),
            jax.ShapeDtypeStruct((t_len, V7X_LANES), F32),
        ],
        scratch_shapes=[
            pltpu.VMEM((tm, d), F32),
            pltpu.VMEM((tm, d), BF16),
            pltpu.VMEM((n_conv, tail, tn), F32),
            pltpu.SemaphoreType.DMA((1,)),
        ],
        compiler_params=_params(("arbitrary", "arbitrary")),
        name="inproj",
    )(x, g.reshape(1, d), w_t, w_dt, conv_w, conv_b.reshape(1, n_conv_cols))


def _split3(x):
    hi = x.astype(BF16)
    r1 = x - hi.astype(F32)
    mid = r1.astype(BF16)
    lo = (r1 - mid.astype(F32)).astype(BF16)
    return hi, mid, lo


def _ssd_kernel(sz_ref, xs_ref, b_ref, c_ref, dt_ref, dtb_ref, alog_ref, dsk_ref, ng_ref, y_ref, state_ref,
                *, hpg, hd):
    width = hpg * hd
    lanes = V7X_LANES
    per = lanes // hd

    @pl.when(pl.program_id(1) == 0)
    def _():
        state_ref[...] = jnp.zeros(state_ref.shape, F32)

    ri = lax.broadcasted_iota(I32, (CHUNK, CHUNK), 0)
    ci = lax.broadcasted_iota(I32, (CHUNK, CHUNK), 1)
    causal = ri >= ci
    tril = causal.astype(BF16)
    neg_a = -jnp.exp(alog_ref[...])

    def expand(q):
        cols = []
        for j in range(width // lanes):
            blk = jnp.broadcast_to(q[:, j * per:j * per + 1], (CHUNK, lanes))
            for t in range(1, per):
                nxt = jnp.broadcast_to(q[:, j * per + t:j * per + t + 1], (CHUNK, lanes))
                blk = jnp.where(ci >= t * hd, nxt, blk)
            cols.append(blk)
        return jnp.concatenate(cols, axis=1)

    def chunk(k, carry):
        rows = pl.ds(pl.multiple_of(k * CHUNK, CHUNK), CHUNK)
        xs = xs_ref[rows, :].astype(F32)
        bm = b_ref[rows, :]
        cm = c_ref[rows, :]

        dt = _softplus(dt_ref[rows, :] + dtb_ref[...])
        a_dt = dt * neg_a
        acum = sum(jnp.dot(tril, piece, preferred_element_type=F32) for piece in _split3(a_dt))
        acum_t = acum.T
        acum_f = expand(acum)
        last = acum_f[CHUNK - 1:CHUNK, :]
        xdt = xs * expand(dt)

        cb = lax.dot_general(cm, bm, (((1,), (1,)), ((), ())), preferred_element_type=F32)
        y_cols = []
        for j in range(width // lanes):
            x_col = xdt[:, j * lanes:(j + 1) * lanes]
            y_col = jnp.zeros((CHUNK, lanes), F32)
            for t in range(per):
                h = j * per + t
                seg = acum[:, h:h + 1] - acum_t[h:h + 1, :]
                lmat = jnp.exp(jnp.where(causal, seg, -jnp.inf))
                m_h = (cb * lmat).astype(BF16)
                mask = (ci >= t * hd) & (ci < (t + 1) * hd)
                x_h = jnp.where(mask, x_col, 0.0).astype(BF16)
                y_col = y_col + jnp.dot(m_h, x_h, preferred_element_type=F32)
            y_cols.append(y_col)
        y_diag = jnp.concatenate(y_cols, axis=1)

        state = state_ref[...]
        y_off = jnp.dot(cm, state.astype(BF16), preferred_element_type=F32) * jnp.exp(acum_f)
        x_dec = (xdt * jnp.exp(last - acum_f)).astype(BF16)
        s_new = lax.dot_general(bm, x_dec, (((0,), (0,)), ((), ())), preferred_element_type=F32)
        state_ref[...] = state * jnp.exp(last) + s_new

        yg = (y_diag + y_off + xs * dsk_ref[...]) * sz_ref[rows, :].astype(F32)
        yn = yg * lax.rsqrt(jnp.mean(yg * yg, axis=-1, keepdims=True) + EPS) * ng_ref[...]
        y_ref[rows, :] = yn.astype(BF16)
        return carry

    lax.fori_loop(0, y_ref.shape[0] // CHUNK, chunk, 0, unroll=2)


def _ssd(p, dt_raw, dt_bias, a_log, d_skip, norm_g, groups, off_z, off_xs):
    t_len = p.shape[0]
    heads = dt_bias.shape[0]
    width = norm_g.shape[0]
    hd = width // heads
    hpg = heads // groups
    gw = hpg * hd
    n_st = SSD_STATE
    rows = _largest_tile(t_len, SSD_ROWS, CHUNK)
    assert V7X_LANES % hd == 0 and gw % V7X_LANES == 0 and hpg <= V7X_LANES
    assert width // groups == gw
    off_b = off_xs + width
    off_c = off_b + groups * n_st
    assert off_z % gw == 0 and off_xs % gw == 0 and off_b % n_st == 0

    def per_group_lanes(v):
        v = v.reshape(groups, 1, hpg)
        return jnp.pad(v, ((0, 0), (0, 0), (0, V7X_LANES - hpg)))

    dt_g = dt_raw[:, :heads].reshape(t_len, groups, hpg).transpose(1, 0, 2)
    dt_g = jnp.pad(dt_g, ((0, 0), (0, 0), (0, V7X_LANES - hpg)))

    grp = lambda g, c: (g, 0, 0)
    return pl.pallas_call(
        functools.partial(_ssd_kernel, hpg=hpg, hd=hd),
        grid=(groups, t_len // rows),
        in_specs=[
            pl.BlockSpec((rows, gw), lambda g, c: (c, off_z // gw + g)),
            pl.BlockSpec((rows, gw), lambda g, c: (c, off_xs // gw + g)),
            pl.BlockSpec((rows, n_st), lambda g, c: (c, off_b // n_st + g)),
            pl.BlockSpec((rows, n_st), lambda g, c: (c, off_c // n_st + g)),
            pl.BlockSpec((None, rows, V7X_LANES), lambda g, c: (g, c, 0)),
            pl.BlockSpec((None, 1, V7X_LANES), grp),
            pl.BlockSpec((None, 1, V7X_LANES), grp),
            pl.BlockSpec((1, gw), lambda g, c: (0, g)),
            pl.BlockSpec((1, gw), lambda g, c: (0, g)),
        ],
        out_specs=pl.BlockSpec((rows, gw), lambda g, c: (c, g)),
        out_shape=jax.ShapeDtypeStruct((t_len, width), BF16),
        scratch_shapes=[pltpu.VMEM((n_st, gw), F32)],
        compiler_params=_params(("arbitrary", "arbitrary")),
        name="ssd",
    )(p, p, p, p, dt_g, per_group_lanes(dt_bias), per_group_lanes(a_log),
      jnp.repeat(d_skip, hd).reshape(1, width), norm_g.reshape(1, width))


def _row_sums(x):
    ones = jnp.ones((x.shape[1], V7X_LANES), BF16)
    hi = x.astype(BF16)
    lo = (x - hi.astype(F32)).astype(BF16)
    return (jnp.dot(hi, ones, preferred_element_type=F32) + jnp.dot(lo, ones, preferred_element_type=F32))


def _gmlp_kernel(u_ref, v_ref, vg_ref, ws_ref, bs_ref, og_ref, y_ref, yscr_ref, *, heads, hd):
    lanes = V7X_LANES
    reps = hd // lanes
    ri = lax.broadcasted_iota(I32, (CHUNK, CHUNK), 0)
    ci = lax.broadcasted_iota(I32, (CHUNK, CHUNK), 1)
    causal = ri >= ci
    ssq = jnp.zeros((CHUNK, lanes), F32)
    invs = []
    for h in range(heads):
        v_h = v_ref[:, h * hd:(h + 1) * hd].astype(F32)
        invs.append(lax.rsqrt(_row_sums(v_h * v_h) / hd + EPS))
    for h in range(heads):
        cols = slice(h * hd, (h + 1) * hd)
        u_h = u_ref[:, cols].astype(F32)
        v_h = v_ref[:, cols].astype(F32)
        v_n = v_h * jnp.tile(invs[h], (1, reps)) * vg_ref[:, cols]
        w_h = jnp.where(causal, ws_ref[h], 0.0).astype(BF16)
        sv = jnp.dot(w_h, v_n.astype(BF16), preferred_element_type=F32) + jnp.tile(bs_ref[h], (1, reps))
        y_h = u_h * sv
        yscr_ref[:, cols] = y_h
        y_sq = y_h * y_h
        for r in range(reps):
            ssq = ssq + y_sq[:, r * lanes:(r + 1) * lanes]
    scale = lax.rsqrt(_row_sums(ssq) / (heads * hd) + EPS)
    y_ref[...] = (yscr_ref[...] * jnp.tile(scale, (1, heads * reps)) * og_ref[...]).astype(BF16)


def _gmlp(p, v_norm_g, w_s, b_s, out_norm_g, off_u):
    t_len = p.shape[0]
    heads = w_s.shape[0]
    width = out_norm_g.shape[0]
    hd = width // heads
    assert w_s.shape[1] == CHUNK and off_u % width == 0 and hd % V7X_LANES == 0
    b_lanes = jnp.broadcast_to(b_s[:, :, None], (heads, CHUNK, V7X_LANES))
    return pl.pallas_call(
        functools.partial(_gmlp_kernel, heads=heads, hd=hd),
        grid=(t_len // CHUNK,),
        in_specs=[
            pl.BlockSpec((CHUNK, width), lambda c: (c, off_u // width)),
            pl.BlockSpec((CHUNK, width), lambda c: (c, off_u // width + 1)),
            pl.BlockSpec((1, width), lambda c: (0, 0)),
            pl.BlockSpec((heads, CHUNK, CHUNK), lambda c: (0, 0, 0)),
            pl.BlockSpec((heads, CHUNK, V7X_LANES), lambda c: (0, 0, 0)),
            pl.BlockSpec((1, width), lambda c: (0, 0)),
        ],
        out_specs=pl.BlockSpec((CHUNK, width), lambda c: (c, 0)),
        out_shape=jax.ShapeDtypeStruct((t_len, width), BF16),
        scratch_shapes=[pltpu.VMEM((CHUNK, width), F32)],
        compiler_params=_params(("arbitrary",)),
        name="gmlp",
    )(p, p, v_norm_g.reshape(1, width), w_s, b_lanes, out_norm_g.reshape(1, width))


def _outproj_kernel(ys_ref, yg_ref, wa_ref, wb_ref, x_ref, o_ref):
    acc = jnp.dot(ys_ref[...], wa_ref[...], preferred_element_type=F32)
    acc = acc + jnp.dot(yg_ref[...], wb_ref[...], preferred_element_type=F32)
    o_ref[...] = x_ref[...] + acc


def _outproj(y_ssd, y_gmlp, w_out, x):
    t_len, d = x.shape
    ka, kb = y_ssd.shape[1], y_gmlp.shape[1]
    assert ka == kb and w_out.shape[0] == ka + kb
    tm = _largest_tile(t_len, 512, V7X_SUBLANES)
    tn = _largest_tile(d, 512)
    return pl.pallas_call(
        _outproj_kernel,
        grid=(t_len // tm, d // tn),
        in_specs=[
            pl.BlockSpec((tm, ka), lambda i, j: (i, 0)),
            pl.BlockSpec((tm, kb), lambda i, j: (i, 0)),
            pl.BlockSpec((ka, tn), lambda i, j: (0, j)),
            pl.BlockSpec((kb, tn), lambda i, j: (1, j)),
            pl.BlockSpec((tm, tn), lambda i, j: (i, j)),
        ],
        out_specs=pl.BlockSpec((tm, tn), lambda i, j: (i, j)),
        out_shape=jax.ShapeDtypeStruct((t_len, d), F32),
        compiler_params=_params(("arbitrary", "arbitrary")),
        name="outproj",
    )(y_ssd, y_gmlp, w_out, w_out, x)


def _router_kernel(x_ref, g_ref, wr_ref, br_ref, xp_ref, idx_ref, gate_ref, *, n_exp):
    x = x_ref[...]
    tm, d = x.shape
    half = d // 2
    ms = jnp.mean(x * x, axis=-1, keepdims=True)
    xn = x * lax.rsqrt(ms + EPS) * g_ref[...]
    xp_ref[...] = _pack_pair(xn[:, :half], xn[:, half:])

    logits = jnp.dot(xn.astype(BF16), wr_ref[...], preferred_element_type=F32) + br_ref[...]
    lane = lax.broadcasted_iota(I32, (tm, V7X_LANES), 1)
    vals = jnp.where(lane < n_exp, logits, -jnp.inf)
    idx_out = jnp.zeros((tm, V7X_LANES), I32)
    val_out = jnp.full((tm, V7X_LANES), -jnp.inf, F32)
    for k in range(TOP_K):
        m = jnp.max(vals, axis=-1, keepdims=True)
        sel = jnp.min(jnp.where(vals == m, lane, V7X_LANES), axis=-1, keepdims=True)
        idx_out = jnp.where(lane == k, sel, idx_out)
        val_out = jnp.where(lane == k, m, val_out)
        vals = jnp.where(lane == sel, -jnp.inf, vals)
    e = jnp.exp(val_out - jnp.max(val_out, axis=-1, keepdims=True))
    idx_ref[...] = idx_out
    gate_ref[...] = e / jnp.sum(e, axis=-1, keepdims=True)


def _router(x1, g, w_router, b_router):
    t_len, d = x1.shape
    n_exp = w_router.shape[1]
    assert n_exp <= V7X_LANES and TOP_K <= n_exp
    tm = _largest_tile(t_len, 256, V7X_SUBLANES)
    wr = jnp.pad(w_router, ((0, 0), (0, V7X_LANES - n_exp))).astype(BF16)
    br = jnp.pad(b_router, (0, V7X_LANES - n_exp)).reshape(1, V7X_LANES)
    return pl.pallas_call(
        functools.partial(_router_kernel, n_exp=n_exp),
        grid=(t_len // tm,),
        in_specs=[
            pl.BlockSpec((tm, d), lambda i: (i, 0)),
            pl.BlockSpec((1, d), lambda i: (0, 0)),
            pl.BlockSpec((d, V7X_LANES), lambda i: (0, 0)),
            pl.BlockSpec((1, V7X_LANES), lambda i: (0, 0)),
        ],
        out_specs=[
            pl.BlockSpec((tm, d // 2), lambda i: (i, 0)),
            pl.BlockSpec((tm, V7X_LANES), lambda i: (i, 0)),
            pl.BlockSpec((tm, V7X_LANES), lambda i: (i, 0)),
        ],
        out_shape=[
            jax.ShapeDtypeStruct((t_len, d // 2), U32),
            jax.ShapeDtypeStruct((t_len, V7X_LANES), I32),
            jax.ShapeDtypeStruct((t_len, V7X_LANES), F32),
        ],
        compiler_params=_params(("arbitrary",)),
        name="router",
    )(x1, g.reshape(1, d), wr, br)


def _route_plan(top_idx, n_exp):
    t_len, top_k = top_idx.shape
    n_assign = t_len * top_k
    e = top_idx.reshape(n_assign)
    onehot = (e[:, None] == jnp.arange(n_exp, dtype=I32)[None, :]).astype(I32)
    csum = jnp.cumsum(onehot, axis=0)
    rank = jnp.sum(onehot * csum, axis=1) - 1
    counts = csum[-1]
    nb = (counts + MOE_BLK - 1) // MOE_BLK
    sb_end = jnp.cumsum(nb)
    sb_start = sb_end - nb
    dest = sb_start[e] * MOE_BLK + rank
    n_sb = n_assign // MOE_BLK + n_exp
    n_rows = n_sb * MOE_BLK
    row_tok = jnp.zeros((n_rows,), I32).at[dest].set(jnp.arange(n_assign, dtype=I32) // top_k)

    nv = (nb + MOE_NSB - 1) // MOE_NSB
    v_end = jnp.cumsum(nv)
    v_start = v_end - nv
    n_pass = -(-n_sb // MOE_NSB) + n_exp
    vid = jnp.arange(n_pass, dtype=I32)
    ve = jnp.minimum(jnp.searchsorted(v_end, vid, side='right'), n_exp - 1).astype(I32)
    q = vid - v_start[ve]
    valid = vid < v_end[-1]
    vs = jnp.where(valid, sb_start[ve] + q * MOE_NSB, 0).astype(I32)
    vn = jnp.where(valid, jnp.clip(nb[ve] - q * MOE_NSB, 0, MOE_NSB), 0).astype(I32)
    last_e = ve[jnp.maximum(v_end[-1] - 1, 0)]
    ve = jnp.where(valid, ve, last_e).astype(I32)
    used = sb_end[-1:].astype(I32)
    return dict(row_tok=row_tok, dest=dest.astype(I32), ve=ve, vs=vs, vn=vn, used=used, n_rows=n_rows,
                n_pass=n_pass)


def _for_each_block(n, compute, emit):
    def quad(q, carry):
        b = 4 * q
        first = compute(b, 2)
        second = compute(b + 2, 2)
        emit(b, 2, first)
        emit(b + 2, 2, second)
        return carry

    quads = lax.shift_right_logical(n, 2)
    lax.fori_loop(0, quads, quad, 0)

    @pl.when((n & 2) != 0)
    def _():
        emit(4 * quads, 2, compute(4 * quads, 2))

    @pl.when((n & 1) != 0)
    def _():
        emit(n - 1, 1, compute(n - 1, 1))


def _zero_tail(obuf, n_used, n_sb, n_j, out_copy):
    tile = obuf.shape[2]
    obuf[0, 0:MOE_BLK, :] = jnp.zeros((MOE_BLK, tile), obuf.dtype)

    def start(sb, carry):
        for jj in range(n_j):
            out_copy(0, 0, pl.multiple_of(sb * MOE_BLK, MOE_BLK), jj * tile).start()
        return carry

    def wait(sb, carry):
        for jj in range(n_j):
            out_copy(0, 0, 0, 0).wait()
        return carry

    lax.fori_loop(n_used, n_sb, start, 0)
    lax.fori_loop(n_used, n_sb, wait, 0)


def _moe_up_kernel(ve_ref, vs_ref, vn_ref, used_ref, tok_ref, xp_hbm, wg_ref, wl_ref, bg_ref, bl_ref, act_hbm,
                   xbuf, obuf, xsem, osem, ocnt, *, tf, n_j):
    v = pl.program_id(0)
    j = pl.program_id(1)
    n_pass = pl.num_programs(0)
    half = xbuf.shape[2]
    slot = v % 2
    n_sub = vn_ref[v]

    def row_copy(tok, sl, r0, u):
        window = xbuf.at[sl, pl.ds(r0, ISSUE_UNROLL), :]
        return pltpu.make_async_copy(xp_hbm.at[pl.ds(tok, 1), :], window.at[pl.ds(u, 1), :], xsem.at[sl])

    def gather_start(vv, sl):
        base = vs_ref[vv] * MOE_BLK

        def body(q, carry):
            r0 = pl.multiple_of(q * ISSUE_UNROLL, ISSUE_UNROLL)
            for u in range(ISSUE_UNROLL):
                row_copy(tok_ref[base + r0 + u], sl, r0, u).start()
            return carry

        lax.fori_loop(0, vn_ref[vv] * (MOE_BLK // ISSUE_UNROLL), body, 0)

    def gather_wait(vv, sl):
        def body(i, carry):
            r0 = pl.multiple_of(i * MOE_BLK, MOE_BLK)
            pltpu.make_async_copy(xp_hbm.at[pl.ds(0, MOE_BLK), :], xbuf.at[sl, pl.ds(r0, MOE_BLK), :],
                                  xsem.at[sl]).wait()
            return carry

        lax.fori_loop(0, vn_ref[vv], body, 0)

    def out_copy(p, r0, row, col):
        return pltpu.make_async_copy(obuf.at[p, pl.ds(r0, MOE_BLK), :],
                                     act_hbm.at[pl.ds(row, MOE_BLK), pl.ds(col, tf)], osem.at[p])

    def drain(p):
        def body(i, carry):
            out_copy(p, 0, 0, 0).wait()
            return carry

        lax.fori_loop(0, ocnt[p], body, 0)
        ocnt[p] = 0

    @pl.when(j == 0)
    def _():
        @pl.when(v == 0)
        def _():
            ocnt[0] = 0
            ocnt[1] = 0
            gather_start(0, 0)

        gather_wait(v, slot)

    step = v * n_j + j
    p = step % 2
    drain(p)

    @pl.when(n_sub > 0)
    def _():
        nxt = jnp.minimum(v + 1, n_pass - 1)
        n_next = jnp.where(v + 1 < n_pass, vn_ref[nxt], 0)
        groups = n_next * (MOE_BLK // (ISSUE_UNROLL * n_j))
        first = j * groups
        base = vs_ref[nxt] * MOE_BLK

        def request(q, carry):
            r0 = pl.multiple_of((first + q) * ISSUE_UNROLL, ISSUE_UNROLL)
            for u in range(ISSUE_UNROLL):
                row_copy(tok_ref[base + r0 + u], 1 - slot, r0, u).start()
            return carry

        lax.fori_loop(0, groups, request, 0)

        def compute(b, nb):
            r0 = pl.multiple_of(b * MOE_BLK, MOE_BLK)
            lo, hi = _unpack_pair(xbuf[slot, pl.ds(r0, nb * MOE_BLK), :])
            w_lo = jnp.concatenate([wg_ref[0:half, :], wl_ref[0:half, :]], axis=1).astype(BF16)
            w_hi = jnp.concatenate([wg_ref[half:2 * half, :], wl_ref[half:2 * half, :]], axis=1).astype(BF16)
            h = jnp.dot(lo.astype(BF16), w_lo, preferred_element_type=F32)
            h = h + jnp.dot(hi.astype(BF16), w_hi, preferred_element_type=F32)
            glu = jnp.minimum(h[:, 0:tf] + bg_ref[...], SWIGLU_LIMIT)
            lin = jnp.clip(h[:, tf:2 * tf] + bl_ref[...], -SWIGLU_LIMIT, SWIGLU_LIMIT)
            return (glu * jax.nn.sigmoid(SWIGLU_ALPHA * glu) * (lin + 1.0)).astype(BF16)

        def emit(b, nb, act):
            r0 = pl.multiple_of(b * MOE_BLK, MOE_BLK)
            obuf[p, pl.ds(r0, nb * MOE_BLK), :] = act
            for t in range(nb):
                row = pl.multiple_of((vs_ref[v] + b + t) * MOE_BLK, MOE_BLK)
                out_copy(p, r0 + t * MOE_BLK, row, pl.multiple_of(j * tf, tf)).start()

        _for_each_block(n_sub, compute, emit)
        ocnt[p] = n_sub

    @pl.when(step == n_pass * n_j - 1)
    def _():
        drain(0)
        drain(1)
        _zero_tail(obuf, used_ref[0], act_hbm.shape[0] // MOE_BLK, n_j, out_copy)


def _moe_up(xp, plan, w_gate_up, b_gate_up):
    n_exp, d, two_f = w_gate_up.shape
    f = two_f // 2
    half = xp.shape[1]
    assert half * 2 == d
    tf = _largest_tile(f, 256)
    n_j = f // tf
    rows_pass = MOE_NSB * MOE_BLK
    assert MOE_BLK % (ISSUE_UNROLL * n_j) == 0

    def w_map(off):
        def index_map(v, j, ve, vs, vn, used, tok):
            jj = jnp.where(vn[v] > 0, j, n_j - 1)
            return (ve[v], 0, off + jj)
        return index_map

    grid_spec = pltpu.PrefetchScalarGridSpec(
        num_scalar_prefetch=5,
        grid=(plan["n_pass"], n_j),
        in_specs=[
            pl.BlockSpec(memory_space=pl.ANY),
            pl.BlockSpec((None, d, tf), w_map(0)),
            pl.BlockSpec((None, d, tf), w_map(n_j)),
            pl.BlockSpec((None, 1, tf), w_map(0)),
            pl.BlockSpec((None, 1, tf), w_map(n_j)),
        ],
        out_specs=pl.BlockSpec(memory_space=pl.ANY),
        scratch_shapes=[
            pltpu.VMEM((2, rows_pass, half), U32),
            pltpu.VMEM((2, rows_pass, tf), BF16),
            pltpu.SemaphoreType.DMA((2,)),
            pltpu.SemaphoreType.DMA((2,)),
            pltpu.SMEM((2,), I32),
        ],
    )
    b3 = b_gate_up.reshape(n_exp, 1, two_f)
    return pl.pallas_call(
        functools.partial(_moe_up_kernel, tf=tf, n_j=n_j),
        grid_spec=grid_spec,
        out_shape=jax.ShapeDtypeStruct((plan["n_rows"], f), BF16),
        compiler_params=_params(("arbitrary", "arbitrary")),
        name="moe_up",
    )(plan["ve"], plan["vs"], plan["vn"], plan["used"], plan["row_tok"], xp, w_gate_up, w_gate_up, b3, b3)


def _moe_down_kernel(ve_ref, vs_ref, vn_ref, used_ref, act_hbm, wa_ref, wb_ref, ba_ref, bb_ref, y_hbm,
                     abuf, obuf, asem, osem, ocnt, *, tn):
    v = pl.program_id(0)
    j = pl.program_id(1)
    n_pass = pl.num_programs(0)
    n_j = pl.num_programs(1)
    slot = v % 2
    n_sub = vn_ref[v]

    def in_copy(row, sl, r0):
        return pltpu.make_async_copy(act_hbm.at[pl.ds(row, MOE_BLK), :],
                                     abuf.at[sl, pl.ds(r0, MOE_BLK), :], asem.at[sl])

    def load_start(vv, sl):
        def body(i, carry):
            r0 = pl.multiple_of(i * MOE_BLK, MOE_BLK)
            in_copy(pl.multiple_of((vs_ref[vv] + i) * MOE_BLK, MOE_BLK), sl, r0).start()
            return carry

        lax.fori_loop(0, vn_ref[vv], body, 0)

    def load_wait(vv, sl):
        def body(i, carry):
            in_copy(0, sl, pl.multiple_of(i * MOE_BLK, MOE_BLK)).wait()
            return carry

        lax.fori_loop(0, vn_ref[vv], body, 0)

    def out_copy(p, r0, row, col):
        return pltpu.make_async_copy(obuf.at[p, pl.ds(r0, MOE_BLK), :],
                                     y_hbm.at[pl.ds(row, MOE_BLK), pl.ds(col, tn)], osem.at[p])

    def drain(p):
        def body(i, carry):
            out_copy(p, 0, 0, 0).wait()
            return carry

        lax.fori_loop(0, ocnt[p], body, 0)
        ocnt[p] = 0

    @pl.when(j == 0)
    def _():
        @pl.when(v == 0)
        def _():
            ocnt[0] = 0
            ocnt[1] = 0
            load_start(0, 0)

        @pl.when(v + 1 < n_pass)
        def _():
            load_start(v + 1, 1 - slot)

        load_wait(v, slot)

    step = v * n_j + j
    p = step % 2
    drain(p)

    @pl.when(n_sub > 0)
    def _():
        def compute(b, nb):
            r0 = pl.multiple_of(b * MOE_BLK, MOE_BLK)
            w = jnp.concatenate([wa_ref[...], wb_ref[...]], axis=1).astype(BF16)
            y = jnp.dot(abuf[slot, pl.ds(r0, nb * MOE_BLK), :], w, preferred_element_type=F32)
            return _pack_pair(y[:, 0:tn] + ba_ref[...], y[:, tn:2 * tn] + bb_ref[...])

        def emit(b, nb, packed):
            r0 = pl.multiple_of(b * MOE_BLK, MOE_BLK)
            obuf[p, pl.ds(r0, nb * MOE_BLK), :] = packed
            for t in range(nb):
                row = pl.multiple_of((vs_ref[v] + b + t) * MOE_BLK, MOE_BLK)
                out_copy(p, r0 + t * MOE_BLK, row, pl.multiple_of(j * tn, tn)).start()

        _for_each_block(n_sub, compute, emit)
        ocnt[p] = n_sub

    @pl.when(step == n_pass * n_j - 1)
    def _():
        drain(0)
        drain(1)
        _zero_tail(obuf, used_ref[0], y_hbm.shape[0] // MOE_BLK, n_j, out_copy)


def _moe_down(act, plan, w_down, b_down):
    n_exp, f, d = w_down.shape
    half = d // 2
    tn = _largest_tile(half, 256)
    n_j = half // tn
    rows_pass = MOE_NSB * MOE_BLK

    def w_map(off):
        def index_map(v, j, ve, vs, vn, used):
            jj = jnp.where(vn[v] > 0, j, n_j - 1)
            return (ve[v], 0, off + jj)
        return index_map

    grid_spec = pltpu.PrefetchScalarGridSpec(
        num_scalar_prefetch=4,
        grid=(plan["n_pass"], n_j),
        in_specs=[
            pl.BlockSpec(memory_space=pl.ANY),
            pl.BlockSpec((None, f, tn), w_map(0)),
            pl.BlockSpec((None, f, tn), w_map(n_j)),
            pl.BlockSpec((None, 1, tn), w_map(0)),
            pl.BlockSpec((None, 1, tn), w_map(n_j)),
        ],
        out_specs=pl.BlockSpec(memory_space=pl.ANY),
        scratch_shapes=[
            pltpu.VMEM((2, rows_pass, f), BF16),
            pltpu.VMEM((2, rows_pass, tn), U32),
            pltpu.SemaphoreType.DMA((2,)),
            pltpu.SemaphoreType.DMA((2,)),
            pltpu.SMEM((2,), I32),
        ],
    )
    b3 = b_down.reshape(n_exp, 1, d)
    return pl.pallas_call(
        functools.partial(_moe_down_kernel, tn=tn),
        grid_spec=grid_spec,
        out_shape=jax.ShapeDtypeStruct((plan["n_rows"], half), U32),
        compiler_params=_params(("arbitrary", "arbitrary")),
        name="moe_down",
    )(plan["ve"], plan["vs"], plan["vn"], plan["used"], act, w_down, w_down, b3, b3)


def _combine_kernel(dest_ref, y_hbm, gate_ref, x_ref, g_ref, o_ref, ybuf, sem):
    i = pl.program_id(0)
    n_i = pl.num_programs(0)
    tm, d = x_ref.shape
    half = d // 2
    slot = i % 2

    def row_copy(row, sl, k, t0, u):
        window = ybuf.at[sl, k, pl.ds(t0, ISSUE_UNROLL), :]
        return pltpu.make_async_copy(y_hbm.at[pl.ds(row, 1), :], window.at[pl.ds(u, 1), :], sem.at[sl])

    def request(ii, sl, q):
        t0 = pl.multiple_of(q * ISSUE_UNROLL, ISSUE_UNROLL)
        for u in range(ISSUE_UNROLL):
            for k in range(TOP_K):
                row_copy(dest_ref[(ii * tm + t0 + u) * TOP_K + k], sl, k, t0, u).start()

    def gather_wait(sl):
        for k in range(TOP_K):
            pltpu.make_async_copy(y_hbm.at[pl.ds(0, tm), :], ybuf.at[sl, k], sem.at[sl]).wait()

    def combine(q, prefetch):
        rows = pl.ds(pl.multiple_of(q * ISSUE_UNROLL, ISSUE_UNROLL), ISSUE_UNROLL)
        acc_lo = x_ref[rows, 0:half]
        acc_hi = x_ref[rows, half:d]
        for k in range(TOP_K):
            lo, hi = _unpack_pair(ybuf[slot, k, rows, :])
            gk = gate_ref[rows, k:k + 1]
            acc_lo = acc_lo + gk * lo
            acc_hi = acc_hi + gk * hi
        if prefetch:
            request(i + 1, 1 - slot, q)
        ms = (jnp.sum(acc_lo * acc_lo, axis=-1, keepdims=True)
              + jnp.sum(acc_hi * acc_hi, axis=-1, keepdims=True)) / d
        r = lax.rsqrt(ms + EPS)
        o_ref[rows, 0:half] = acc_lo * r * g_ref[:, 0:half]
        o_ref[rows, half:d] = acc_hi * r * g_ref[:, half:d]

    n_groups = tm // ISSUE_UNROLL

    @pl.when(i == 0)
    def _():
        lax.fori_loop(0, n_groups, lambda q, c: (request(0, 0, q), c)[1], 0)

    gather_wait(slot)

    @pl.when(i + 1 < n_i)
    def _():
        lax.fori_loop(0, n_groups, lambda q, c: (combine(q, True), c)[1], 0, unroll=2)

    @pl.when(i + 1 == n_i)
    def _():
        lax.fori_loop(0, n_groups, lambda q, c: (combine(q, False), c)[1], 0, unroll=2)


def _combine(y_rows, dest, gate, x1, final_g):
    t_len, d = x1.shape
    half = d // 2
    tm = _largest_tile(t_len, 128, V7X_SUBLANES)
    grid_spec = pltpu.PrefetchScalarGridSpec(
        num_scalar_prefetch=1,
        grid=(t_len // tm,),
        in_specs=[
            pl.BlockSpec(memory_space=pl.ANY),
            pl.BlockSpec((tm, V7X_LANES), lambda i, dest: (i, 0)),
            pl.BlockSpec((tm, d), lambda i, dest: (i, 0)),
            pl.BlockSpec((1, d), lambda i, dest: (0, 0)),
        ],
        out_specs=pl.BlockSpec((tm, d), lambda i, dest: (i, 0)),
        scratch_shapes=[
            pltpu.VMEM((2, TOP_K, tm, half), U32),
            pltpu.SemaphoreType.DMA((2,)),
        ],
    )
    return pl.pallas_call(
        _combine_kernel,
        grid_spec=grid_spec,
        out_shape=jax.ShapeDtypeStruct((t_len, d), F32),
        compiler_params=_params(("arbitrary",)),
        name="combine",
    )(dest, y_rows, gate, x1, final_g.reshape(1, d))


def kernel(x, mix_norm_g, w_in, ssd_conv_w, ssd_conv_b, ssd_dt_bias, ssd_a_log, ssd_d, ssd_norm_g,
           gmlp_v_norm_g, gmlp_w_s, gmlp_b_s, gmlp_out_norm_g, w_out, ffn_norm_g, w_router, b_router,
           w_gate_up, b_gate_up, w_down, b_down, final_norm_g):
    bsz, t_len, d = x.shape
    assert bsz == 1
    depth = w_in.shape[0]
    ssd_w = ssd_norm_g.shape[-1]
    heads = ssd_dt_bias.shape[-1]
    gmlp_w = gmlp_out_norm_g.shape[-1]
    conv_dim = ssd_conv_b.shape[-1]
    assert heads <= V7X_LANES
    c_dt = ssd_w + conv_dim
    c_uv = c_dt + heads
    off_u, off_z, off_xs = 0, 2 * gmlp_w, 2 * gmlp_w + ssd_w

    h = x.reshape(t_len, d)
    out = None
    for layer in range(depth):
        w = w_in[layer]
        w_t = jnp.swapaxes(w, 0, 1)
        w_dt = jnp.pad(w_t[c_dt:c_uv], ((0, V7X_LANES - heads), (0, 0))).astype(BF16)
        p, dt_raw = _inproj(h, mix_norm_g[layer], w_t, w_dt, ssd_conv_w[layer], ssd_conv_b[layer],
                            2 * gmlp_w, ssd_w, c_uv)
        groups = (conv_dim - ssd_w) // (2 * SSD_STATE)
        y_ssd = _ssd(p, dt_raw, ssd_dt_bias[layer], ssd_a_log[layer], ssd_d[layer], ssd_norm_g[layer],
                     groups, off_z, off_xs)
        y_gmlp = _gmlp(p, gmlp_v_norm_g[layer], gmlp_w_s[layer], gmlp_b_s[layer],
                       gmlp_out_norm_g[layer], off_u)
        x1 = _outproj(y_ssd, y_gmlp, w_out[layer].astype(BF16), h)

        xp, idx_pad, gate_pad = _router(x1, ffn_norm_g[layer], w_router[layer], b_router[layer])
        plan = _route_plan(idx_pad[:, :TOP_K], w_router.shape[-1])
        act = _moe_up(xp, plan, w_gate_up[layer], b_gate_up[layer])
        y_rows = _moe_down(act, plan, w_down[layer], b_down[layer])
        last = layer == depth - 1
        assert last, "multi-layer stacking needs an un-normalised combine"
        out = _combine(y_rows, plan["dest"], gate_pad, x1, final_norm_g)
    return out.reshape(bsz, t_len, d)
```

```python
import functools
import math

import jax
import jax.numpy as jnp
from jax import lax
from jax.experimental import pallas as pl
from jax.experimental.pallas import tpu as pltpu

F32, BF16, U32, I32 = jnp.float32, jnp.bfloat16, jnp.uint32, jnp.int32

EPS = 1e-5
SSD_STATE = 128
SSD_CONV = 4
CHUNK = 128
TOP_K = 4
SSD_ROWS = 1024
SWIGLU_ALPHA = 1.702
SWIGLU_LIMIT = 7.0

V7X_LANES = 128
V7X_SUBLANES = 8
V7X_VMEM_BYTES = 64 * 1024 * 1024
VMEM_LIMIT = V7X_VMEM_BYTES - 8 * 1024 * 1024

MOE_BLK = 128
MOE_NSB = 12
ISSUE_UNROLL = 8
HI_MASK = 0xFFFF0000


_gcd = math.gcd


def _largest_tile(total, cap, quantum=V7X_LANES):
    best = None
    t = quantum
    while t <= min(total, cap):
        if total % t == 0:
            best = t
        t += quantum
    assert best is not None, (total, cap, quantum)
    return best


def _params(semantics):
    return pltpu.CompilerParams(dimension_semantics=semantics, vmem_limit_bytes=VMEM_LIMIT)


def _silu(x):
    return x * jax.nn.sigmoid(x)


def _gelu(x):
    return 0.5 * x * (1.0 + lax.erf(x * (2.0 ** -0.5)))


def _softplus(x):
    return jnp.maximum(x, 0.0) + jnp.log1p(jnp.exp(-jnp.abs(x)))


def _pack_pair(lo, hi):
    lo_bits = lax.bitcast_convert_type(lo.astype(BF16).astype(F32), U32)
    hi_bits = lax.bitcast_convert_type(hi.astype(BF16).astype(F32), U32)
    return (lo_bits >> 16) | (hi_bits & jnp.uint32(HI_MASK))


def _unpack_pair(u):
    lo = lax.bitcast_convert_type(u << 16, F32)
    hi = lax.bitcast_convert_type(u & jnp.uint32(HI_MASK), F32)
    return lo, hi


NT_DIMS = (((1,), (1,)), ((), ()))


def _inproj_kernel(x_hbm, g_ref, w_ref, wdt_ref, cw_ref, cb_ref, p_ref, dt_ref, xbuf, xn_ref, halo_ref, sem,
                   *, n_gelu, n_silu):
    i = pl.program_id(0)
    j = pl.program_id(1)
    tm = xbuf.shape[0]
    rows = _largest_tile(tm, 128, V7X_SUBLANES)
    tail = halo_ref.shape[1]

    def x_copy(ii):
        return pltpu.make_async_copy(x_hbm.at[pl.ds(pl.multiple_of(ii * tm, tm), tm), :], xbuf, sem.at[0])

    @pl.when(j == 0)
    def _():
        @pl.when(i == 0)
        def _():
            x_copy(0).start()
            halo_ref[...] = jnp.zeros(halo_ref.shape, F32)

        x_copy(i).wait()

        def norm(r, carry):
            sl = pl.ds(pl.multiple_of(r * rows, rows), rows)
            x = xbuf[sl, :]
            ms = jnp.mean(x * x, axis=-1, keepdims=True)
            xn = (x * lax.rsqrt(ms + EPS) * g_ref[...]).astype(BF16)
            xn_ref[sl, :] = xn
            dt_ref[sl, :] = lax.dot_general(xn, wdt_ref[...], NT_DIMS, preferred_element_type=F32)
            return carry

        lax.fori_loop(0, tm // rows, norm, 0)

        @pl.when(i + 1 < pl.num_programs(0))
        def _():
            x_copy(i + 1).start()

    def project():
        return lax.dot_general(xn_ref[...], w_ref[...].astype(BF16), NT_DIMS, preferred_element_type=F32)

    @pl.when(j < n_gelu)
    def _():
        p_ref[...] = _gelu(project()).astype(BF16)

    @pl.when((j >= n_gelu) & (j < n_gelu + n_silu))
    def _():
        p_ref[...] = _silu(project()).astype(BF16)

    @pl.when(j >= n_gelu + n_silu)
    def _():
        x = project()
        jc = j - (n_gelu + n_silu)
        ext = jnp.concatenate([halo_ref[jc], x], axis=0)
        acc = cb_ref[...] + cw_ref[SSD_CONV - 1:SSD_CONV, :] * ext
        for s in range(1, SSD_CONV):
            acc = acc + cw_ref[SSD_CONV - 1 - s:SSD_CONV - s, :] * pltpu.roll(ext, s, axis=0)
        p_ref[...] = _silu(acc[tail:, :]).astype(BF16)
        halo_ref[jc] = x[tm - tail:tm, :]


def _inproj(x, g, w_t, w_dt, conv_w, conv_b, n_gelu_cols, n_silu_cols, row_gelu):
    t_len, d = x.shape
    n_conv_cols = conv_b.shape[0]
    n_out = n_gelu_cols + n_silu_cols + n_conv_cols
    assert row_gelu + n_gelu_cols == w_t.shape[0] and row_gelu >= n_silu_cols + n_conv_cols
    tm = _largest_tile(t_len, 1024, V7X_SUBLANES)
    tn = _largest_tile(_gcd(_gcd(n_gelu_cols, n_silu_cols), n_conv_cols), 512)
    n_gelu, n_silu, n_conv = n_gelu_cols // tn, n_silu_cols // tn, n_conv_cols // tn
    tail = 2 * V7X_SUBLANES
    assert row_gelu % tail == 0 and tn % tail == 0
    conv_map = lambda i, j: (0, jnp.maximum(j - (n_gelu + n_silu), 0))
    w_map = lambda i, j: (
        tail * jnp.where(j < n_gelu, row_gelu // tail + j * (tn // tail), (j - n_gelu) * (tn // tail)), 0)
    return pl.pallas_call(
        functools.partial(_inproj_kernel, n_gelu=n_gelu, n_silu=n_silu),
        grid=(t_len // tm, n_out // tn),
        in_specs=[
            pl.BlockSpec(memory_space=pl.ANY),
            pl.BlockSpec((1, d), lambda i, j: (0, 0)),
            pl.BlockSpec((pl.Element(tn), pl.Element(d)), w_map),
            pl.BlockSpec((V7X_LANES, d), lambda i, j: (0, 0)),
            pl.BlockSpec((SSD_CONV, tn), conv_map),
            pl.BlockSpec((1, tn), conv_map),
        ],
        out_specs=[
            pl.BlockSpec((tm, tn), lambda i, j: (i, j)),
            pl.BlockSpec((tm, V7X_LANES), lambda i, j: (i, 0)),
        ],
        out_shape=[
            jax.ShapeDtypeStruct((t_len, n_out), BF16),
            jax.ShapeDtypeStruct((t_len, V7X_LANES), F32),
        ],
        scratch_shapes=[
            pltpu.VMEM((tm, d), F32),
            pltpu.VMEM((tm, d), BF16),
            pltpu.VMEM((n_conv, tail, tn), F32),
            pltpu.SemaphoreType.DMA((1,)),
        ],
        compiler_params=_params(("arbitrary", "arbitrary")),
        name="inproj",
    )(x, g.reshape(1, d), w_t, w_dt, conv_w, conv_b.reshape(1, n_conv_cols))


def _split3(x):
    hi = x.astype(BF16)
    r1 = x - hi.astype(F32)
    mid = r1.astype(BF16)
    lo = (r1 - mid.astype(F32)).astype(BF16)
    return hi, mid, lo


def _ssd_kernel(sz_ref, xs_ref, b_ref, c_ref, dt_ref, dtb_ref, alog_ref, dsk_ref, ng_ref, y_ref, state_ref,
                *, hpg, hd):
    width = hpg * hd
    lanes = V7X_LANES
    per = lanes // hd

    @pl.when(pl.program_id(1) == 0)
    def _():
        state_ref[...] = jnp.zeros(state_ref.shape, F32)

    ri = lax.broadcasted_iota(I32, (CHUNK, CHUNK), 0)
    ci = lax.broadcasted_iota(I32, (CHUNK, CHUNK), 1)
    causal = ri >= ci
    tril = causal.astype(BF16)
    neg_a = -jnp.exp(alog_ref[...])

    def expand(q):
        cols = []
        for j in range(width // lanes):
            blk = jnp.broadcast_to(q[:, j * per:j * per + 1], (CHUNK, lanes))
            for t in range(1, per):
                nxt = jnp.broadcast_to(q[:, j * per + t:j * per + t + 1], (CHUNK, lanes))
                blk = jnp.where(ci >= t * hd, nxt, blk)
            cols.append(blk)
        return jnp.concatenate(cols, axis=1)

    def chunk(k, carry):
        rows = pl.ds(pl.multiple_of(k * CHUNK, CHUNK), CHUNK)
        xs = xs_ref[rows, :].astype(F32)
        bm = b_ref[rows, :]
        cm = c_ref[rows, :]

        dt = _softplus(dt_ref[rows, :] + dtb_ref[...])
        a_dt = dt * neg_a
        acum = sum(jnp.dot(tril, piece, preferred_element_type=F32) for piece in _split3(a_dt))
        acum_t = acum.T
        acum_f = expand(acum)
        last = acum_f[CHUNK - 1:CHUNK, :]
        xdt = xs * expand(dt)

        cb = lax.dot_general(cm, bm, (((1,), (1,)), ((), ())), preferred_element_type=F32)
        y_cols = []
        for j in range(width // lanes):
            x_col = xdt[:, j * lanes:(j + 1) * lanes]
            y_col = jnp.zeros((CHUNK, lanes), F32)
            for t in range(per):
                h = j * per + t
                seg = acum[:, h:h + 1] - acum_t[h:h + 1, :]
                lmat = jnp.exp(jnp.where(causal, seg, -jnp.inf))
                m_h = (cb * lmat).astype(BF16)
                mask = (ci >= t * hd) & (ci < (t + 1) * hd)
                x_h = jnp.where(mask, x_col, 0.0).astype(BF16)
                y_col = y_col + jnp.dot(m_h, x_h, preferred_element_type=F32)
            y_cols.append(y_col)
        y_diag = jnp.concatenate(y_cols, axis=1)

        state = state_ref[...]
        y_off = jnp.dot(cm, state.astype(BF16), preferred_element_type=F32) * jnp.exp(acum_f)
        x_dec = (xdt * jnp.exp(last - acum_f)).astype(BF16)
        s_new = lax.dot_general(bm, x_dec, (((0,), (0,)), ((), ())), preferred_element_type=F32)
        state_ref[...] = state * jnp.exp(last) + s_new

        yg = (y_diag + y_off + xs * dsk_ref[...]) * sz_ref[rows, :].astype(F32)
        yn = yg * lax.rsqrt(jnp.mean(yg * yg, axis=-1, keepdims=True) + EPS) * ng_ref[...]
        y_ref[rows, :] = yn.astype(BF16)
        return carry

    lax.fori_loop(0, y_ref.shape[0] // CHUNK, chunk, 0, unroll=2)


def _ssd(p, dt_raw, dt_bias, a_log, d_skip, norm_g, groups, off_z, off_xs):
    t_len = p.shape[0]
    heads = dt_bias.shape[0]
    width = norm_g.shape[0]
    hd = width // heads
    hpg = heads // groups
    gw = hpg * hd
    n_st = SSD_STATE
    rows = _largest_tile(t_len, SSD_ROWS, CHUNK)
    assert V7X_LANES % hd == 0 and gw % V7X_LANES == 0 and hpg <= V7X_LANES
    assert width // groups == gw
    off_b = off_xs + width
    off_c = off_b + groups * n_st
    assert off_z % gw == 0 and off_xs % gw == 0 and off_b % n_st == 0

    def per_group_lanes(v):
        v = v.reshape(groups, 1, hpg)
        return jnp.pad(v, ((0, 0), (0, 0), (0, V7X_LANES - hpg)))

    dt_g = dt_raw[:, :heads].reshape(t_len, groups, hpg).transpose(1, 0, 2)
    dt_g = jnp.pad(dt_g, ((0, 0), (0, 0), (0, V7X_LANES - hpg)))

    grp = lambda g, c: (g, 0, 0)
    return pl.pallas_call(
        functools.partial(_ssd_kernel, hpg=hpg, hd=hd),
        grid=(groups, t_len // rows),
        in_specs=[
            pl.BlockSpec((rows, gw), lambda g, c: (c, off_z // gw + g)),
            pl.BlockSpec((rows, gw), lambda g, c: (c, off_xs // gw + g)),
            pl.BlockSpec((rows, n_st), lambda g, c: (c, off_b // n_st + g)),
            pl.BlockSpec((rows, n_st), lambda g, c: (c, off_c // n_st + g)),
            pl.BlockSpec((None, rows, V7X_LANES), lambda g, c: (g, c, 0)),
            pl.BlockSpec((None, 1, V7X_LANES), grp),
            pl.BlockSpec((None, 1, V7X_LANES), grp),
            pl.BlockSpec((1, gw), lambda g, c: (0, g)),
            pl.BlockSpec((1, gw), lambda g, c: (0, g)),
        ],
        out_specs=pl.BlockSpec((rows, gw), lambda g, c: (c, g)),
        out_shape=jax.ShapeDtypeStruct((t_len, width), BF16),
        scratch_shapes=[pltpu.VMEM((n_st, gw), F32)],
        compiler_params=_params(("arbitrary", "arbitrary")),
        name="ssd",
    )(p, p, p, p, dt_g, per_group_lanes(dt_bias), per_group_lanes(a_log),
      jnp.repeat(d_skip, hd).reshape(1, width), norm_g.reshape(1, width))


def _row_sums(x):
    ones = jnp.ones((x.shape[1], V7X_LANES), BF16)
    hi = x.astype(BF16)
    lo = (x - hi.astype(F32)).astype(BF16)
    return (jnp.dot(hi, ones, preferred_element_type=F32) + jnp.dot(lo, ones, preferred_element_type=F32))


def _gmlp_kernel(u_ref, v_ref, vg_ref, ws_ref, bs_ref, og_ref, y_ref, yscr_ref, *, heads, hd):
    lanes = V7X_LANES
    reps = hd // lanes
    ri = lax.broadcasted_iota(I32, (CHUNK, CHUNK), 0)
    ci = lax.broadcasted_iota(I32, (CHUNK, CHUNK), 1)
    causal = ri >= ci
    ssq = jnp.zeros((CHUNK, lanes), F32)
    invs = []
    for h in range(heads):
        v_h = v_ref[:, h * hd:(h + 1) * hd].astype(F32)
        invs.append(lax.rsqrt(_row_sums(v_h * v_h) / hd + EPS))
    for h in range(heads):
        cols = slice(h * hd, (h + 1) * hd)
        u_h = u_ref[:, cols].astype(F32)
        v_h = v_ref[:, cols].astype(F32)
        v_n = v_h * jnp.tile(invs[h], (1, reps)) * vg_ref[:, cols]
        w_h = jnp.where(causal, ws_ref[h], 0.0).astype(BF16)
        sv = jnp.dot(w_h, v_n.astype(BF16), preferred_element_type=F32) + jnp.tile(bs_ref[h], (1, reps))
        y_h = u_h * sv
        yscr_ref[:, cols] = y_h
        y_sq = y_h * y_h
        for r in range(reps):
            ssq = ssq + y_sq[:, r * lanes:(r + 1) * lanes]
    scale = lax.rsqrt(_row_sums(ssq) / (heads * hd) + EPS)
    y_ref[...] = (yscr_ref[...] * jnp.tile(scale, (1, heads * reps)) * og_ref[...]).astype(BF16)


def _gmlp(p, v_norm_g, w_s, b_s, out_norm_g, off_u):
    t_len = p.shape[0]
    heads = w_s.shape[0]
    width = out_norm_g.shape[0]
    hd = width // heads
    assert w_s.shape[1] == CHUNK and off_u % width == 0 and hd % V7X_LANES == 0
    b_lanes = jnp.broadcast_to(b_s[:, :, None], (heads, CHUNK, V7X_LANES))
    return pl.pallas_call(
        functools.partial(_gmlp_kernel, heads=heads, hd=hd),
        grid=(t_len // CHUNK,),
        in_specs=[
            pl.BlockSpec((CHUNK, width), lambda c: (c, off_u // width)),
            pl.BlockSpec((CHUNK, width), lambda c: (c, off_u // width + 1)),
            pl.BlockSpec((1, width), lambda c: (0, 0)),
            pl.BlockSpec((heads, CHUNK, CHUNK), lambda c: (0, 0, 0)),
            pl.BlockSpec((heads, CHUNK, V7X_LANES), lambda c: (0, 0, 0)),
            pl.BlockSpec((1, width), lambda c: (0, 0)),
        ],
        out_specs=pl.BlockSpec((CHUNK, width), lambda c: (c, 0)),
        out_shape=jax.ShapeDtypeStruct((t_len, width), BF16),
        scratch_shapes=[pltpu.VMEM((CHUNK, width), F32)],
        compiler_params=_params(("arbitrary",)),
        name="gmlp",
    )(p, p, v_norm_g.reshape(1, width), w_s, b_lanes, out_norm_g.reshape(1, width))


def _outproj_kernel(ys_ref, yg_ref, wa_ref, wb_ref, x_ref, o_ref):
    acc = jnp.dot(ys_ref[...], wa_ref[...], preferred_element_type=F32)
    acc = acc + jnp.dot(yg_ref[...], wb_ref[...], preferred_element_type=F32)
    o_ref[...] = x_ref[...] + acc


def _outproj(y_ssd, y_gmlp, w_out, x):
    t_len, d = x.shape
    ka, kb = y_ssd.shape[1], y_gmlp.shape[1]
    assert ka == kb and w_out.shape[0] == ka + kb
    tm = _largest_tile(t_len, 512, V7X_SUBLANES)
    tn = _largest_tile(d, 512)
    return pl.pallas_call(
        _outproj_kernel,
        grid=(t_len // tm, d // tn),
        in_specs=[
            pl.BlockSpec((tm, ka), lambda i, j: (i, 0)),
            pl.BlockSpec((tm, kb), lambda i, j: (i, 0)),
            pl.BlockSpec((ka, tn), lambda i, j: (0, j)),
            pl.BlockSpec((kb, tn), lambda i, j: (1, j)),
            pl.BlockSpec((tm, tn), lambda i, j: (i, j)),
        ],
        out_specs=pl.BlockSpec((tm, tn), lambda i, j: (i, j)),
        out_shape=jax.ShapeDtypeStruct((t_len, d), F32),
        compiler_params=_params(("arbitrary", "arbitrary")),
        name="outproj",
    )(y_ssd, y_gmlp, w_out, w_out, x)


def _router_kernel(x_ref, g_ref, wr_ref, br_ref, xp_ref, idx_ref, gate_ref, *, n_exp):
    x = x_ref[...]
    tm, d = x.shape
    half = d // 2
    ms = jnp.mean(x * x, axis=-1, keepdims=True)
    xn = x * lax.rsqrt(ms + EPS) * g_ref[...]
    xp_ref[...] = _pack_pair(xn[:, :half], xn[:, half:])

    logits = jnp.dot(xn.astype(BF16), wr_ref[...], preferred_element_type=F32) + br_ref[...]
    lane = lax.broadcasted_iota(I32, (tm, V7X_LANES), 1)
    vals = jnp.where(lane < n_exp, logits, -jnp.inf)
    idx_out = jnp.zeros((tm, V7X_LANES), I32)
    val_out = jnp.full((tm, V7X_LANES), -jnp.inf, F32)
    for k in range(TOP_K):
        m = jnp.max(vals, axis=-1, keepdims=True)
        sel = jnp.min(jnp.where(vals == m, lane, V7X_LANES), axis=-1, keepdims=True)
        idx_out = jnp.where(lane == k, sel, idx_out)
        val_out = jnp.where(lane == k, m, val_out)
        vals = jnp.where(lane == sel, -jnp.inf, vals)
    e = jnp.exp(val_out - jnp.max(val_out, axis=-1, keepdims=True))
    idx_ref[...] = idx_out
    gate_ref[...] = e / jnp.sum(e, axis=-1, keepdims=True)


def _router(x1, g, w_router, b_router):
    t_len, d = x1.shape
    n_exp = w_router.shape[1]
    assert n_exp <= V7X_LANES and TOP_K <= n_exp
    tm = _largest_tile(t_len, 256, V7X_SUBLANES)
    wr = jnp.pad(w_router, ((0, 0), (0, V7X_LANES - n_exp))).astype(BF16)
    br = jnp.pad(b_router, (0, V7X_LANES - n_exp)).reshape(1, V7X_LANES)
    return pl.pallas_call(
        functools.partial(_router_kernel, n_exp=n_exp),
        grid=(t_len // tm,),
        in_specs=[
            pl.BlockSpec((tm, d), lambda i: (i, 0)),
            pl.BlockSpec((1, d), lambda i: (0, 0)),
            pl.BlockSpec((d, V7X_LANES), lambda i: (0, 0)),
            pl.BlockSpec((1, V7X_LANES), lambda i: (0, 0)),
        ],
        out_specs=[
            pl.BlockSpec((tm, d // 2), lambda i: (i, 0)),
            pl.BlockSpec((tm, V7X_LANES), lambda i: (i, 0)),
            pl.BlockSpec((tm, V7X_LANES), lambda i: (i, 0)),
        ],
        out_shape=[
            jax.ShapeDtypeStruct((t_len, d // 2), U32),
            jax.ShapeDtypeStruct((t_len, V7X_LANES), I32),
            jax.ShapeDtypeStruct((t_len, V7X_LANES), F32),
        ],
        compiler_params=_params(("arbitrary",)),
        name="router",
    )(x1, g.reshape(1, d), wr, br)


def _route_plan(top_idx, n_exp):
    t_len, top_k = top_idx.shape
    n_assign = t_len * top_k
    e = top_idx.reshape(n_assign)
    onehot = (e[:, None] == jnp.arange(n_exp, dtype=I32)[None, :]).astype(I32)
    csum = jnp.cumsum(onehot, axis=0)
    rank = jnp.sum(onehot * csum, axis=1) - 1
    counts = csum[-1]
    nb = (counts + MOE_BLK - 1) // MOE_BLK
    sb_end = jnp.cumsum(nb)
    sb_start = sb_end - nb
    dest = sb_start[e] * MOE_BLK + rank
    n_sb = n_assign // MOE_BLK + n_exp
    n_rows = n_sb * MOE_BLK
    row_tok = jnp.zeros((n_rows,), I32).at[dest].set(jnp.arange(n_assign, dtype=I32) // top_k)

    nv = (nb + MOE_NSB - 1) // MOE_NSB
    v_end = jnp.cumsum(nv)
    v_start = v_end - nv
    n_pass = -(-n_sb // MOE_NSB) + n_exp
    vid = jnp.arange(n_pass, dtype=I32)
    ve = jnp.minimum(jnp.searchsorted(v_end, vid, side='right'), n_exp - 1).astype(I32)
    q = vid - v_start[ve]
    valid = vid < v_end[-1]
    vs = jnp.where(valid, sb_start[ve] + q * MOE_NSB, 0).astype(I32)
    vn = jnp.where(valid, jnp.clip(nb[ve] - q * MOE_NSB, 0, MOE_NSB), 0).astype(I32)
    last_e = ve[jnp.maximum(v_end[-1] - 1, 0)]
    ve = jnp.where(valid, ve, last_e).astype(I32)
    used = sb_end[-1:].astype(I32)
    n_live = jnp.maximum(v_end[-1], 1).astype(I32)
    return dict(row_tok=row_tok, dest=dest.astype(I32), ve=ve, vs=vs, vn=vn, used=used, n_rows=n_rows,
                n_pass=n_live)


def _for_each_block(n, compute, emit):
    def quad(q, carry):
        b = 4 * q
        first = compute(b, 2)
        second = compute(b + 2, 2)
        emit(b, 2, first)
        emit(b + 2, 2, second)
        return carry

    quads = lax.shift_right_logical(n, 2)
    lax.fori_loop(0, quads, quad, 0)

    @pl.when((n & 2) != 0)
    def _():
        emit(4 * quads, 2, compute(4 * quads, 2))

    @pl.when((n & 1) != 0)
    def _():
        emit(n - 1, 1, compute(n - 1, 1))


def _zero_tail(obuf, n_used, n_sb, n_j, out_copy):
    tile = obuf.shape[2]
    obuf[0, 0:MOE_BLK, :] = jnp.zeros((MOE_BLK, tile), obuf.dtype)

    def start(sb, carry):
        for jj in range(n_j):
            out_copy(0, 0, pl.multiple_of(sb * MOE_BLK, MOE_BLK), jj * tile).start()
        return carry

    def wait(sb, carry):
        for jj in range(n_j):
            out_copy(0, 0, 0, 0).wait()
        return carry

    lax.fori_loop(n_used, n_sb, start, 0)
    lax.fori_loop(n_used, n_sb, wait, 0)


def _moe_up_kernel(ve_ref, vs_ref, vn_ref, used_ref, tok_ref, xp_hbm, wg_ref, wl_ref, bg_ref, bl_ref, act_hbm,
                   xbuf, obuf, xsem, osem, ocnt, *, tf, n_j):
    v = pl.program_id(0)
    j = pl.program_id(1)
    n_pass = pl.num_programs(0)
    half = xbuf.shape[2]
    slot = v % 2
    n_sub = vn_ref[v]

    def row_copy(tok, sl, r0, u):
        window = xbuf.at[sl, pl.ds(r0, ISSUE_UNROLL), :]
        return pltpu.make_async_copy(xp_hbm.at[pl.ds(tok, 1), :], window.at[pl.ds(u, 1), :], xsem.at[sl])

    def gather_start(vv, sl):
        base = vs_ref[vv] * MOE_BLK

        def body(q, carry):
            r0 = pl.multiple_of(q * ISSUE_UNROLL, ISSUE_UNROLL)
            for u in range(ISSUE_UNROLL):
                row_copy(tok_ref[base + r0 + u], sl, r0, u).start()
            return carry

        lax.fori_loop(0, vn_ref[vv] * (MOE_BLK // ISSUE_UNROLL), body, 0)

    def gather_wait(vv, sl):
        def body(i, carry):
            r0 = pl.multiple_of(i * MOE_BLK, MOE_BLK)
            pltpu.make_async_copy(xp_hbm.at[pl.ds(0, MOE_BLK), :], xbuf.at[sl, pl.ds(r0, MOE_BLK), :],
                                  xsem.at[sl]).wait()
            return carry

        lax.fori_loop(0, vn_ref[vv], body, 0)

    def out_copy(p, r0, row, col):
        return pltpu.make_async_copy(obuf.at[p, pl.ds(r0, MOE_BLK), :],
                                     act_hbm.at[pl.ds(row, MOE_BLK), pl.ds(col, tf)], osem.at[p])

    def drain(p):
        def body(i, carry):
            out_copy(p, 0, 0, 0).wait()
            return carry

        lax.fori_loop(0, ocnt[p], body, 0)
        ocnt[p] = 0

    @pl.when(j == 0)
    def _():
        @pl.when(v == 0)
        def _():
            ocnt[0] = 0
            ocnt[1] = 0
            gather_start(0, 0)

        gather_wait(v, slot)

    step = v * n_j + j
    p = step % 2
    drain(p)

    @pl.when(n_sub > 0)
    def _():
        nxt = jnp.minimum(v + 1, n_pass - 1)
        n_next = jnp.where(v + 1 < n_pass, vn_ref[nxt], 0)
        groups = n_next * (MOE_BLK // (ISSUE_UNROLL * n_j))
        first = j * groups
        base = vs_ref[nxt] * MOE_BLK

        def request(q, carry):
            r0 = pl.multiple_of((first + q) * ISSUE_UNROLL, ISSUE_UNROLL)
            for u in range(ISSUE_UNROLL):
                row_copy(tok_ref[base + r0 + u], 1 - slot, r0, u).start()
            return carry

        lax.fori_loop(0, groups, request, 0)

        def compute(b, nb):
            r0 = pl.multiple_of(b * MOE_BLK, MOE_BLK)
            lo, hi = _unpack_pair(xbuf[slot, pl.ds(r0, nb * MOE_BLK), :])
            w_lo = jnp.concatenate([wg_ref[0:half, :], wl_ref[0:half, :]], axis=1).astype(BF16)
            w_hi = jnp.concatenate([wg_ref[half:2 * half, :], wl_ref[half:2 * half, :]], axis=1).astype(BF16)
            h = jnp.dot(lo.astype(BF16), w_lo, preferred_element_type=F32)
            h = h + jnp.dot(hi.astype(BF16), w_hi, preferred_element_type=F32)
            glu = jnp.minimum(h[:, 0:tf] + bg_ref[...], SWIGLU_LIMIT)
            lin = jnp.clip(h[:, tf:2 * tf] + bl_ref[...], -SWIGLU_LIMIT, SWIGLU_LIMIT)
            return (glu * jax.nn.sigmoid(SWIGLU_ALPHA * glu) * (lin + 1.0)).astype(BF16)

        def emit(b, nb, act):
            r0 = pl.multiple_of(b * MOE_BLK, MOE_BLK)
            obuf[p, pl.ds(r0, nb * MOE_BLK), :] = act
            for t in range(nb):
                row = pl.multiple_of((vs_ref[v] + b + t) * MOE_BLK, MOE_BLK)
                out_copy(p, r0 + t * MOE_BLK, row, pl.multiple_of(j * tf, tf)).start()

        _for_each_block(n_sub, compute, emit)
        ocnt[p] = n_sub

    @pl.when(step == n_pass * n_j - 1)
    def _():
        drain(0)
        drain(1)
        _zero_tail(obuf, used_ref[0], act_hbm.shape[0] // MOE_BLK, n_j, out_copy)


def _moe_up(xp, plan, w_gate_up, b_gate_up):
    n_exp, d, two_f = w_gate_up.shape
    f = two_f // 2
    half = xp.shape[1]
    assert half * 2 == d
    tf = _largest_tile(f, 256)
    n_j = f // tf
    rows_pass = MOE_NSB * MOE_BLK
    assert MOE_BLK % (ISSUE_UNROLL * n_j) == 0

    def w_map(off):
        def index_map(v, j, ve, vs, vn, used, tok):
            jj = jnp.where(vn[v] > 0, j, n_j - 1)
            return (ve[v], 0, off + jj)
        return index_map

    grid_spec = pltpu.PrefetchScalarGridSpec(
        num_scalar_prefetch=5,
        grid=(plan["n_pass"], n_j),
        in_specs=[
            pl.BlockSpec(memory_space=pl.ANY),
            pl.BlockSpec((None, d, tf), w_map(0)),
            pl.BlockSpec((None, d, tf), w_map(n_j)),
            pl.BlockSpec((None, 1, tf), w_map(0)),
            pl.BlockSpec((None, 1, tf), w_map(n_j)),
        ],
        out_specs=pl.BlockSpec(memory_space=pl.ANY),
        scratch_shapes=[
            pltpu.VMEM((2, rows_pass, half), U32),
            pltpu.VMEM((2, rows_pass, tf), BF16),
            pltpu.SemaphoreType.DMA((2,)),
            pltpu.SemaphoreType.DMA((2,)),
            pltpu.SMEM((2,), I32),
        ],
    )
    b3 = b_gate_up.reshape(n_exp, 1, two_f)
    return pl.pallas_call(
        functools.partial(_moe_up_kernel, tf=tf, n_j=n_j),
        grid_spec=grid_spec,
        out_shape=jax.ShapeDtypeStruct((plan["n_rows"], f), BF16),
        compiler_params=_params(("arbitrary", "arbitrary")),
        name="moe_up",
    )(plan["ve"], plan["vs"], plan["vn"], plan["used"], plan["row_tok"], xp, w_gate_up, w_gate_up, b3, b3)


def _moe_down_kernel(ve_ref, vs_ref, vn_ref, used_ref, act_hbm, wa_ref, wb_ref, ba_ref, bb_ref, y_hbm,
                     abuf, obuf, asem, osem, ocnt, *, tn):
    v = pl.program_id(0)
    j = pl.program_id(1)
    n_pass = pl.num_programs(0)
    n_j = pl.num_programs(1)
    slot = v % 2
    n_sub = vn_ref[v]

    def in_copy(row, sl, r0):
        return pltpu.make_async_copy(act_hbm.at[pl.ds(row, MOE_BLK), :],
                                     abuf.at[sl, pl.ds(r0, MOE_BLK), :], asem.at[sl])

    def load_start(vv, sl):
        def body(i, carry):
            r0 = pl.multiple_of(i * MOE_BLK, MOE_BLK)
            in_copy(pl.multiple_of((vs_ref[vv] + i) * MOE_BLK, MOE_BLK), sl, r0).start()
            return carry

        lax.fori_loop(0, vn_ref[vv], body, 0)

    def load_wait(vv, sl):
        def body(i, carry):
            in_copy(0, sl, pl.multiple_of(i * MOE_BLK, MOE_BLK)).wait()
            return carry

        lax.fori_loop(0, vn_ref[vv], body, 0)

    def out_copy(p, r0, row, col):
        return pltpu.make_async_copy(obuf.at[p, pl.ds(r0, MOE_BLK), :],
                                     y_hbm.at[pl.ds(row, MOE_BLK), pl.ds(col, tn)], osem.at[p])

    def drain(p):
        def body(i, carry):
            out_copy(p, 0, 0, 0).wait()
            return carry

        lax.fori_loop(0, ocnt[p], body, 0)
        ocnt[p] = 0

    @pl.when(j == 0)
    def _():
        @pl.when(v == 0)
        def _():
            ocnt[0] = 0
            ocnt[1] = 0
            load_start(0, 0)

        @pl.when(v + 1 < n_pass)
        def _():
            load_start(v + 1, 1 - slot)

        load_wait(v, slot)

    step = v * n_j + j
    p = step % 2
    drain(p)

    @pl.when(n_sub > 0)
    def _():
        def compute(b, nb):
            r0 = pl.multiple_of(b * MOE_BLK, MOE_BLK)
            w = jnp.concatenate([wa_ref[...], wb_ref[...]], axis=1).astype(BF16)
            y = jnp.dot(abuf[slot, pl.ds(r0, nb * MOE_BLK), :], w, preferred_element_type=F32)
            return _pack_pair(y[:, 0:tn] + ba_ref[...], y[:, tn:2 * tn] + bb_ref[...])

        def emit(b, nb, packed):
            r0 = pl.multiple_of(b * MOE_BLK, MOE_BLK)
            obuf[p, pl.ds(r0, nb * MOE_BLK), :] = packed
            for t in range(nb):
                row = pl.multiple_of((vs_ref[v] + b + t) * MOE_BLK, MOE_BLK)
                out_copy(p, r0 + t * MOE_BLK, row, pl.multiple_of(j * tn, tn)).start()

        _for_each_block(n_sub, compute, emit)
        ocnt[p] = n_sub

    @pl.when(step == n_pass * n_j - 1)
    def _():
        drain(0)
        drain(1)
        _zero_tail(obuf, used_ref[0], y_hbm.shape[0] // MOE_BLK, n_j, out_copy)


def _moe_down(act, plan, w_down, b_down):
    n_exp, f, d = w_down.shape
    half = d // 2
    tn = _largest_tile(half, 256)
    n_j = half // tn
    rows_pass = MOE_NSB * MOE_BLK

    def w_map(off):
        def index_map(v, j, ve, vs, vn, used):
            jj = jnp.where(vn[v] > 0, j, n_j - 1)
            return (ve[v], 0, off + jj)
        return index_map

    grid_spec = pltpu.PrefetchScalarGridSpec(
        num_scalar_prefetch=4,
        grid=(plan["n_pass"], n_j),
        in_specs=[
            pl.BlockSpec(memory_space=pl.ANY),
            pl.BlockSpec((None, f, tn), w_map(0)),
            pl.BlockSpec((None, f, tn), w_map(n_j)),
            pl.BlockSpec((None, 1, tn), w_map(0)),
            pl.BlockSpec((None, 1, tn), w_map(n_j)),
        ],
        out_specs=pl.BlockSpec(memory_space=pl.ANY),
        scratch_shapes=[
            pltpu.VMEM((2, rows_pass, f), BF16),
            pltpu.VMEM((2, rows_pass, tn), U32),
            pltpu.SemaphoreType.DMA((2,)),
            pltpu.SemaphoreType.DMA((2,)),
            pltpu.SMEM((2,), I32),
        ],
    )
    b3 = b_down.reshape(n_exp, 1, d)
    return pl.pallas_call(
        functools.partial(_moe_down_kernel, tn=tn),
        grid_spec=grid_spec,
        out_shape=jax.ShapeDtypeStruct((plan["n_rows"], half), U32),
        compiler_params=_params(("arbitrary", "arbitrary")),
        name="moe_down",
    )(plan["ve"], plan["vs"], plan["vn"], plan["used"], act, w_down, w_down, b3, b3)


def _combine_kernel(dest_ref, y_hbm, gate_ref, x_ref, g_ref, o_ref, ybuf, sem):
    i = pl.program_id(0)
    n_i = pl.num_programs(0)
    tm, d = x_ref.shape
    half = d // 2
    slot = i % 2

    def row_copy(row, sl, k, t0, u):
        window = ybuf.at[sl, k, pl.ds(t0, ISSUE_UNROLL), :]
        return pltpu.make_async_copy(y_hbm.at[pl.ds(row, 1), :], window.at[pl.ds(u, 1), :], sem.at[sl])

    def request(ii, sl, q):
        t0 = pl.multiple_of(q * ISSUE_UNROLL, ISSUE_UNROLL)
        for u in range(ISSUE_UNROLL):
            for k in range(TOP_K):
                row_copy(dest_ref[(ii * tm + t0 + u) * TOP_K + k], sl, k, t0, u).start()

    def gather_wait(sl):
        for k in range(TOP_K):
            pltpu.make_async_copy(y_hbm.at[pl.ds(0, tm), :], ybuf.at[sl, k], sem.at[sl]).wait()

    def combine(q, prefetch):
        rows = pl.ds(pl.multiple_of(q * ISSUE_UNROLL, ISSUE_UNROLL), ISSUE_UNROLL)
        acc_lo = x_ref[rows, 0:half]
        acc_hi = x_ref[rows, half:d]
        for k in range(TOP_K):
            lo, hi = _unpack_pair(ybuf[slot, k, rows, :])
            gk = gate_ref[rows, k:k + 1]
            acc_lo = acc_lo + gk * lo
            acc_hi = acc_hi + gk * hi
        if prefetch:
            request(i + 1, 1 - slot, q)
        ms = (jnp.sum(acc_lo * acc_lo, axis=-1, keepdims=True)
              + jnp.sum(acc_hi * acc_hi, axis=-1, keepdims=True)) / d
        r = lax.rsqrt(ms + EPS)
        o_ref[rows, 0:half] = acc_lo * r * g_ref[:, 0:half]
        o_ref[rows, half:d] = acc_hi * r * g_ref[:, half:d]

    n_groups = tm // ISSUE_UNROLL

    @pl.when(i == 0)
    def _():
        lax.fori_loop(0, n_groups, lambda q, c: (request(0, 0, q), c)[1], 0)

    gather_wait(slot)

    @pl.when(i + 1 < n_i)
    def _():
        lax.fori_loop(0, n_groups, lambda q, c: (combine(q, True), c)[1], 0, unroll=2)

    @pl.when(i + 1 == n_i)
    def _():
        lax.fori_loop(0, n_groups, lambda q, c: (combine(q, False), c)[1], 0, unroll=2)


def _combine(y_rows, dest, gate, x1, final_g):
    t_len, d = x1.shape
    half = d // 2
    tm = _largest_tile(t_len, 128, V7X_SUBLANES)
    grid_spec = pltpu.PrefetchScalarGridSpec(
        num_scalar_prefetch=1,
        grid=(t_len // tm,),
        in_specs=[
            pl.BlockSpec(memory_space=pl.ANY),
            pl.BlockSpec((tm, V7X_LANES), lambda i, dest: (i, 0)),
            pl.BlockSpec((tm, d), lambda i, dest: (i, 0)),
            pl.BlockSpec((1, d), lambda i, dest: (0, 0)),
        ],
        out_specs=pl.BlockSpec((tm, d), lambda i, dest: (i, 0)),
        scratch_shapes=[
            pltpu.VMEM((2, TOP_K, tm, half), U32),
            pltpu.SemaphoreType.DMA((2,)),
        ],
    )
    return pl.pallas_call(
        _combine_kernel,
        grid_spec=grid_spec,
        out_shape=jax.ShapeDtypeStruct((t_len, d), F32),
        compiler_params=_params(("arbitrary",)),
        name="combine",
    )(dest, y_rows, gate, x1, final_g.reshape(1, d))


def kernel(x, mix_norm_g, w_in, ssd_conv_w, ssd_conv_b, ssd_dt_bias, ssd_a_log, ssd_d, ssd_norm_g,
           gmlp_v_norm_g, gmlp_w_s, gmlp_b_s, gmlp_out_norm_g, w_out, ffn_norm_g, w_router, b_router,
           w_gate_up, b_gate_up, w_down, b_down, final_norm_g):
    bsz, t_len, d = x.shape
    assert bsz == 1
    depth = w_in.shape[0]
    ssd_w = ssd_norm_g.shape[-1]
    heads = ssd_dt_bias.shape[-1]
    gmlp_w = gmlp_out_norm_g.shape[-1]
    conv_dim = ssd_conv_b.shape[-1]
    assert heads <= V7X_LANES
    c_dt = ssd_w + conv_dim
    c_uv = c_dt + heads
    off_u, off_z, off_xs = 0, 2 * gmlp_w, 2 * gmlp_w + ssd_w

    h = x.reshape(t_len, d)
    out = None
    for layer in range(depth):
        w = w_in[layer]
        w_t = jnp.swapaxes(w, 0, 1)
        w_dt = jnp.pad(w_t[c_dt:c_uv], ((0, V7X_LANES - heads), (0, 0))).astype(BF16)
        p, dt_raw = _inproj(h, mix_norm_g[layer], w_t, w_dt, ssd_conv_w[layer], ssd_conv_b[layer],
                            2 * gmlp_w, ssd_w, c_uv)
        groups = (conv_dim - ssd_w) // (2 * SSD_STATE)
        y_ssd = _ssd(p, dt_raw, ssd_dt_bias[layer], ssd_a_log[layer], ssd_d[layer], ssd_norm_g[layer],
                     groups, off_z, off_xs)
        y_gmlp = _gmlp(p, gmlp_v_norm_g[layer], gmlp_w_s[layer], gmlp_b_s[layer],
                       gmlp_out_norm_g[layer], off_u)
        x1 = _outproj(y_ssd, y_gmlp, w_out[layer].astype(BF16), h)

        xp, idx_pad, gate_pad = _router(x1, ffn_norm_g[layer], w_router[layer], b_router[layer])
        plan = _route_plan(idx_pad[:, :TOP_K], w_router.shape[-1])
        act = _moe_up(xp, plan, w_gate_up[layer], b_gate_up[layer])
        y_rows = _moe_down(act, plan, w_down[layer], b_down[layer])
        last = layer == depth - 1
        assert last, "multi-layer stacking needs an un-normalised combine"
        out = _combine(y_rows, plan["dest"], gate_pad, x1, final_norm_g)
    return out.reshape(bsz, t_len, d)
```

```python
import functools
import math

import jax
import jax.numpy as jnp
from jax import lax
from jax.experimental import pallas as pl
from jax.experimental.pallas import tpu as pltpu

F32, BF16, U32, I32 = jnp.float32, jnp.bfloat16, jnp.uint32, jnp.int32

EPS = 1e-5
SSD_STATE = 128
SSD_CONV = 4
CHUNK = 128
TOP_K = 4
SSD_ROWS = 1024
SWIGLU_ALPHA = 1.702
SWIGLU_LIMIT = 7.0

V7X_LANES = 128
V7X_SUBLANES = 8
V7X_VMEM_BYTES = 64 * 1024 * 1024
VMEM_LIMIT = V7X_VMEM_BYTES - 8 * 1024 * 1024

MOE_BLK = 128
MOE_NSB = 12
ISSUE_UNROLL = 8
HI_MASK = 0xFFFF0000


_gcd = math.gcd


def _largest_tile(total, cap, quantum=V7X_LANES):
    best = None
    t = quantum
    while t <= min(total, cap):
        if total % t == 0:
            best = t
        t += quantum
    assert best is not None, (total, cap, quantum)
    return best


def _params(semantics):
    return pltpu.CompilerParams(dimension_semantics=semantics, vmem_limit_bytes=VMEM_LIMIT)


def _silu(x):
    return x * jax.nn.sigmoid(x)


def _gelu(x):
    return 0.5 * x * (1.0 + lax.erf(x * (2.0 ** -0.5)))


def _softplus(x):
    return jnp.maximum(x, 0.0) + jnp.log1p(jnp.exp(-jnp.abs(x)))


def _pack_pair(lo, hi):
    lo_bits = lax.bitcast_convert_type(lo.astype(BF16).astype(F32), U32)
    hi_bits = lax.bitcast_convert_type(hi.astype(BF16).astype(F32), U32)
    return (lo_bits >> 16) | (hi_bits & jnp.uint32(HI_MASK))


def _unpack_pair(u):
    lo = lax.bitcast_convert_type(u << 16, F32)
    hi = lax.bitcast_convert_type(u & jnp.uint32(HI_MASK), F32)
    return lo, hi


NT_DIMS = (((1,), (1,)), ((), ()))


def _inproj_kernel(x_hbm, g_ref, w_ref, wdt_ref, cw_ref, cb_ref, p_ref, dt_ref, xbuf, xn_ref, halo_ref, sem,
                   *, n_gelu, n_silu):
    i = pl.program_id(0)
    j = pl.program_id(1)
    tm = xbuf.shape[0]
    rows = _largest_tile(tm, 128, V7X_SUBLANES)
    tail = halo_ref.shape[1]

    def x_copy(ii):
        return pltpu.make_async_copy(x_hbm.at[pl.ds(pl.multiple_of(ii * tm, tm), tm), :], xbuf, sem.at[0])

    @pl.when(j == 0)
    def _():
        @pl.when(i == 0)
        def _():
            x_copy(0).start()
            halo_ref[...] = jnp.zeros(halo_ref.shape, F32)

        x_copy(i).wait()

        def norm(r, carry):
            sl = pl.ds(pl.multiple_of(r * rows, rows), rows)
            x = xbuf[sl, :]
            ms = jnp.mean(x * x, axis=-1, keepdims=True)
            xn = (x * lax.rsqrt(ms + EPS) * g_ref[...]).astype(BF16)
            xn_ref[sl, :] = xn
            dt_ref[sl, :] = lax.dot_general(xn, wdt_ref[...], NT_DIMS, preferred_element_type=F32)
            return carry

        lax.fori_loop(0, tm // rows, norm, 0)

        @pl.when(i + 1 < pl.num_programs(0))
        def _():
            x_copy(i + 1).start()

    def project():
        return lax.dot_general(xn_ref[...], w_ref[...].astype(BF16), NT_DIMS, preferred_element_type=F32)

    @pl.when(j < n_gelu)
    def _():
        p_ref[...] = _gelu(project()).astype(BF16)

    @pl.when((j >= n_gelu) & (j < n_gelu + n_silu))
    def _():
        p_ref[...] = _silu(project()).astype(BF16)

    @pl.when(j >= n_gelu + n_silu)
    def _():
        x = project()
        jc = j - (n_gelu + n_silu)
        ext = jnp.concatenate([halo_ref[jc], x], axis=0)
        acc = cb_ref[...] + cw_ref[SSD_CONV - 1:SSD_CONV, :] * ext
        for s in range(1, SSD_CONV):
            acc = acc + cw_ref[SSD_CONV - 1 - s:SSD_CONV - s, :] * pltpu.roll(ext, s, axis=0)
        p_ref[...] = _silu(acc[tail:, :]).astype(BF16)
        halo_ref[jc] = x[tm - tail:tm, :]


def _inproj(x, g, w_t, w_dt, conv_w, conv_b, n_gelu_cols, n_silu_cols, row_gelu):
    t_len, d = x.shape
    n_conv_cols = conv_b.shape[0]
    n_out = n_gelu_cols + n_silu_cols + n_conv_cols
    assert row_gelu + n_gelu_cols == w_t.shape[0] and row_gelu >= n_silu_cols + n_conv_cols
    tm = _largest_tile(t_len, 1024, V7X_SUBLANES)
    tn = _largest_tile(_gcd(_gcd(n_gelu_cols, n_silu_cols), n_conv_cols), 512)
    n_gelu, n_silu, n_conv = n_gelu_cols // tn, n_silu_cols // tn, n_conv_cols // tn
    tail = 2 * V7X_SUBLANES
    assert row_gelu % tail == 0 and tn % tail == 0
    conv_map = lambda i, j: (0, jnp.maximum(j - (n_gelu + n_silu), 0))
    w_map = lambda i, j: (
        tail * jnp.where(j < n_gelu, row_gelu // tail + j * (tn // tail), (j - n_gelu) * (tn // tail)), 0)
    return pl.pallas_call(
        functools.partial(_inproj_kernel, n_gelu=n_gelu, n_silu=n_silu),
        grid=(t_len // tm, n_out // tn),
        in_specs=[
            pl.BlockSpec(memory_space=pl.ANY),
            pl.BlockSpec((1, d), lambda i, j: (0, 0)),
            pl.BlockSpec((pl.Element(tn), pl.Element(d)), w_map),
            pl.BlockSpec((V7X_LANES, d), lambda i, j: (0, 0)),
            pl.BlockSpec((SSD_CONV, tn), conv_map),
            pl.BlockSpec((1, tn), conv_map),
        ],
        out_specs=[
            pl.BlockSpec((tm, tn), lambda i, j: (i, j)),
            pl.BlockSpec((tm, V7X_LANES), lambda i, j: (i, 0)),
        ],
        out_shape=[
            jax.ShapeDtypeStruct((t_len, n_out), BF16),
            jax.ShapeDtypeStruct((t_len, V7X_LANES), F32),
        ],
        scratch_shapes=[
            pltpu.VMEM((tm, d), F32),
            pltpu.VMEM((tm, d), BF16),
            pltpu.VMEM((n_conv, tail, tn), F32),
            pltpu.SemaphoreType.DMA((1,)),
        ],
        compiler_params=_params(("arbitrary", "arbitrary")),
        name="inproj",
    )(x, g.reshape(1, d), w_t, w_dt, conv_w, conv_b.reshape(1, n_conv_cols))


def _split3(x):
    hi = x.astype(BF16)
    r1 = x - hi.astype(F32)
    mid = r1.astype(BF16)
    lo = (r1 - mid.astype(F32)).astype(BF16)
    return hi, mid, lo


def _ssd_kernel(sz_ref, xs_ref, b_ref, c_ref, dt_ref, dtb_ref, alog_ref, dsk_ref, ng_ref, y_ref, state_ref,
                *, hpg, hd):
    width = hpg * hd
    lanes = V7X_LANES
    per = lanes // hd

    @pl.when(pl.program_id(1) == 0)
    def _():
        state_ref[...] = jnp.zeros(state_ref.shape, F32)

    ri = lax.broadcasted_iota(I32, (CHUNK, CHUNK), 0)
    ci = lax.broadcasted_iota(I32, (CHUNK, CHUNK), 1)
    causal = ri >= ci
    tril = causal.astype(BF16)
    neg_a = -jnp.exp(alog_ref[...])

    def expand(q):
        cols = []
        for j in range(width // lanes):
            blk = jnp.broadcast_to(q[:, j * per:j * per + 1], (CHUNK, lanes))
            for t in range(1, per):
                nxt = jnp.broadcast_to(q[:, j * per + t:j * per + t + 1], (CHUNK, lanes))
                blk = jnp.where(ci >= t * hd, nxt, blk)
            cols.append(blk)
        return jnp.concatenate(cols, axis=1)

    def chunk(k, carry):
        rows = pl.ds(pl.multiple_of(k * CHUNK, CHUNK), CHUNK)
        xs = xs_ref[rows, :].astype(F32)
        bm = b_ref[rows, :]
        cm = c_ref[rows, :]

        dt = _softplus(dt_ref[rows, :] + dtb_ref[...])
        a_dt = dt * neg_a
        acum = sum(jnp.dot(tril, piece, preferred_element_type=F32) for piece in _split3(a_dt))
        acum_t = acum.T
        acum_f = expand(acum)
        last = acum_f[CHUNK - 1:CHUNK, :]
        xdt = xs * expand(dt)

        cb = lax.dot_general(cm, bm, (((1,), (1,)), ((), ())), preferred_element_type=F32)
        y_cols = []
        for j in range(width // lanes):
            x_col = xdt[:, j * lanes:(j + 1) * lanes]
            y_col = jnp.zeros((CHUNK, lanes), F32)
            for t in range(per):
                h = j * per + t
                seg = acum[:, h:h + 1] - acum_t[h:h + 1, :]
                lmat = jnp.exp(jnp.where(causal, seg, -jnp.inf))
                m_h = (cb * lmat).astype(BF16)
                mask = (ci >= t * hd) & (ci < (t + 1) * hd)
                x_h = jnp.where(mask, x_col, 0.0).astype(BF16)
                y_col = y_col + jnp.dot(m_h, x_h, preferred_element_type=F32)
            y_cols.append(y_col)
        y_diag = jnp.concatenate(y_cols, axis=1)

        state = state_ref[...]
        y_off = jnp.dot(cm, state.astype(BF16), preferred_element_type=F32) * jnp.exp(acum_f)
        x_dec = (xdt * jnp.exp(last - acum_f)).astype(BF16)
        s_new = lax.dot_general(bm, x_dec, (((0,), (0,)), ((), ())), preferred_element_type=F32)
        state_ref[...] = state * jnp.exp(last) + s_new

        yg = (y_diag + y_off + xs * dsk_ref[...]) * sz_ref[rows, :].astype(F32)
        yn = yg * lax.rsqrt(jnp.mean(yg * yg, axis=-1, keepdims=True) + EPS) * ng_ref[...]
        y_ref[rows, :] = yn.astype(BF16)
        return carry

    lax.fori_loop(0, y_ref.shape[0] // CHUNK, chunk, 0, unroll=4)


def _ssd(p, dt_raw, dt_bias, a_log, d_skip, norm_g, groups, off_z, off_xs):
    t_len = p.shape[0]
    heads = dt_bias.shape[0]
    width = norm_g.shape[0]
    hd = width // heads
    hpg = heads // groups
    gw = hpg * hd
    n_st = SSD_STATE
    rows = _largest_tile(t_len, SSD_ROWS, CHUNK)
    assert V7X_LANES % hd == 0 and gw % V7X_LANES == 0 and hpg <= V7X_LANES
    assert width // groups == gw
    off_b = off_xs + width
    off_c = off_b + groups * n_st
    assert off_z % gw == 0 and off_xs % gw == 0 and off_b % n_st == 0

    def per_group_lanes(v):
        v = v.reshape(groups, 1, hpg)
        return jnp.pad(v, ((0, 0), (0, 0), (0, V7X_LANES - hpg)))

    dt_g = dt_raw[:, :heads].reshape(t_len, groups, hpg).transpose(1, 0, 2)
    dt_g = jnp.pad(dt_g, ((0, 0), (0, 0), (0, V7X_LANES - hpg)))

    grp = lambda g, c: (g, 0, 0)
    return pl.pallas_call(
        functools.partial(_ssd_kernel, hpg=hpg, hd=hd),
        grid=(groups, t_len // rows),
        in_specs=[
            pl.BlockSpec((rows, gw), lambda g, c: (c, off_z // gw + g)),
            pl.BlockSpec((rows, gw), lambda g, c: (c, off_xs // gw + g)),
            pl.BlockSpec((rows, n_st), lambda g, c: (c, off_b // n_st + g)),
            pl.BlockSpec((rows, n_st), lambda g, c: (c, off_c // n_st + g)),
            pl.BlockSpec((None, rows, V7X_LANES), lambda g, c: (g, c, 0)),
            pl.BlockSpec((None, 1, V7X_LANES), grp),
            pl.BlockSpec((None, 1, V7X_LANES), grp),
            pl.BlockSpec((1, gw), lambda g, c: (0, g)),
            pl.BlockSpec((1, gw), lambda g, c: (0, g)),
        ],
        out_specs=pl.BlockSpec((rows, gw), lambda g, c: (c, g)),
        out_shape=jax.ShapeDtypeStruct((t_len, width), BF16),
        scratch_shapes=[pltpu.VMEM((n_st, gw), F32)],
        compiler_params=_params(("arbitrary", "arbitrary")),
        name="ssd",
    )(p, p, p, p, dt_g, per_group_lanes(dt_bias), per_group_lanes(a_log),
      jnp.repeat(d_skip, hd).reshape(1, width), norm_g.reshape(1, width))


def _row_sums(x):
    ones = jnp.ones((x.shape[1], V7X_LANES), BF16)
    hi = x.astype(BF16)
    lo = (x - hi.astype(F32)).astype(BF16)
    return (jnp.dot(hi, ones, preferred_element_type=F32) + jnp.dot(lo, ones, preferred_element_type=F32))


def _gmlp_kernel(u_ref, v_ref, vg_ref, ws_ref, bs_ref, og_ref, y_ref, yscr_ref, *, heads, hd):
    lanes = V7X_LANES
    reps = hd // lanes
    ri = lax.broadcasted_iota(I32, (CHUNK, CHUNK), 0)
    ci = lax.broadcasted_iota(I32, (CHUNK, CHUNK), 1)
    causal = ri >= ci
    ssq = jnp.zeros((CHUNK, lanes), F32)
    invs = []
    for h in range(heads):
        v_h = v_ref[:, h * hd:(h + 1) * hd].astype(F32)
        invs.append(lax.rsqrt(_row_sums(v_h * v_h) / hd + EPS))
    for h in range(heads):
        cols = slice(h * hd, (h + 1) * hd)
        u_h = u_ref[:, cols].astype(F32)
        v_h = v_ref[:, cols].astype(F32)
        v_n = v_h * jnp.tile(invs[h], (1, reps)) * vg_ref[:, cols]
        w_h = jnp.where(causal, ws_ref[h], 0.0).astype(BF16)
        sv = jnp.dot(w_h, v_n.astype(BF16), preferred_element_type=F32) + jnp.tile(bs_ref[h], (1, reps))
        y_h = u_h * sv
        yscr_ref[:, cols] = y_h
        y_sq = y_h * y_h
        for r in range(reps):
            ssq = ssq + y_sq[:, r * lanes:(r + 1) * lanes]
    scale = lax.rsqrt(_row_sums(ssq) / (heads * hd) + EPS)
    y_ref[...] = (yscr_ref[...] * jnp.tile(scale, (1, heads * reps)) * og_ref[...]).astype(BF16)


def _gmlp(p, v_norm_g, w_s, b_s, out_norm_g, off_u):
    t_len = p.shape[0]
    heads = w_s.shape[0]
    width = out_norm_g.shape[0]
    hd = width // heads
    assert w_s.shape[1] == CHUNK and off_u % width == 0 and hd % V7X_LANES == 0
    b_lanes = jnp.broadcast_to(b_s[:, :, None], (heads, CHUNK, V7X_LANES))
    return pl.pallas_call(
        functools.partial(_gmlp_kernel, heads=heads, hd=hd),
        grid=(t_len // CHUNK,),
        in_specs=[
            pl.BlockSpec((CHUNK, width), lambda c: (c, off_u // width)),
            pl.BlockSpec((CHUNK, width), lambda c: (c, off_u // width + 1)),
            pl.BlockSpec((1, width), lambda c: (0, 0)),
            pl.BlockSpec((heads, CHUNK, CHUNK), lambda c: (0, 0, 0)),
            pl.BlockSpec((heads, CHUNK, V7X_LANES), lambda c: (0, 0, 0)),
            pl.BlockSpec((1, width), lambda c: (0, 0)),
        ],
        out_specs=pl.BlockSpec((CHUNK, width), lambda c: (c, 0)),
        out_shape=jax.ShapeDtypeStruct((t_len, width), BF16),
        scratch_shapes=[pltpu.VMEM((CHUNK, width), F32)],
        compiler_params=_params(("arbitrary",)),
        name="gmlp",
    )(p, p, v_norm_g.reshape(1, width), w_s, b_lanes, out_norm_g.reshape(1, width))


def _outproj_kernel(ys_ref, yg_ref, wa_ref, wb_ref, x_ref, o_ref):
    acc = jnp.dot(ys_ref[...], wa_ref[...], preferred_element_type=F32)
    acc = acc + jnp.dot(yg_ref[...], wb_ref[...], preferred_element_type=F32)
    o_ref[...] = x_ref[...] + acc


def _outproj(y_ssd, y_gmlp, w_out, x):
    t_len, d = x.shape
    ka, kb = y_ssd.shape[1], y_gmlp.shape[1]
    assert ka == kb and w_out.shape[0] == ka + kb
    tm = _largest_tile(t_len, 512, V7X_SUBLANES)
    tn = _largest_tile(d, 512)
    return pl.pallas_call(
        _outproj_kernel,
        grid=(t_len // tm, d // tn),
        in_specs=[
            pl.BlockSpec((tm, ka), lambda i, j: (i, 0)),
            pl.BlockSpec((tm, kb), lambda i, j: (i, 0)),
            pl.BlockSpec((ka, tn), lambda i, j: (0, j)),
            pl.BlockSpec((kb, tn), lambda i, j: (1, j)),
            pl.BlockSpec((tm, tn), lambda i, j: (i, j)),
        ],
        out_specs=pl.BlockSpec((tm, tn), lambda i, j: (i, j)),
        out_shape=jax.ShapeDtypeStruct((t_len, d), F32),
        compiler_params=_params(("arbitrary", "arbitrary")),
        name="outproj",
    )(y_ssd, y_gmlp, w_out, w_out, x)


def _router_kernel(x_ref, g_ref, wr_ref, br_ref, xp_ref, idx_ref, gate_ref, *, n_exp):
    x = x_ref[...]
    tm, d = x.shape
    half = d // 2
    ms = jnp.mean(x * x, axis=-1, keepdims=True)
    xn = x * lax.rsqrt(ms + EPS) * g_ref[...]
    xp_ref[...] = _pack_pair(xn[:, :half], xn[:, half:])

    logits = jnp.dot(xn.astype(BF16), wr_ref[...], preferred_element_type=F32) + br_ref[...]
    lane = lax.broadcasted_iota(I32, (tm, V7X_LANES), 1)
    vals = jnp.where(lane < n_exp, logits, -jnp.inf)
    idx_out = jnp.zeros((tm, V7X_LANES), I32)
    val_out = jnp.full((tm, V7X_LANES), -jnp.inf, F32)
    for k in range(TOP_K):
        m = jnp.max(vals, axis=-1, keepdims=True)
        sel = jnp.min(jnp.where(vals == m, lane, V7X_LANES), axis=-1, keepdims=True)
        idx_out = jnp.where(lane == k, sel, idx_out)
        val_out = jnp.where(lane == k, m, val_out)
        vals = jnp.where(lane == sel, -jnp.inf, vals)
    e = jnp.exp(val_out - jnp.max(val_out, axis=-1, keepdims=True))
    idx_ref[...] = idx_out
    gate_ref[...] = e / jnp.sum(e, axis=-1, keepdims=True)


def _router(x1, g, w_router, b_router):
    t_len, d = x1.shape
    n_exp = w_router.shape[1]
    assert n_exp <= V7X_LANES and TOP_K <= n_exp
    tm = _largest_tile(t_len, 256, V7X_SUBLANES)
    wr = jnp.pad(w_router, ((0, 0), (0, V7X_LANES - n_exp))).astype(BF16)
    br = jnp.pad(b_router, (0, V7X_LANES - n_exp)).reshape(1, V7X_LANES)
    return pl.pallas_call(
        functools.partial(_router_kernel, n_exp=n_exp),
        grid=(t_len // tm,),
        in_specs=[
            pl.BlockSpec((tm, d), lambda i: (i, 0)),
            pl.BlockSpec((1, d), lambda i: (0, 0)),
            pl.BlockSpec((d, V7X_LANES), lambda i: (0, 0)),
            pl.BlockSpec((1, V7X_LANES), lambda i: (0, 0)),
        ],
        out_specs=[
            pl.BlockSpec((tm, d // 2), lambda i: (i, 0)),
            pl.BlockSpec((tm, V7X_LANES), lambda i: (i, 0)),
            pl.BlockSpec((tm, V7X_LANES), lambda i: (i, 0)),
        ],
        out_shape=[
            jax.ShapeDtypeStruct((t_len, d // 2), U32),
            jax.ShapeDtypeStruct((t_len, V7X_LANES), I32),
            jax.ShapeDtypeStruct((t_len, V7X_LANES), F32),
        ],
        compiler_params=_params(("arbitrary",)),
        name="router",
    )(x1, g.reshape(1, d), wr, br)


def _row_tokens_kernel(dest_ref, tok_ref, *, top_k):
    per = ISSUE_UNROLL // top_k

    def clear(q, carry):
        for u in range(ISSUE_UNROLL):
            tok_ref[q * ISSUE_UNROLL + u] = 0
        return carry

    def place(q, carry):
        for u in range(ISSUE_UNROLL):
            tok_ref[dest_ref[q * ISSUE_UNROLL + u]] = q * per + u // top_k
        return carry

    lax.fori_loop(0, tok_ref.shape[0] // ISSUE_UNROLL, clear, 0)
    lax.fori_loop(0, dest_ref.shape[0] // ISSUE_UNROLL, place, 0)


def _row_tokens(dest, n_rows, top_k):
    assert ISSUE_UNROLL % top_k == 0 and n_rows % ISSUE_UNROLL == 0 and dest.shape[0] % ISSUE_UNROLL == 0
    return pl.pallas_call(
        functools.partial(_row_tokens_kernel, top_k=top_k),
        in_specs=[pl.BlockSpec(memory_space=pltpu.SMEM)],
        out_specs=pl.BlockSpec(memory_space=pltpu.SMEM),
        out_shape=jax.ShapeDtypeStruct((n_rows,), I32),
        name="row_tokens",
    )(dest)


def _route_plan(top_idx, n_exp):
    t_len, top_k = top_idx.shape
    n_assign = t_len * top_k
    e = top_idx.reshape(n_assign)
    onehot = (e[:, None] == jnp.arange(n_exp, dtype=I32)[None, :]).astype(I32)
    csum = jnp.cumsum(onehot, axis=0)
    rank = jnp.sum(onehot * csum, axis=1) - 1
    counts = csum[-1]
    nb = (counts + MOE_BLK - 1) // MOE_BLK
    sb_end = jnp.cumsum(nb)
    sb_start = sb_end - nb
    dest = sb_start[e] * MOE_BLK + rank
    n_sb = n_assign // MOE_BLK + n_exp
    n_rows = n_sb * MOE_BLK
    row_tok = _row_tokens(dest.astype(I32), n_rows, top_k)

    nv = (nb + MOE_NSB - 1) // MOE_NSB
    v_end = jnp.cumsum(nv)
    v_start = v_end - nv
    n_pass = -(-n_sb // MOE_NSB) + n_exp
    vid = jnp.arange(n_pass, dtype=I32)
    ve = jnp.minimum(jnp.searchsorted(v_end, vid, side='right'), n_exp - 1).astype(I32)
    q = vid - v_start[ve]
    valid = vid < v_end[-1]
    vs = jnp.where(valid, sb_start[ve] + q * MOE_NSB, 0).astype(I32)
    vn = jnp.where(valid, jnp.clip(nb[ve] - q * MOE_NSB, 0, MOE_NSB), 0).astype(I32)
    last_e = ve[jnp.maximum(v_end[-1] - 1, 0)]
    ve = jnp.where(valid, ve, last_e).astype(I32)
    used = sb_end[-1:].astype(I32)
    n_live = jnp.maximum(v_end[-1], 1).astype(I32)
    return dict(row_tok=row_tok, dest=dest.astype(I32), ve=ve, vs=vs, vn=vn, used=used, n_rows=n_rows,
                n_pass=n_live)


def _for_each_block(n, compute, emit):
    def quad(q, carry):
        b = 4 * q
        first = compute(b, 2)
        second = compute(b + 2, 2)
        emit(b, 2, first)
        emit(b + 2, 2, second)
        return carry

    quads = lax.shift_right_logical(n, 2)
    lax.fori_loop(0, quads, quad, 0)

    @pl.when((n & 2) != 0)
    def _():
        emit(4 * quads, 2, compute(4 * quads, 2))

    @pl.when((n & 1) != 0)
    def _():
        emit(n - 1, 1, compute(n - 1, 1))


def _zero_tail(obuf, n_used, n_sb, n_j, out_copy):
    tile = obuf.shape[2]
    obuf[0, 0:MOE_BLK, :] = jnp.zeros((MOE_BLK, tile), obuf.dtype)

    def start(sb, carry):
        for jj in range(n_j):
            out_copy(0, 0, pl.multiple_of(sb * MOE_BLK, MOE_BLK), jj * tile).start()
        return carry

    def wait(sb, carry):
        for jj in range(n_j):
            out_copy(0, 0, 0, 0).wait()
        return carry

    lax.fori_loop(n_used, n_sb, start, 0)
    lax.fori_loop(n_used, n_sb, wait, 0)


def _moe_up_kernel(ve_ref, vs_ref, vn_ref, used_ref, tok_ref, xp_hbm, wg_ref, wl_ref, bg_ref, bl_ref, act_hbm,
                   xbuf, obuf, xsem, osem, ocnt, *, tf, n_j):
    v = pl.program_id(0)
    j = pl.program_id(1)
    n_pass = pl.num_programs(0)
    half = xbuf.shape[2]
    slot = v % 2
    n_sub = vn_ref[v]

    def row_copy(tok, sl, r0, u):
        window = xbuf.at[sl, pl.ds(r0, ISSUE_UNROLL), :]
        return pltpu.make_async_copy(xp_hbm.at[pl.ds(tok, 1), :], window.at[pl.ds(u, 1), :], xsem.at[sl])

    def gather_start(vv, sl):
        base = vs_ref[vv] * MOE_BLK

        def body(q, carry):
            r0 = pl.multiple_of(q * ISSUE_UNROLL, ISSUE_UNROLL)
            for u in range(ISSUE_UNROLL):
                row_copy(tok_ref[base + r0 + u], sl, r0, u).start()
            return carry

        lax.fori_loop(0, vn_ref[vv] * (MOE_BLK // ISSUE_UNROLL), body, 0)

    def gather_wait(vv, sl):
        def body(i, carry):
            r0 = pl.multiple_of(i * MOE_BLK, MOE_BLK)
            pltpu.make_async_copy(xp_hbm.at[pl.ds(0, MOE_BLK), :], xbuf.at[sl, pl.ds(r0, MOE_BLK), :],
                                  xsem.at[sl]).wait()
            return carry

        lax.fori_loop(0, vn_ref[vv], body, 0)

    def out_copy(p, r0, row, col):
        return pltpu.make_async_copy(obuf.at[p, pl.ds(r0, MOE_BLK), :],
                                     act_hbm.at[pl.ds(row, MOE_BLK), pl.ds(col, tf)], osem.at[p])

    def drain(p):
        def body(i, carry):
            out_copy(p, 0, 0, 0).wait()
            return carry

        lax.fori_loop(0, ocnt[p], body, 0)
        ocnt[p] = 0

    @pl.when(j == 0)
    def _():
        @pl.when(v == 0)
        def _():
            ocnt[0] = 0
            ocnt[1] = 0
            gather_start(0, 0)

        gather_wait(v, slot)

    step = v * n_j + j
    p = step % 2
    drain(p)

    @pl.when(n_sub > 0)
    def _():
        nxt = jnp.minimum(v + 1, n_pass - 1)
        n_next = jnp.where(v + 1 < n_pass, vn_ref[nxt], 0)
        groups = n_next * (MOE_BLK // (ISSUE_UNROLL * n_j))
        first = j * groups
        base = vs_ref[nxt] * MOE_BLK

        def request(q, carry):
            r0 = pl.multiple_of((first + q) * ISSUE_UNROLL, ISSUE_UNROLL)
            for u in range(ISSUE_UNROLL):
                row_copy(tok_ref[base + r0 + u], 1 - slot, r0, u).start()
            return carry

        lax.fori_loop(0, groups, request, 0)

        def compute(b, nb):
            r0 = pl.multiple_of(b * MOE_BLK, MOE_BLK)
            lo, hi = _unpack_pair(xbuf[slot, pl.ds(r0, nb * MOE_BLK), :])
            w_lo = jnp.concatenate([wg_ref[0:half, :], wl_ref[0:half, :]], axis=1).astype(BF16)
            w_hi = jnp.concatenate([wg_ref[half:2 * half, :], wl_ref[half:2 * half, :]], axis=1).astype(BF16)
            h = jnp.dot(lo.astype(BF16), w_lo, preferred_element_type=F32)
            h = h + jnp.dot(hi.astype(BF16), w_hi, preferred_element_type=F32)
            glu = jnp.minimum(h[:, 0:tf] + bg_ref[...], SWIGLU_LIMIT)
            lin = jnp.clip(h[:, tf:2 * tf] + bl_ref[...], -SWIGLU_LIMIT, SWIGLU_LIMIT)
            return (glu * jax.nn.sigmoid(SWIGLU_ALPHA * glu) * (lin + 1.0)).astype(BF16)

        def emit(b, nb, act):
            r0 = pl.multiple_of(b * MOE_BLK, MOE_BLK)
            obuf[p, pl.ds(r0, nb * MOE_BLK), :] = act
            for t in range(nb):
                row = pl.multiple_of((vs_ref[v] + b + t) * MOE_BLK, MOE_BLK)
                out_copy(p, r0 + t * MOE_BLK, row, pl.multiple_of(j * tf, tf)).start()

        _for_each_block(n_sub, compute, emit)
        ocnt[p] = n_sub

    @pl.when(step == n_pass * n_j - 1)
    def _():
        drain(0)
        drain(1)
        _zero_tail(obuf, used_ref[0], act_hbm.shape[0] // MOE_BLK, n_j, out_copy)


def _moe_up(xp, plan, w_gate_up, b_gate_up):
    n_exp, d, two_f = w_gate_up.shape
    f = two_f // 2
    half = xp.shape[1]
    assert half * 2 == d
    tf = _largest_tile(f, 256)
    n_j = f // tf
    rows_pass = MOE_NSB * MOE_BLK
    assert MOE_BLK % (ISSUE_UNROLL * n_j) == 0

    def w_map(off):
        def index_map(v, j, ve, vs, vn, used, tok):
            jj = jnp.where(vn[v] > 0, j, n_j - 1)
            return (ve[v], 0, off + jj)
        return index_map

    grid_spec = pltpu.PrefetchScalarGridSpec(
        num_scalar_prefetch=5,
        grid=(plan["n_pass"], n_j),
        in_specs=[
            pl.BlockSpec(memory_space=pl.ANY),
            pl.BlockSpec((None, d, tf), w_map(0)),
            pl.BlockSpec((None, d, tf), w_map(n_j)),
            pl.BlockSpec((None, 1, tf), w_map(0)),
            pl.BlockSpec((None, 1, tf), w_map(n_j)),
        ],
        out_specs=pl.BlockSpec(memory_space=pl.ANY),
        scratch_shapes=[
            pltpu.VMEM((2, rows_pass, half), U32),
            pltpu.VMEM((2, rows_pass, tf), BF16),
            pltpu.SemaphoreType.DMA((2,)),
            pltpu.SemaphoreType.DMA((2,)),
            pltpu.SMEM((2,), I32),
        ],
    )
    b3 = b_gate_up.reshape(n_exp, 1, two_f)
    return pl.pallas_call(
        functools.partial(_moe_up_kernel, tf=tf, n_j=n_j),
        grid_spec=grid_spec,
        out_shape=jax.ShapeDtypeStruct((plan["n_rows"], f), BF16),
        compiler_params=_params(("arbitrary", "arbitrary")),
        name="moe_up",
    )(plan["ve"], plan["vs"], plan["vn"], plan["used"], plan["row_tok"], xp, w_gate_up, w_gate_up, b3, b3)


def _moe_down_kernel(ve_ref, vs_ref, vn_ref, used_ref, act_hbm, wa_ref, wb_ref, ba_ref, bb_ref, y_hbm,
                     abuf, obuf, asem, osem, ocnt, *, tn):
    v = pl.program_id(0)
    j = pl.program_id(1)
    n_pass = pl.num_programs(0)
    n_j = pl.num_programs(1)
    slot = v % 2
    n_sub = vn_ref[v]

    def in_copy(row, sl, r0):
        return pltpu.make_async_copy(act_hbm.at[pl.ds(row, MOE_BLK), :],
                                     abuf.at[sl, pl.ds(r0, MOE_BLK), :], asem.at[sl])

    def load_start(vv, sl):
        def body(i, carry):
            r0 = pl.multiple_of(i * MOE_BLK, MOE_BLK)
            in_copy(pl.multiple_of((vs_ref[vv] + i) * MOE_BLK, MOE_BLK), sl, r0).start()
            return carry

        lax.fori_loop(0, vn_ref[vv], body, 0)

    def load_wait(vv, sl):
        def body(i, carry):
            in_copy(0, sl, pl.multiple_of(i * MOE_BLK, MOE_BLK)).wait()
            return carry

        lax.fori_loop(0, vn_ref[vv], body, 0)

    def out_copy(p, r0, row, col):
        return pltpu.make_async_copy(obuf.at[p, pl.ds(r0, MOE_BLK), :],
                                     y_hbm.at[pl.ds(row, MOE_BLK), pl.ds(col, tn)], osem.at[p])

    def drain(p):
        def body(i, carry):
            out_copy(p, 0, 0, 0).wait()
            return carry

        lax.fori_loop(0, ocnt[p], body, 0)
        ocnt[p] = 0

    @pl.when(j == 0)
    def _():
        @pl.when(v == 0)
        def _():
            ocnt[0] = 0
            ocnt[1] = 0
            load_start(0, 0)

        @pl.when(v + 1 < n_pass)
        def _():
            load_start(v + 1, 1 - slot)

        load_wait(v, slot)

    step = v * n_j + j
    p = step % 2
    drain(p)

    @pl.when(n_sub > 0)
    def _():
        def compute(b, nb):
            r0 = pl.multiple_of(b * MOE_BLK, MOE_BLK)
            w = jnp.concatenate([wa_ref[...], wb_ref[...]], axis=1).astype(BF16)
            y = jnp.dot(abuf[slot, pl.ds(r0, nb * MOE_BLK), :], w, preferred_element_type=F32)
            return _pack_pair(y[:, 0:tn] + ba_ref[...], y[:, tn:2 * tn] + bb_ref[...])

        def emit(b, nb, packed):
            r0 = pl.multiple_of(b * MOE_BLK, MOE_BLK)
            obuf[p, pl.ds(r0, nb * MOE_BLK), :] = packed
            for t in range(nb):
                row = pl.multiple_of((vs_ref[v] + b + t) * MOE_BLK, MOE_BLK)
                out_copy(p, r0 + t * MOE_BLK, row, pl.multiple_of(j * tn, tn)).start()

        _for_each_block(n_sub, compute, emit)
        ocnt[p] = n_sub

    @pl.when(step == n_pass * n_j - 1)
    def _():
        drain(0)
        drain(1)
        _zero_tail(obuf, used_ref[0], y_hbm.shape[0] // MOE_BLK, n_j, out_copy)


def _moe_down(act, plan, w_down, b_down):
    n_exp, f, d = w_down.shape
    half = d // 2
    tn = _largest_tile(half, 256)
    n_j = half // tn
    rows_pass = MOE_NSB * MOE_BLK

    def w_map(off):
        def index_map(v, j, ve, vs, vn, used):
            jj = jnp.where(vn[v] > 0, j, n_j - 1)
            return (ve[v], 0, off + jj)
        return index_map

    grid_spec = pltpu.PrefetchScalarGridSpec(
        num_scalar_prefetch=4,
        grid=(plan["n_pass"], n_j),
        in_specs=[
            pl.BlockSpec(memory_space=pl.ANY),
            pl.BlockSpec((None, f, tn), w_map(0)),
            pl.BlockSpec((None, f, tn), w_map(n_j)),
            pl.BlockSpec((None, 1, tn), w_map(0)),
            pl.BlockSpec((None, 1, tn), w_map(n_j)),
        ],
        out_specs=pl.BlockSpec(memory_space=pl.ANY),
        scratch_shapes=[
            pltpu.VMEM((2, rows_pass, f), BF16),
            pltpu.VMEM((2, rows_pass, tn), U32),
            pltpu.SemaphoreType.DMA((2,)),
            pltpu.SemaphoreType.DMA((2,)),
            pltpu.SMEM((2,), I32),
        ],
    )
    b3 = b_down.reshape(n_exp, 1, d)
    return pl.pallas_call(
        functools.partial(_moe_down_kernel, tn=tn),
        grid_spec=grid_spec,
        out_shape=jax.ShapeDtypeStruct((plan["n_rows"], half), U32),
        compiler_params=_params(("arbitrary", "arbitrary")),
        name="moe_down",
    )(plan["ve"], plan["vs"], plan["vn"], plan["used"], act, w_down, w_down, b3, b3)


def _combine_kernel(dest_ref, y_hbm, gate_ref, x_ref, g_ref, o_ref, ybuf, sem):
    i = pl.program_id(0)
    n_i = pl.num_programs(0)
    tm, d = x_ref.shape
    half = d // 2
    slot = i % 2

    def row_copy(row, sl, k, t0, u):
        window = ybuf.at[sl, k, pl.ds(t0, ISSUE_UNROLL), :]
        return pltpu.make_async_copy(y_hbm.at[pl.ds(row, 1), :], window.at[pl.ds(u, 1), :], sem.at[sl])

    def request(ii, sl, q):
        t0 = pl.multiple_of(q * ISSUE_UNROLL, ISSUE_UNROLL)
        for u in range(ISSUE_UNROLL):
            for k in range(TOP_K):
                row_copy(dest_ref[(ii * tm + t0 + u) * TOP_K + k], sl, k, t0, u).start()

    def gather_wait(sl):
        for k in range(TOP_K):
            pltpu.make_async_copy(y_hbm.at[pl.ds(0, tm), :], ybuf.at[sl, k], sem.at[sl]).wait()

    def combine(q, prefetch):
        rows = pl.ds(pl.multiple_of(q * ISSUE_UNROLL, ISSUE_UNROLL), ISSUE_UNROLL)
        acc_lo = x_ref[rows, 0:half]
        acc_hi = x_ref[rows, half:d]
        for k in range(TOP_K):
            lo, hi = _unpack_pair(ybuf[slot, k, rows, :])
            gk = gate_ref[rows, k:k + 1]
            acc_lo = acc_lo + gk * lo
            acc_hi = acc_hi + gk * hi
        if prefetch:
            request(i + 1, 1 - slot, q)
        ms = (jnp.sum(acc_lo * acc_lo, axis=-1, keepdims=True)
              + jnp.sum(acc_hi * acc_hi, axis=-1, keepdims=True)) / d
        r = lax.rsqrt(ms + EPS)
        o_ref[rows, 0:half] = acc_lo * r * g_ref[:, 0:half]
        o_ref[rows, half:d] = acc_hi * r * g_ref[:, half:d]

    n_groups = tm // ISSUE_UNROLL

    @pl.when(i == 0)
    def _():
        lax.fori_loop(0, n_groups, lambda q, c: (request(0, 0, q), c)[1], 0)

    gather_wait(slot)

    @pl.when(i + 1 < n_i)
    def _():
        lax.fori_loop(0, n_groups, lambda q, c: (combine(q, True), c)[1], 0, unroll=2)

    @pl.when(i + 1 == n_i)
    def _():
        lax.fori_loop(0, n_groups, lambda q, c: (combine(q, False), c)[1], 0, unroll=2)


def _combine(y_rows, dest, gate, x1, final_g):
    t_len, d = x1.shape
    half = d // 2
    tm = _largest_tile(t_len, 128, V7X_SUBLANES)
    grid_spec = pltpu.PrefetchScalarGridSpec(
        num_scalar_prefetch=1,
        grid=(t_len // tm,),
        in_specs=[
            pl.BlockSpec(memory_space=pl.ANY),
            pl.BlockSpec((tm, V7X_LANES), lambda i, dest: (i, 0)),
            pl.BlockSpec((tm, d), lambda i, dest: (i, 0)),
            pl.BlockSpec((1, d), lambda i, dest: (0, 0)),
        ],
        out_specs=pl.BlockSpec((tm, d), lambda i, dest: (i, 0)),
        scratch_shapes=[
            pltpu.VMEM((2, TOP_K, tm, half), U32),
            pltpu.SemaphoreType.DMA((2,)),
        ],
    )
    return pl.pallas_call(
        _combine_kernel,
        grid_spec=grid_spec,
        out_shape=jax.ShapeDtypeStruct((t_len, d), F32),
        compiler_params=_params(("arbitrary",)),
        name="combine",
    )(dest, y_rows, gate, x1, final_g.reshape(1, d))


def kernel(x, mix_norm_g, w_in, ssd_conv_w, ssd_conv_b, ssd_dt_bias, ssd_a_log, ssd_d, ssd_norm_g,
           gmlp_v_norm_g, gmlp_w_s, gmlp_b_s, gmlp_out_norm_g, w_out, ffn_norm_g, w_router, b_router,
           w_gate_up, b_gate_up, w_down, b_down, final_norm_g):
    bsz, t_len, d = x.shape
    assert bsz == 1
    depth = w_in.shape[0]
    ssd_w = ssd_norm_g.shape[-1]
    heads = ssd_dt_bias.shape[-1]
    gmlp_w = gmlp_out_norm_g.shape[-1]
    conv_dim = ssd_conv_b.shape[-1]
    assert heads <= V7X_LANES
    c_dt = ssd_w + conv_dim
    c_uv = c_dt + heads
    off_u, off_z, off_xs = 0, 2 * gmlp_w, 2 * gmlp_w + ssd_w

    h = x.reshape(t_len, d)
    out = None
    for layer in range(depth):
        w = w_in[layer]
        w_t = jnp.swapaxes(w, 0, 1)
        w_dt = jnp.pad(w_t[c_dt:c_uv], ((0, V7X_LANES - heads), (0, 0))).astype(BF16)
        p, dt_raw = _inproj(h, mix_norm_g[layer], w_t, w_dt, ssd_conv_w[layer], ssd_conv_b[layer],
                            2 * gmlp_w, ssd_w, c_uv)
        groups = (conv_dim - ssd_w) // (2 * SSD_STATE)
        y_ssd = _ssd(p, dt_raw, ssd_dt_bias[layer], ssd_a_log[layer], ssd_d[layer], ssd_norm_g[layer],
                     groups, off_z, off_xs)
        y_gmlp = _gmlp(p, gmlp_v_norm_g[layer], gmlp_w_s[layer], gmlp_b_s[layer],
                       gmlp_out_norm_g[layer], off_u)
        x1 = _outproj(y_ssd, y_gmlp, w_out[layer].astype(BF16), h)

        xp, idx_pad, gate_pad = _router(x1, ffn_norm_g[layer], w_router[layer], b_router[layer])
        plan = _route_plan(idx_pad[:, :TOP_K], w_router.shape[-1])
        act = _moe_up(xp, plan, w_gate_up[layer], b_gate_up[layer])
        y_rows = _moe_down(act, plan, w_down[layer], b_down[layer])
        last = layer == depth - 1
        assert last, "multi-layer stacking needs an un-normalised combine"
        out = _combine(y_rows, plan["dest"], gate_pad, x1, final_norm_g)
    return out.reshape(bsz, t_len, d)
```

```python
import functools
import math

import jax
import jax.numpy as jnp
from jax import lax
from jax.experimental import pallas as pl
from jax.experimental.pallas import tpu as pltpu

F32, BF16, U32, I32 = jnp.float32, jnp.bfloat16, jnp.uint32, jnp.int32

EPS = 1e-5
SSD_STATE = 128
SSD_CONV = 4
CHUNK = 128
TOP_K = 4
SSD_ROWS = 1024
SWIGLU_ALPHA = 1.702
SWIGLU_LIMIT = 7.0

V7X_LANES = 128
V7X_SUBLANES = 8
V7X_VMEM_BYTES = 64 * 1024 * 1024
VMEM_LIMIT = V7X_VMEM_BYTES - 8 * 1024 * 1024

MOE_BLK = 128
MOE_NSB = 12
ISSUE_UNROLL = 8
HI_MASK = 0xFFFF0000


_gcd = math.gcd


def _largest_tile(total, cap, quantum=V7X_LANES):
    best = None
    t = quantum
    while t <= min(total, cap):
        if total % t == 0:
            best = t
        t += quantum
    assert best is not None, (total, cap, quantum)
    return best


def _params(semantics):
    return pltpu.CompilerParams(dimension_semantics=semantics, vmem_limit_bytes=VMEM_LIMIT)


def _silu(x):
    return x * jax.nn.sigmoid(x)


def _gelu(x):
    return 0.5 * x * (1.0 + lax.erf(x * (2.0 ** -0.5)))


def _softplus(x):
    return jnp.maximum(x, 0.0) + jnp.log1p(jnp.exp(-jnp.abs(x)))


def _pack_pair(lo, hi):
    lo_bits = lax.bitcast_convert_type(lo.astype(BF16).astype(F32), U32)
    hi_bits = lax.bitcast_convert_type(hi.astype(BF16).astype(F32), U32)
    return (lo_bits >> 16) | (hi_bits & jnp.uint32(HI_MASK))


def _unpack_pair(u):
    lo = lax.bitcast_convert_type(u << 16, F32)
    hi = lax.bitcast_convert_type(u & jnp.uint32(HI_MASK), F32)
    return lo, hi


NT_DIMS = (((1,), (1,)), ((), ()))


def _inproj_kernel(x_hbm, g_ref, w_ref, wdt_ref, cw_ref, cb_ref, p_ref, dt_ref, xbuf, xn_ref, halo_ref, sem,
                   *, n_gelu, n_silu):
    i = pl.program_id(0)
    j = pl.program_id(1)
    tm = xbuf.shape[0]
    rows = _largest_tile(tm, 128, V7X_SUBLANES)
    tail = halo_ref.shape[1]

    def x_copy(ii):
        return pltpu.make_async_copy(x_hbm.at[pl.ds(pl.multiple_of(ii * tm, tm), tm), :], xbuf, sem.at[0])

    @pl.when(j == 0)
    def _():
        @pl.when(i == 0)
        def _():
            x_copy(0).start()
            halo_ref[...] = jnp.zeros(halo_ref.shape, F32)

        x_copy(i).wait()

        def norm(r, carry):
            sl = pl.ds(pl.multiple_of(r * rows, rows), rows)
            x = xbuf[sl, :]
            ms = jnp.mean(x * x, axis=-1, keepdims=True)
            xn = (x * lax.rsqrt(ms + EPS) * g_ref[...]).astype(BF16)
            xn_ref[sl, :] = xn
            dt_ref[sl, :] = lax.dot_general(xn, wdt_ref[...], NT_DIMS, preferred_element_type=F32)
            return carry

        lax.fori_loop(0, tm // rows, norm, 0)

        @pl.when(i + 1 < pl.num_programs(0))
        def _():
            x_copy(i + 1).start()

    def project():
        return lax.dot_general(xn_ref[...], w_ref[...].astype(BF16), NT_DIMS, preferred_element_type=F32)

    @pl.when(j < n_gelu)
    def _():
        p_ref[...] = _gelu(project()).astype(BF16)

    @pl.when((j >= n_gelu) & (j < n_gelu + n_silu))
    def _():
        p_ref[...] = _silu(project()).astype(BF16)

    @pl.when(j >= n_gelu + n_silu)
    def _():
        x = project()
        jc = j - (n_gelu + n_silu)
        ext = jnp.concatenate([halo_ref[jc], x], axis=0)
        acc = cb_ref[...] + cw_ref[SSD_CONV - 1:SSD_CONV, :] * ext
        for s in range(1, SSD_CONV):
            acc = acc + cw_ref[SSD_CONV - 1 - s:SSD_CONV - s, :] * pltpu.roll(ext, s, axis=0)
        p_ref[...] = _silu(acc[tail:, :]).astype(BF16)
        halo_ref[jc] = x[tm - tail:tm, :]


def _inproj(x, g, w_t, w_dt, conv_w, conv_b, n_gelu_cols, n_silu_cols, row_gelu):
    t_len, d = x.shape
    n_conv_cols = conv_b.shape[0]
    n_out = n_gelu_cols + n_silu_cols + n_conv_cols
    assert row_gelu + n_gelu_cols == w_t.shape[0] and row_gelu >= n_silu_cols + n_conv_cols
    tm = _largest_tile(t_len, 1024, V7X_SUBLANES)
    tn = _largest_tile(_gcd(_gcd(n_gelu_cols, n_silu_cols), n_conv_cols), 512)
    n_gelu, n_silu, n_conv = n_gelu_cols // tn, n_silu_cols // tn, n_conv_cols // tn
    tail = 2 * V7X_SUBLANES
    assert row_gelu % tail == 0 and tn % tail == 0
    conv_map = lambda i, j: (0, jnp.maximum(j - (n_gelu + n_silu), 0))
    w_map = lambda i, j: (
        tail * jnp.where(j < n_gelu, row_gelu // tail + j * (tn // tail), (j - n_gelu) * (tn // tail)), 0)
    return pl.pallas_call(
        functools.partial(_inproj_kernel, n_gelu=n_gelu, n_silu=n_silu),
        grid=(t_len // tm, n_out // tn),
        in_specs=[
            pl.BlockSpec(memory_space=pl.ANY),
            pl.BlockSpec((1, d), lambda i, j: (0, 0)),
            pl.BlockSpec((pl.Element(tn), pl.Element(d)), w_map),
            pl.BlockSpec((V7X_LANES, d), lambda i, j: (0, 0)),
            pl.BlockSpec((SSD_CONV, tn), conv_map),
            pl.BlockSpec((1, tn), conv_map),
        ],
        out_specs=[
            pl.BlockSpec((tm, tn), lambda i, j: (i, j)),
            pl.BlockSpec((tm, V7X_LANES), lambda i, j: (i, 0)),
        ],
        out_shape=[
            jax.ShapeDtypeStruct((t_len, n_out), BF16),
            jax.ShapeDtypeStruct((t_len, V7X_LANES), F32),
        ],
        scratch_shapes=[
            pltpu.VMEM((tm, d), F32),
            pltpu.VMEM((tm, d), BF16),
            pltpu.VMEM((n_conv, tail, tn), F32),
            pltpu.SemaphoreType.DMA((1,)),
        ],
        compiler_params=_params(("arbitrary", "arbitrary")),
        name="inproj",
    )(x, g.reshape(1, d), w_t, w_dt, conv_w, conv_b.reshape(1, n_conv_cols))


def _split3(x):
    hi = x.astype(BF16)
    r1 = x - hi.astype(F32)
    mid = r1.astype(BF16)
    lo = (r1 - mid.astype(F32)).astype(BF16)
    return hi, mid, lo


def _ssd_kernel(sz_ref, xs_ref, b_ref, c_ref, dt_ref, dtb_ref, alog_ref, dsk_ref, ng_ref, y_ref, state_ref,
                *, hpg, hd):
    width = hpg * hd
    lanes = V7X_LANES
    per = lanes // hd

    @pl.when(pl.program_id(1) == 0)
    def _():
        state_ref[...] = jnp.zeros(state_ref.shape, F32)

    ri = lax.broadcasted_iota(I32, (CHUNK, CHUNK), 0)
    ci = lax.broadcasted_iota(I32, (CHUNK, CHUNK), 1)
    causal = ri >= ci
    tril = causal.astype(BF16)
    neg_a = -jnp.exp(alog_ref[...])

    def expand(q):
        cols = []
        for j in range(width // lanes):
            blk = jnp.broadcast_to(q[:, j * per:j * per + 1], (CHUNK, lanes))
            for t in range(1, per):
                nxt = jnp.broadcast_to(q[:, j * per + t:j * per + t + 1], (CHUNK, lanes))
                blk = jnp.where(ci >= t * hd, nxt, blk)
            cols.append(blk)
        return jnp.concatenate(cols, axis=1)

    def chunk(k, carry):
        rows = pl.ds(pl.multiple_of(k * CHUNK, CHUNK), CHUNK)
        xs = xs_ref[rows, :].astype(F32)
        bm = b_ref[rows, :]
        cm = c_ref[rows, :]

        dt = _softplus(dt_ref[rows, :] + dtb_ref[...])
        a_dt = dt * neg_a
        acum = sum(jnp.dot(tril, piece, preferred_element_type=F32) for piece in _split3(a_dt))
        acum_t = acum.T
        acum_f = expand(acum)
        last = acum_f[CHUNK - 1:CHUNK, :]
        xdt = xs * expand(dt)

        cb = lax.dot_general(cm, bm, (((1,), (1,)), ((), ())), preferred_element_type=F32)
        y_cols = []
        for j in range(width // lanes):
            x_col = xdt[:, j * lanes:(j + 1) * lanes]
            y_col = jnp.zeros((CHUNK, lanes), F32)
            for t in range(per):
                h = j * per + t
                seg = acum[:, h:h + 1] - acum_t[h:h + 1, :]
                lmat = jnp.exp(jnp.where(causal, seg, -jnp.inf))
                m_h = (cb * lmat).astype(BF16)
                mask = (ci >= t * hd) & (ci < (t + 1) * hd)
                x_h = jnp.where(mask, x_col, 0.0).astype(BF16)
                y_col = y_col + jnp.dot(m_h, x_h, preferred_element_type=F32)
            y_cols.append(y_col)
        y_diag = jnp.concatenate(y_cols, axis=1)

        state = state_ref[...]
        y_off = jnp.dot(cm, state.astype(BF16), preferred_element_type=F32) * jnp.exp(acum_f)
        x_dec = (xdt * jnp.exp(last - acum_f)).astype(BF16)
        s_new = lax.dot_general(bm, x_dec, (((0,), (0,)), ((), ())), preferred_element_type=F32)
        state_ref[...] = state * jnp.exp(last) + s_new

        yg = (y_diag + y_off + xs * dsk_ref[...]) * sz_ref[rows, :].astype(F32)
        yn = yg * lax.rsqrt(jnp.mean(yg * yg, axis=-1, keepdims=True) + EPS) * ng_ref[...]
        y_ref[rows, :] = yn.astype(BF16)
        return carry

    lax.fori_loop(0, y_ref.shape[0] // CHUNK, chunk, 0, unroll=8)


def _ssd(p, dt_raw, dt_bias, a_log, d_skip, norm_g, groups, off_z, off_xs):
    t_len = p.shape[0]
    heads = dt_bias.shape[0]
    width = norm_g.shape[0]
    hd = width // heads
    hpg = heads // groups
    gw = hpg * hd
    n_st = SSD_STATE
    rows = _largest_tile(t_len, SSD_ROWS, CHUNK)
    assert V7X_LANES % hd == 0 and gw % V7X_LANES == 0 and hpg <= V7X_LANES
    assert width // groups == gw
    off_b = off_xs + width
    off_c = off_b + groups * n_st
    assert off_z % gw == 0 and off_xs % gw == 0 and off_b % n_st == 0

    def per_group_lanes(v):
        v = v.reshape(groups, 1, hpg)
        return jnp.pad(v, ((0, 0), (0, 0), (0, V7X_LANES - hpg)))

    dt_g = dt_raw[:, :heads].reshape(t_len, groups, hpg).transpose(1, 0, 2)
    dt_g = jnp.pad(dt_g, ((0, 0), (0, 0), (0, V7X_LANES - hpg)))

    grp = lambda g, c: (g, 0, 0)
    return pl.pallas_call(
        functools.partial(_ssd_kernel, hpg=hpg, hd=hd),
        grid=(groups, t_len // rows),
        in_specs=[
            pl.BlockSpec((rows, gw), lambda g, c: (c, off_z // gw + g)),
            pl.BlockSpec((rows, gw), lambda g, c: (c, off_xs // gw + g)),
            pl.BlockSpec((rows, n_st), lambda g, c: (c, off_b // n_st + g)),
            pl.BlockSpec((rows, n_st), lambda g, c: (c, off_c // n_st + g)),
            pl.BlockSpec((None, rows, V7X_LANES), lambda g, c: (g, c, 0)),
            pl.BlockSpec((None, 1, V7X_LANES), grp),
            pl.BlockSpec((None, 1, V7X_LANES), grp),
            pl.BlockSpec((1, gw), lambda g, c: (0, g)),
            pl.BlockSpec((1, gw), lambda g, c: (0, g)),
        ],
        out_specs=pl.BlockSpec((rows, gw), lambda g, c: (c, g)),
        out_shape=jax.ShapeDtypeStruct((t_len, width), BF16),
        scratch_shapes=[pltpu.VMEM((n_st, gw), F32)],
        compiler_params=_params(("arbitrary", "arbitrary")),
        name="ssd",
    )(p, p, p, p, dt_g, per_group_lanes(dt_bias), per_group_lanes(a_log),
      jnp.repeat(d_skip, hd).reshape(1, width), norm_g.reshape(1, width))


def _row_sums(x):
    ones = jnp.ones((x.shape[1], V7X_LANES), BF16)
    hi = x.astype(BF16)
    lo = (x - hi.astype(F32)).astype(BF16)
    return (jnp.dot(hi, ones, preferred_element_type=F32) + jnp.dot(lo, ones, preferred_element_type=F32))


def _gmlp_kernel(u_ref, v_ref, vg_ref, ws_ref, bs_ref, og_ref, y_ref, yscr_ref, *, heads, hd):
    lanes = V7X_LANES
    reps = hd // lanes
    ri = lax.broadcasted_iota(I32, (CHUNK, CHUNK), 0)
    ci = lax.broadcasted_iota(I32, (CHUNK, CHUNK), 1)
    causal = ri >= ci
    ssq = jnp.zeros((CHUNK, lanes), F32)
    invs = []
    for h in range(heads):
        v_h = v_ref[:, h * hd:(h + 1) * hd].astype(F32)
        invs.append(lax.rsqrt(_row_sums(v_h * v_h) / hd + EPS))
    for h in range(heads):
        cols = slice(h * hd, (h + 1) * hd)
        u_h = u_ref[:, cols].astype(F32)
        v_h = v_ref[:, cols].astype(F32)
        v_n = v_h * jnp.tile(invs[h], (1, reps)) * vg_ref[:, cols]
        w_h = jnp.where(causal, ws_ref[h], 0.0).astype(BF16)
        sv = jnp.dot(w_h, v_n.astype(BF16), preferred_element_type=F32) + jnp.tile(bs_ref[h], (1, reps))
        y_h = u_h * sv
        yscr_ref[:, cols] = y_h
        y_sq = y_h * y_h
        for r in range(reps):
            ssq = ssq + y_sq[:, r * lanes:(r + 1) * lanes]
    scale = lax.rsqrt(_row_sums(ssq) / (heads * hd) + EPS)
    y_ref[...] = (yscr_ref[...] * jnp.tile(scale, (1, heads * reps)) * og_ref[...]).astype(BF16)


def _gmlp(p, v_norm_g, w_s, b_s, out_norm_g, off_u):
    t_len = p.shape[0]
    heads = w_s.shape[0]
    width = out_norm_g.shape[0]
    hd = width // heads
    assert w_s.shape[1] == CHUNK and off_u % width == 0 and hd % V7X_LANES == 0
    b_lanes = jnp.broadcast_to(b_s[:, :, None], (heads, CHUNK, V7X_LANES))
    return pl.pallas_call(
        functools.partial(_gmlp_kernel, heads=heads, hd=hd),
        grid=(t_len // CHUNK,),
        in_specs=[
            pl.BlockSpec((CHUNK, width), lambda c: (c, off_u // width)),
            pl.BlockSpec((CHUNK, width), lambda c: (c, off_u // width + 1)),
            pl.BlockSpec((1, width), lambda c: (0, 0)),
            pl.BlockSpec((heads, CHUNK, CHUNK), lambda c: (0, 0, 0)),
            pl.BlockSpec((heads, CHUNK, V7X_LANES), lambda c: (0, 0, 0)),
            pl.BlockSpec((1, width), lambda c: (0, 0)),
        ],
        out_specs=pl.BlockSpec((CHUNK, width), lambda c: (c, 0)),
        out_shape=jax.ShapeDtypeStruct((t_len, width), BF16),
        scratch_shapes=[pltpu.VMEM((CHUNK, width), F32)],
        compiler_params=_params(("arbitrary",)),
        name="gmlp",
    )(p, p, v_norm_g.reshape(1, width), w_s, b_lanes, out_norm_g.reshape(1, width))


def _outproj_kernel(ys_ref, yg_ref, wa_ref, wb_ref, x_ref, o_ref):
    acc = jnp.dot(ys_ref[...], wa_ref[...], preferred_element_type=F32)
    acc = acc + jnp.dot(yg_ref[...], wb_ref[...], preferred_element_type=F32)
    o_ref[...] = x_ref[...] + acc


def _outproj(y_ssd, y_gmlp, w_out, x):
    t_len, d = x.shape
    ka, kb = y_ssd.shape[1], y_gmlp.shape[1]
    assert ka == kb and w_out.shape[0] == ka + kb
    tm = _largest_tile(t_len, 512, V7X_SUBLANES)
    tn = _largest_tile(d, 512)
    return pl.pallas_call(
        _outproj_kernel,
        grid=(t_len // tm, d // tn),
        in_specs=[
            pl.BlockSpec((tm, ka), lambda i, j: (i, 0)),
            pl.BlockSpec((tm, kb), lambda i, j: (i, 0)),
            pl.BlockSpec((ka, tn), lambda i, j: (0, j)),
            pl.BlockSpec((kb, tn), lambda i, j: (1, j)),
            pl.BlockSpec((tm, tn), lambda i, j: (i, j)),
        ],
        out_specs=pl.BlockSpec((tm, tn), lambda i, j: (i, j)),
        out_shape=jax.ShapeDtypeStruct((t_len, d), F32),
        compiler_params=_params(("arbitrary", "arbitrary")),
        name="outproj",
    )(y_ssd, y_gmlp, w_out, w_out, x)


def _router_kernel(x_ref, g_ref, wr_ref, br_ref, xp_ref, idx_ref, gate_ref, *, n_exp):
    x = x_ref[...]
    tm, d = x.shape
    half = d // 2
    ms = jnp.mean(x * x, axis=-1, keepdims=True)
    xn = x * lax.rsqrt(ms + EPS) * g_ref[...]
    xp_ref[...] = _pack_pair(xn[:, :half], xn[:, half:])

    logits = jnp.dot(xn.astype(BF16), wr_ref[...], preferred_element_type=F32) + br_ref[...]
    lane = lax.broadcasted_iota(I32, (tm, V7X_LANES), 1)
    vals = jnp.where(lane < n_exp, logits, -jnp.inf)
    idx_out = jnp.zeros((tm, V7X_LANES), I32)
    val_out = jnp.full((tm, V7X_LANES), -jnp.inf, F32)
    for k in range(TOP_K):
        m = jnp.max(vals, axis=-1, keepdims=True)
        sel = jnp.min(jnp.where(vals == m, lane, V7X_LANES), axis=-1, keepdims=True)
        idx_out = jnp.where(lane == k, sel, idx_out)
        val_out = jnp.where(lane == k, m, val_out)
        vals = jnp.where(lane == sel, -jnp.inf, vals)
    e = jnp.exp(val_out - jnp.max(val_out, axis=-1, keepdims=True))
    idx_ref[...] = idx_out
    gate_ref[...] = e / jnp.sum(e, axis=-1, keepdims=True)


def _router(x1, g, w_router, b_router):
    t_len, d = x1.shape
    n_exp = w_router.shape[1]
    assert n_exp <= V7X_LANES and TOP_K <= n_exp
    tm = _largest_tile(t_len, 256, V7X_SUBLANES)
    wr = jnp.pad(w_router, ((0, 0), (0, V7X_LANES - n_exp))).astype(BF16)
    br = jnp.pad(b_router, (0, V7X_LANES - n_exp)).reshape(1, V7X_LANES)
    return pl.pallas_call(
        functools.partial(_router_kernel, n_exp=n_exp),
        grid=(t_len // tm,),
        in_specs=[
            pl.BlockSpec((tm, d), lambda i: (i, 0)),
            pl.BlockSpec((1, d), lambda i: (0, 0)),
            pl.BlockSpec((d, V7X_LANES), lambda i: (0, 0)),
            pl.BlockSpec((1, V7X_LANES), lambda i: (0, 0)),
        ],
        out_specs=[
            pl.BlockSpec((tm, d // 2), lambda i: (i, 0)),
            pl.BlockSpec((tm, V7X_LANES), lambda i: (i, 0)),
            pl.BlockSpec((tm, V7X_LANES), lambda i: (i, 0)),
        ],
        out_shape=[
            jax.ShapeDtypeStruct((t_len, d // 2), U32),
            jax.ShapeDtypeStruct((t_len, V7X_LANES), I32),
            jax.ShapeDtypeStruct((t_len, V7X_LANES), F32),
        ],
        compiler_params=_params(("arbitrary",)),
        name="router",
    )(x1, g.reshape(1, d), wr, br)


def _row_tokens_kernel(dest_ref, tok_ref, *, top_k):
    per = ISSUE_UNROLL // top_k

    def clear(q, carry):
        for u in range(ISSUE_UNROLL):
            tok_ref[q * ISSUE_UNROLL + u] = 0
        return carry

    def place(q, carry):
        for u in range(ISSUE_UNROLL):
            tok_ref[dest_ref[q * ISSUE_UNROLL + u]] = q * per + u // top_k
        return carry

    lax.fori_loop(0, tok_ref.shape[0] // ISSUE_UNROLL, clear, 0)
    lax.fori_loop(0, dest_ref.shape[0] // ISSUE_UNROLL, place, 0)


def _row_tokens(dest, n_rows, top_k):
    assert ISSUE_UNROLL % top_k == 0 and n_rows % ISSUE_UNROLL == 0 and dest.shape[0] % ISSUE_UNROLL == 0
    return pl.pallas_call(
        functools.partial(_row_tokens_kernel, top_k=top_k),
        in_specs=[pl.BlockSpec(memory_space=pltpu.SMEM)],
        out_specs=pl.BlockSpec(memory_space=pltpu.SMEM),
        out_shape=jax.ShapeDtypeStruct((n_rows,), I32),
        name="row_tokens",
    )(dest)


def _route_plan(top_idx, n_exp):
    t_len, top_k = top_idx.shape
    n_assign = t_len * top_k
    e = top_idx.reshape(n_assign)
    onehot = (e[:, None] == jnp.arange(n_exp, dtype=I32)[None, :]).astype(I32)
    csum = jnp.cumsum(onehot, axis=0)
    rank = jnp.sum(onehot * csum, axis=1) - 1
    counts = csum[-1]
    nb = (counts + MOE_BLK - 1) // MOE_BLK
    sb_end = jnp.cumsum(nb)
    sb_start = sb_end - nb
    dest = sb_start[e] * MOE_BLK + rank
    n_sb = n_assign // MOE_BLK + n_exp
    n_rows = n_sb * MOE_BLK
    row_tok = _row_tokens(dest.astype(I32), n_rows, top_k)

    nv = (nb + MOE_NSB - 1) // MOE_NSB
    v_end = jnp.cumsum(nv)
    v_start = v_end - nv
    n_pass = -(-n_sb // MOE_NSB) + n_exp
    vid = jnp.arange(n_pass, dtype=I32)
    ve = jnp.minimum(jnp.searchsorted(v_end, vid, side='right'), n_exp - 1).astype(I32)
    q = vid - v_start[ve]
    valid = vid < v_end[-1]
    vs = jnp.where(valid, sb_start[ve] + q * MOE_NSB, 0).astype(I32)
    vn = jnp.where(valid, jnp.clip(nb[ve] - q * MOE_NSB, 0, MOE_NSB), 0).astype(I32)
    last_e = ve[jnp.maximum(v_end[-1] - 1, 0)]
    ve = jnp.where(valid, ve, last_e).astype(I32)
    used = sb_end[-1:].astype(I32)
    n_live = jnp.maximum(v_end[-1], 1).astype(I32)
    return dict(row_tok=row_tok, dest=dest.astype(I32), ve=ve, vs=vs, vn=vn, used=used, n_rows=n_rows,
                n_pass=n_live)


def _for_each_block(n, compute, emit):
    def quad(q, carry):
        b = 4 * q
        first = compute(b, 2)
        second = compute(b + 2, 2)
        emit(b, 2, first)
        emit(b + 2, 2, second)
        return carry

    quads = lax.shift_right_logical(n, 2)
    lax.fori_loop(0, quads, quad, 0)

    @pl.when((n & 2) != 0)
    def _():
        emit(4 * quads, 2, compute(4 * quads, 2))

    @pl.when((n & 1) != 0)
    def _():
        emit(n - 1, 1, compute(n - 1, 1))


def _zero_tail(obuf, n_used, n_sb, n_j, out_copy):
    tile = obuf.shape[2]
    obuf[0, 0:MOE_BLK, :] = jnp.zeros((MOE_BLK, tile), obuf.dtype)

    def start(sb, carry):
        for jj in range(n_j):
            out_copy(0, 0, pl.multiple_of(sb * MOE_BLK, MOE_BLK), jj * tile).start()
        return carry

    def wait(sb, carry):
        for jj in range(n_j):
            out_copy(0, 0, 0, 0).wait()
        return carry

    lax.fori_loop(n_used, n_sb, start, 0)
    lax.fori_loop(n_used, n_sb, wait, 0)


def _moe_up_kernel(ve_ref, vs_ref, vn_ref, used_ref, tok_ref, xp_hbm, wg_ref, wl_ref, bg_ref, bl_ref, act_hbm,
                   xbuf, obuf, xsem, osem, ocnt, *, tf, n_j):
    v = pl.program_id(0)
    j = pl.program_id(1)
    n_pass = pl.num_programs(0)
    half = xbuf.shape[2]
    slot = v % 2
    n_sub = vn_ref[v]

    def row_copy(tok, sl, r0, u):
        window = xbuf.at[sl, pl.ds(r0, ISSUE_UNROLL), :]
        return pltpu.make_async_copy(xp_hbm.at[pl.ds(tok, 1), :], window.at[pl.ds(u, 1), :], xsem.at[sl])

    def gather_start(vv, sl):
        base = vs_ref[vv] * MOE_BLK

        def body(q, carry):
            r0 = pl.multiple_of(q * ISSUE_UNROLL, ISSUE_UNROLL)
            for u in range(ISSUE_UNROLL):
                row_copy(tok_ref[base + r0 + u], sl, r0, u).start()
            return carry

        lax.fori_loop(0, vn_ref[vv] * (MOE_BLK // ISSUE_UNROLL), body, 0)

    def gather_wait(vv, sl):
        def body(i, carry):
            r0 = pl.multiple_of(i * MOE_BLK, MOE_BLK)
            pltpu.make_async_copy(xp_hbm.at[pl.ds(0, MOE_BLK), :], xbuf.at[sl, pl.ds(r0, MOE_BLK), :],
                                  xsem.at[sl]).wait()
            return carry

        lax.fori_loop(0, vn_ref[vv], body, 0)

    def out_copy(p, r0, row, col):
        return pltpu.make_async_copy(obuf.at[p, pl.ds(r0, MOE_BLK), :],
                                     act_hbm.at[pl.ds(row, MOE_BLK), pl.ds(col, tf)], osem.at[p])

    def drain(p):
        def body(i, carry):
            out_copy(p, 0, 0, 0).wait()
            return carry

        lax.fori_loop(0, ocnt[p], body, 0)
        ocnt[p] = 0

    @pl.when(j == 0)
    def _():
        @pl.when(v == 0)
        def _():
            ocnt[0] = 0
            ocnt[1] = 0
            gather_start(0, 0)

        gather_wait(v, slot)

    step = v * n_j + j
    p = step % 2
    drain(p)

    @pl.when(n_sub > 0)
    def _():
        nxt = jnp.minimum(v + 1, n_pass - 1)
        n_next = jnp.where(v + 1 < n_pass, vn_ref[nxt], 0)
        groups = n_next * (MOE_BLK // (ISSUE_UNROLL * n_j))
        first = j * groups
        base = vs_ref[nxt] * MOE_BLK

        def request(q, carry):
            r0 = pl.multiple_of((first + q) * ISSUE_UNROLL, ISSUE_UNROLL)
            for u in range(ISSUE_UNROLL):
                row_copy(tok_ref[base + r0 + u], 1 - slot, r0, u).start()
            return carry

        lax.fori_loop(0, groups, request, 0)

        def compute(b, nb):
            r0 = pl.multiple_of(b * MOE_BLK, MOE_BLK)
            lo, hi = _unpack_pair(xbuf[slot, pl.ds(r0, nb * MOE_BLK), :])
            w_lo = jnp.concatenate([wg_ref[0:half, :], wl_ref[0:half, :]], axis=1).astype(BF16)
            w_hi = jnp.concatenate([wg_ref[half:2 * half, :], wl_ref[half:2 * half, :]], axis=1).astype(BF16)
            h = jnp.dot(lo.astype(BF16), w_lo, preferred_element_type=F32)
            h = h + jnp.dot(hi.astype(BF16), w_hi, preferred_element_type=F32)
            glu = jnp.minimum(h[:, 0:tf] + bg_ref[...], SWIGLU_LIMIT)
            lin = jnp.clip(h[:, tf:2 * tf] + bl_ref[...], -SWIGLU_LIMIT, SWIGLU_LIMIT)
            return (glu * jax.nn.sigmoid(SWIGLU_ALPHA * glu) * (lin + 1.0)).astype(BF16)

        def emit(b, nb, act):
            r0 = pl.multiple_of(b * MOE_BLK, MOE_BLK)
            obuf[p, pl.ds(r0, nb * MOE_BLK), :] = act
            for t in range(nb):
                row = pl.multiple_of((vs_ref[v] + b + t) * MOE_BLK, MOE_BLK)
                out_copy(p, r0 + t * MOE_BLK, row, pl.multiple_of(j * tf, tf)).start()

        _for_each_block(n_sub, compute, emit)
        ocnt[p] = n_sub

    @pl.when(step == n_pass * n_j - 1)
    def _():
        drain(0)
        drain(1)
        _zero_tail(obuf, used_ref[0], act_hbm.shape[0] // MOE_BLK, n_j, out_copy)


def _moe_up(xp, plan, w_gate_up, b_gate_up):
    n_exp, d, two_f = w_gate_up.shape
    f = two_f // 2
    half = xp.shape[1]
    assert half * 2 == d
    tf = _largest_tile(f, 256)
    n_j = f // tf
    rows_pass = MOE_NSB * MOE_BLK
    assert MOE_BLK % (ISSUE_UNROLL * n_j) == 0

    def w_map(off):
        def index_map(v, j, ve, vs, vn, used, tok):
            jj = jnp.where(vn[v] > 0, j, n_j - 1)
            return (ve[v], 0, off + jj)
        return index_map

    grid_spec = pltpu.PrefetchScalarGridSpec(
        num_scalar_prefetch=5,
        grid=(plan["n_pass"], n_j),
        in_specs=[
            pl.BlockSpec(memory_space=pl.ANY),
            pl.BlockSpec((None, d, tf), w_map(0)),
            pl.BlockSpec((None, d, tf), w_map(n_j)),
            pl.BlockSpec((None, 1, tf), w_map(0)),
            pl.BlockSpec((None, 1, tf), w_map(n_j)),
        ],
        out_specs=pl.BlockSpec(memory_space=pl.ANY),
        scratch_shapes=[
            pltpu.VMEM((2, rows_pass, half), U32),
            pltpu.VMEM((2, rows_pass, tf), BF16),
            pltpu.SemaphoreType.DMA((2,)),
            pltpu.SemaphoreType.DMA((2,)),
            pltpu.SMEM((2,), I32),
        ],
    )
    b3 = b_gate_up.reshape(n_exp, 1, two_f)
    return pl.pallas_call(
        functools.partial(_moe_up_kernel, tf=tf, n_j=n_j),
        grid_spec=grid_spec,
        out_shape=jax.ShapeDtypeStruct((plan["n_rows"], f), BF16),
        compiler_params=_params(("arbitrary", "arbitrary")),
        name="moe_up",
    )(plan["ve"], plan["vs"], plan["vn"], plan["used"], plan["row_tok"], xp, w_gate_up, w_gate_up, b3, b3)


def _moe_down_kernel(ve_ref, vs_ref, vn_ref, used_ref, act_hbm, wa_ref, wb_ref, ba_ref, bb_ref, y_hbm,
                     abuf, obuf, asem, osem, ocnt, *, tn):
    v = pl.program_id(0)
    j = pl.program_id(1)
    n_pass = pl.num_programs(0)
    n_j = pl.num_programs(1)
    slot = v % 2
    n_sub = vn_ref[v]

    def in_copy(row, sl, r0):
        return pltpu.make_async_copy(act_hbm.at[pl.ds(row, MOE_BLK), :],
                                     abuf.at[sl, pl.ds(r0, MOE_BLK), :], asem.at[sl])

    def load_start(vv, sl):
        def body(i, carry):
            r0 = pl.multiple_of(i * MOE_BLK, MOE_BLK)
            in_copy(pl.multiple_of((vs_ref[vv] + i) * MOE_BLK, MOE_BLK), sl, r0).start()
            return carry

        lax.fori_loop(0, vn_ref[vv], body, 0)

    def load_wait(vv, sl):
        def body(i, carry):
            in_copy(0, sl, pl.multiple_of(i * MOE_BLK, MOE_BLK)).wait()
            return carry

        lax.fori_loop(0, vn_ref[vv], body, 0)

    def out_copy(p, r0, row, col):
        return pltpu.make_async_copy(obuf.at[p, pl.ds(r0, MOE_BLK), :],
                                     y_hbm.at[pl.ds(row, MOE_BLK), pl.ds(col, tn)], osem.at[p])

    def drain(p):
        def body(i, carry):
            out_copy(p, 0, 0, 0).wait()
            return carry

        lax.fori_loop(0, ocnt[p], body, 0)
        ocnt[p] = 0

    @pl.when(j == 0)
    def _():
        @pl.when(v == 0)
        def _():
            ocnt[0] = 0
            ocnt[1] = 0
            load_start(0, 0)

        @pl.when(v + 1 < n_pass)
        def _():
            load_start(v + 1, 1 - slot)

        load_wait(v, slot)

    step = v * n_j + j
    p = step % 2
    drain(p)

    @pl.when(n_sub > 0)
    def _():
        def compute(b, nb):
            r0 = pl.multiple_of(b * MOE_BLK, MOE_BLK)
            w = jnp.concatenate([wa_ref[...], wb_ref[...]], axis=1).astype(BF16)
            y = jnp.dot(abuf[slot, pl.ds(r0, nb * MOE_BLK), :], w, preferred_element_type=F32)
            return _pack_pair(y[:, 0:tn] + ba_ref[...], y[:, tn:2 * tn] + bb_ref[...])

        def emit(b, nb, packed):
            r0 = pl.multiple_of(b * MOE_BLK, MOE_BLK)
            obuf[p, pl.ds(r0, nb * MOE_BLK), :] = packed
            for t in range(nb):
                row = pl.multiple_of((vs_ref[v] + b + t) * MOE_BLK, MOE_BLK)
                out_copy(p, r0 + t * MOE_BLK, row, pl.multiple_of(j * tn, tn)).start()

        _for_each_block(n_sub, compute, emit)
        ocnt[p] = n_sub

    @pl.when(step == n_pass * n_j - 1)
    def _():
        drain(0)
        drain(1)
        _zero_tail(obuf, used_ref[0], y_hbm.shape[0] // MOE_BLK, n_j, out_copy)


def _moe_down(act, plan, w_down, b_down):
    n_exp, f, d = w_down.shape
    half = d // 2
    tn = _largest_tile(half, 512)
    n_j = half // tn
    rows_pass = MOE_NSB * MOE_BLK

    def w_map(off):
        def index_map(v, j, ve, vs, vn, used):
            jj = jnp.where(vn[v] > 0, j, n_j - 1)
            return (ve[v], 0, off + jj)
        return index_map

    grid_spec = pltpu.PrefetchScalarGridSpec(
        num_scalar_prefetch=4,
        grid=(plan["n_pass"], n_j),
        in_specs=[
            pl.BlockSpec(memory_space=pl.ANY),
            pl.BlockSpec((None, f, tn), w_map(0)),
            pl.BlockSpec((None, f, tn), w_map(n_j)),
            pl.BlockSpec((None, 1, tn), w_map(0)),
            pl.BlockSpec((None, 1, tn), w_map(n_j)),
        ],
        out_specs=pl.BlockSpec(memory_space=pl.ANY),
        scratch_shapes=[
            pltpu.VMEM((2, rows_pass, f), BF16),
            pltpu.VMEM((2, rows_pass, tn), U32),
            pltpu.SemaphoreType.DMA((2,)),
            pltpu.SemaphoreType.DMA((2,)),
            pltpu.SMEM((2,), I32),
        ],
    )
    b3 = b_down.reshape(n_exp, 1, d)
    return pl.pallas_call(
        functools.partial(_moe_down_kernel, tn=tn),
        grid_spec=grid_spec,
        out_shape=jax.ShapeDtypeStruct((plan["n_rows"], half), U32),
        compiler_params=_params(("arbitrary", "arbitrary")),
        name="moe_down",
    )(plan["ve"], plan["vs"], plan["vn"], plan["used"], act, w_down, w_down, b3, b3)


def _combine_kernel(dest_ref, y_hbm, gate_ref, x_ref, g_ref, o_ref, ybuf, sem):
    i = pl.program_id(0)
    n_i = pl.num_programs(0)
    tm, d = x_ref.shape
    half = d // 2
    slot = i % 2

    def row_copy(row, sl, k, t0, u):
        window = ybuf.at[sl, k, pl.ds(t0, ISSUE_UNROLL), :]
        return pltpu.make_async_copy(y_hbm.at[pl.ds(row, 1), :], window.at[pl.ds(u, 1), :], sem.at[sl])

    def request(ii, sl, q):
        t0 = pl.multiple_of(q * ISSUE_UNROLL, ISSUE_UNROLL)
        for u in range(ISSUE_UNROLL):
            for k in range(TOP_K):
                row_copy(dest_ref[(ii * tm + t0 + u) * TOP_K + k], sl, k, t0, u).start()

    def gather_wait(sl):
        for k in range(TOP_K):
            pltpu.make_async_copy(y_hbm.at[pl.ds(0, tm), :], ybuf.at[sl, k], sem.at[sl]).wait()

    def combine(q, prefetch):
        rows = pl.ds(pl.multiple_of(q * ISSUE_UNROLL, ISSUE_UNROLL), ISSUE_UNROLL)
        acc_lo = x_ref[rows, 0:half]
        acc_hi = x_ref[rows, half:d]
        for k in range(TOP_K):
            lo, hi = _unpack_pair(ybuf[slot, k, rows, :])
            gk = gate_ref[rows, k:k + 1]
            acc_lo = acc_lo + gk * lo
            acc_hi = acc_hi + gk * hi
        if prefetch:
            request(i + 1, 1 - slot, q)
        ms = (jnp.sum(acc_lo * acc_lo, axis=-1, keepdims=True)
              + jnp.sum(acc_hi * acc_hi, axis=-1, keepdims=True)) / d
        r = lax.rsqrt(ms + EPS)
        o_ref[rows, 0:half] = acc_lo * r * g_ref[:, 0:half]
        o_ref[rows, half:d] = acc_hi * r * g_ref[:, half:d]

    n_groups = tm // ISSUE_UNROLL

    @pl.when(i == 0)
    def _():
        lax.fori_loop(0, n_groups, lambda q, c: (request(0, 0, q), c)[1], 0)

    gather_wait(slot)

    @pl.when(i + 1 < n_i)
    def _():
        lax.fori_loop(0, n_groups, lambda q, c: (combine(q, True), c)[1], 0, unroll=2)

    @pl.when(i + 1 == n_i)
    def _():
        lax.fori_loop(0, n_groups, lambda q, c: (combine(q, False), c)[1], 0, unroll=2)


def _combine(y_rows, dest, gate, x1, final_g):
    t_len, d = x1.shape
    half = d // 2
    tm = _largest_tile(t_len, 128, V7X_SUBLANES)
    grid_spec = pltpu.PrefetchScalarGridSpec(
        num_scalar_prefetch=1,
        grid=(t_len // tm,),
        in_specs=[
            pl.BlockSpec(memory_space=pl.ANY),
            pl.BlockSpec((tm, V7X_LANES), lambda i, dest: (i, 0)),
            pl.BlockSpec((tm, d), lambda i, dest: (i, 0)),
            pl.BlockSpec((1, d), lambda i, dest: (0, 0)),
        ],
        out_specs=pl.BlockSpec((tm, d), lambda i, dest: (i, 0)),
        scratch_shapes=[
            pltpu.VMEM((2, TOP_K, tm, half), U32),
            pltpu.SemaphoreType.DMA((2,)),
        ],
    )
    return pl.pallas_call(
        _combine_kernel,
        grid_spec=grid_spec,
        out_shape=jax.ShapeDtypeStruct((t_len, d), F32),
        compiler_params=_params(("arbitrary",)),
        name="combine",
    )(dest, y_rows, gate, x1, final_g.reshape(1, d))


def kernel(x, mix_norm_g, w_in, ssd_conv_w, ssd_conv_b, ssd_dt_bias, ssd_a_log, ssd_d, ssd_norm_g,
           gmlp_v_norm_g, gmlp_w_s, gmlp_b_s, gmlp_out_norm_g, w_out, ffn_norm_g, w_router, b_router,
           w_gate_up, b_gate_up, w_down, b_down, final_norm_g):
    bsz, t_len, d = x.shape
    assert bsz == 1
    depth = w_in.shape[0]
    ssd_w = ssd_norm_g.shape[-1]
    heads = ssd_dt_bias.shape[-1]
    gmlp_w = gmlp_out_norm_g.shape[-1]
    conv_dim = ssd_conv_b.shape[-1]
    assert heads <= V7X_LANES
    c_dt = ssd_w + conv_dim
    c_uv = c_dt + heads
    off_u, off_z, off_xs = 0, 2 * gmlp_w, 2 * gmlp_w + ssd_w

    h = x.reshape(t_len, d)
    out = None
    for layer in range(depth):
        w = w_in[layer]
        w_t = jnp.swapaxes(w, 0, 1)
        w_dt = jnp.pad(w_t[c_dt:c_uv], ((0, V7X_LANES - heads), (0, 0))).astype(BF16)
        p, dt_raw = _inproj(h, mix_norm_g[layer], w_t, w_dt, ssd_conv_w[layer], ssd_conv_b[layer],
                            2 * gmlp_w, ssd_w, c_uv)
        groups = (conv_dim - ssd_w) // (2 * SSD_STATE)
        y_ssd = _ssd(p, dt_raw, ssd_dt_bias[layer], ssd_a_log[layer], ssd_d[layer], ssd_norm_g[layer],
                     groups, off_z, off_xs)
        y_gmlp = _gmlp(p, gmlp_v_norm_g[layer], gmlp_w_s[layer], gmlp_b_s[layer],
                       gmlp_out_norm_g[layer], off_u)
        x1 = _outproj(y_ssd, y_gmlp, w_out[layer].astype(BF16), h)

        xp, idx_pad, gate_pad = _router(x1, ffn_norm_g[layer], w_router[layer], b_router[layer])
        plan = _route_plan(idx_pad[:, :TOP_K], w_router.shape[-1])
        act = _moe_up(xp, plan, w_gate_up[layer], b_gate_up[layer])
        y_rows = _moe_down(act, plan, w_down[layer], b_down[layer])
        last = layer == depth - 1
        assert last, "multi-layer stacking needs an un-normalised combine"
        out = _combine(y_rows, plan["dest"], gate_pad, x1, final_norm_g)
    return out.reshape(bsz, t_len, d)
```

```python
import functools
import math

import jax
import jax.numpy as jnp
from jax import lax
from jax.experimental import pallas as pl
from jax.experimental.pallas import tpu as pltpu

F32, BF16, U32, I32 = jnp.float32, jnp.bfloat16, jnp.uint32, jnp.int32

EPS = 1e-5
SSD_STATE = 128
SSD_CONV = 4
CHUNK = 128
TOP_K = 4
SSD_ROWS = 1024
SWIGLU_ALPHA = 1.702
SWIGLU_LIMIT = 7.0

V7X_LANES = 128
V7X_SUBLANES = 8
V7X_VMEM_BYTES = 64 * 1024 * 1024
VMEM_LIMIT = V7X_VMEM_BYTES - 8 * 1024 * 1024

MOE_BLK = 128
MOE_NSB = 12
ISSUE_UNROLL = 8
HI_MASK = 0xFFFF0000


_gcd = math.gcd


def _largest_tile(total, cap, quantum=V7X_LANES):
    best = None
    t = quantum
    while t <= min(total, cap):
        if total % t == 0:
            best = t
        t += quantum
    assert best is not None, (total, cap, quantum)
    return best


def _params(semantics):
    return pltpu.CompilerParams(dimension_semantics=semantics, vmem_limit_bytes=VMEM_LIMIT)


def _silu(x):
    return x * jax.nn.sigmoid(x)


def _gelu(x):
    return 0.5 * x * (1.0 + lax.erf(x * (2.0 ** -0.5)))


def _softplus(x):
    return jnp.maximum(x, 0.0) + jnp.log1p(jnp.exp(-jnp.abs(x)))


def _pack_pair(lo, hi):
    lo_bits = lax.bitcast_convert_type(lo.astype(BF16).astype(F32), U32)
    hi_bits = lax.bitcast_convert_type(hi.astype(BF16).astype(F32), U32)
    return (lo_bits >> 16) | (hi_bits & jnp.uint32(HI_MASK))


def _unpack_pair(u):
    lo = lax.bitcast_convert_type(u << 16, F32)
    hi = lax.bitcast_convert_type(u & jnp.uint32(HI_MASK), F32)
    return lo, hi


NT_DIMS = (((1,), (1,)), ((), ()))


def _inproj_kernel(x_hbm, g_ref, w_ref, wdt_ref, cw_ref, cb_ref, p_ref, dt_ref, xbuf, xn_ref, halo_ref, sem,
                   *, n_gelu, n_silu):
    i = pl.program_id(0)
    j = pl.program_id(1)
    tm = xbuf.shape[0]
    rows = _largest_tile(tm, 128, V7X_SUBLANES)
    tail = halo_ref.shape[1]

    def x_copy(ii):
        return pltpu.make_async_copy(x_hbm.at[pl.ds(pl.multiple_of(ii * tm, tm), tm), :], xbuf, sem.at[0])

    @pl.when(j == 0)
    def _():
        @pl.when(i == 0)
        def _():
            x_copy(0).start()
            halo_ref[...] = jnp.zeros(halo_ref.shape, F32)

        x_copy(i).wait()

        def norm(r, carry):
            sl = pl.ds(pl.multiple_of(r * rows, rows), rows)
            x = xbuf[sl, :]
            ms = jnp.mean(x * x, axis=-1, keepdims=True)
            xn = (x * lax.rsqrt(ms + EPS) * g_ref[...]).astype(BF16)
            xn_ref[sl, :] = xn
            dt_ref[sl, :] = lax.dot_general(xn, wdt_ref[...], NT_DIMS, preferred_element_type=F32)
            return carry

        lax.fori_loop(0, tm // rows, norm, 0)

        @pl.when(i + 1 < pl.num_programs(0))
        def _():
            x_copy(i + 1).start()

    def project():
        return lax.dot_general(xn_ref[...], w_ref[...].astype(BF16), NT_DIMS, preferred_element_type=F32)

    @pl.when(j < n_gelu)
    def _():
        p_ref[...] = _gelu(project()).astype(BF16)

    @pl.when((j >= n_gelu) & (j < n_gelu + n_silu))
    def _():
        p_ref[...] = _silu(project()).astype(BF16)

    @pl.when(j >= n_gelu + n_silu)
    def _():
        x = project()
        jc = j - (n_gelu + n_silu)
        ext = jnp.concatenate([halo_ref[jc], x], axis=0)
        acc = cb_ref[...] + cw_ref[SSD_CONV - 1:SSD_CONV, :] * ext
        for s in range(1, SSD_CONV):
            acc = acc + cw_ref[SSD_CONV - 1 - s:SSD_CONV - s, :] * pltpu.roll(ext, s, axis=0)
        p_ref[...] = _silu(acc[tail:, :]).astype(BF16)
        halo_ref[jc] = x[tm - tail:tm, :]


def _inproj(x, g, w_t, w_dt, conv_w, conv_b, n_gelu_cols, n_silu_cols, row_gelu):
    t_len, d = x.shape
    n_conv_cols = conv_b.shape[0]
    n_out = n_gelu_cols + n_silu_cols + n_conv_cols
    assert row_gelu + n_gelu_cols == w_t.shape[0] and row_gelu >= n_silu_cols + n_conv_cols
    tm = _largest_tile(t_len, 1024, V7X_SUBLANES)
    tn = _largest_tile(_gcd(_gcd(n_gelu_cols, n_silu_cols), n_conv_cols), 512)
    n_gelu, n_silu, n_conv = n_gelu_cols // tn, n_silu_cols // tn, n_conv_cols // tn
    tail = 2 * V7X_SUBLANES
    assert row_gelu % tail == 0 and tn % tail == 0
    conv_map = lambda i, j: (0, jnp.maximum(j - (n_gelu + n_silu), 0))
    w_map = lambda i, j: (
        tail * jnp.where(j < n_gelu, row_gelu // tail + j * (tn // tail), (j - n_gelu) * (tn // tail)), 0)
    return pl.pallas_call(
        functools.partial(_inproj_kernel, n_gelu=n_gelu, n_silu=n_silu),
        grid=(t_len // tm, n_out // tn),
        in_specs=[
            pl.BlockSpec(memory_space=pl.ANY),
            pl.BlockSpec((1, d), lambda i, j: (0, 0)),
            pl.BlockSpec((pl.Element(tn), pl.Element(d)), w_map),
            pl.BlockSpec((V7X_LANES, d), lambda i, j: (0, 0)),
            pl.BlockSpec((SSD_CONV, tn), conv_map),
            pl.BlockSpec((1, tn), conv_map),
        ],
        out_specs=[
            pl.BlockSpec((tm, tn), lambda i, j: (i, j)),
            pl.BlockSpec((tm, V7X_LANES), lambda i, j: (i, 0)),
        ],
        out_shape=[
            jax.ShapeDtypeStruct((t_len, n_out), BF16),
            jax.ShapeDtypeStruct((t_len, V7X_LANES), F32),
        ],
        scratch_shapes=[
            pltpu.VMEM((tm, d), F32),
            pltpu.VMEM((tm, d), BF16),
            pltpu.VMEM((n_conv, tail, tn), F32),
            pltpu.SemaphoreType.DMA((1,)),
        ],
        compiler_params=_params(("arbitrary", "arbitrary")),
        name="inproj",
    )(x, g.reshape(1, d), w_t, w_dt, conv_w, conv_b.reshape(1, n_conv_cols))


def _split3(x):
    hi = x.astype(BF16)
    r1 = x - hi.astype(F32)
    mid = r1.astype(BF16)
    lo = (r1 - mid.astype(F32)).astype(BF16)
    return hi, mid, lo


def _ssd_kernel(sz_ref, xs_ref, b_ref, c_ref, dt_ref, dtb_ref, alog_ref, dsk_ref, ng_ref, y_ref, state_ref,
                *, hpg, hd):
    width = hpg * hd
    lanes = V7X_LANES
    per = lanes // hd

    @pl.when(pl.program_id(1) == 0)
    def _():
        state_ref[...] = jnp.zeros(state_ref.shape, F32)

    ri = lax.broadcasted_iota(I32, (CHUNK, CHUNK), 0)
    ci = lax.broadcasted_iota(I32, (CHUNK, CHUNK), 1)
    causal = ri >= ci
    tril = causal.astype(BF16)
    neg_a = -jnp.exp(alog_ref[...])

    def expand(q):
        cols = []
        for j in range(width // lanes):
            blk = jnp.broadcast_to(q[:, j * per:j * per + 1], (CHUNK, lanes))
            for t in range(1, per):
                nxt = jnp.broadcast_to(q[:, j * per + t:j * per + t + 1], (CHUNK, lanes))
                blk = jnp.where(ci >= t * hd, nxt, blk)
            cols.append(blk)
        return jnp.concatenate(cols, axis=1)

    def chunk(k, carry):
        rows = pl.ds(pl.multiple_of(k * CHUNK, CHUNK), CHUNK)
        xs = xs_ref[rows, :].astype(F32)
        bm = b_ref[rows, :]
        cm = c_ref[rows, :]

        dt = _softplus(dt_ref[rows, :] + dtb_ref[...])
        a_dt = dt * neg_a
        acum = sum(jnp.dot(tril, piece, preferred_element_type=F32) for piece in _split3(a_dt))
        acum_t = acum.T
        acum_f = expand(acum)
        last = acum_f[CHUNK - 1:CHUNK, :]
        xdt = xs * expand(dt)

        cb = lax.dot_general(cm, bm, (((1,), (1,)), ((), ())), preferred_element_type=F32)
        y_cols = []
        for j in range(width // lanes):
            x_col = xdt[:, j * lanes:(j + 1) * lanes]
            y_col = jnp.zeros((CHUNK, lanes), F32)
            for t in range(per):
                h = j * per + t
                seg = acum[:, h:h + 1] - acum_t[h:h + 1, :]
                lmat = jnp.exp(jnp.where(causal, seg, -jnp.inf))
                m_h = (cb * lmat).astype(BF16)
                mask = (ci >= t * hd) & (ci < (t + 1) * hd)
                x_h = jnp.where(mask, x_col, 0.0).astype(BF16)
                y_col = y_col + jnp.dot(m_h, x_h, preferred_element_type=F32)
            y_cols.append(y_col)
        y_diag = jnp.concatenate(y_cols, axis=1)

        state = state_ref[...]
        y_off = jnp.dot(cm, state.astype(BF16), preferred_element_type=F32) * jnp.exp(acum_f)
        x_dec = (xdt * jnp.exp(last - acum_f)).astype(BF16)
        s_new = lax.dot_general(bm, x_dec, (((0,), (0,)), ((), ())), preferred_element_type=F32)
        state_ref[...] = state * jnp.exp(last) + s_new

        yg = (y_diag + y_off + xs * dsk_ref[...]) * sz_ref[rows, :].astype(F32)
        yn = yg * lax.rsqrt(jnp.mean(yg * yg, axis=-1, keepdims=True) + EPS) * ng_ref[...]
        y_ref[rows, :] = yn.astype(BF16)
        return carry

    lax.fori_loop(0, y_ref.shape[0] // CHUNK, chunk, 0, unroll=8)


def _ssd(p, dt_raw, dt_bias, a_log, d_skip, norm_g, groups, off_z, off_xs):
    t_len = p.shape[0]
    heads = dt_bias.shape[0]
    width = norm_g.shape[0]
    hd = width // heads
    hpg = heads // groups
    gw = hpg * hd
    n_st = SSD_STATE
    rows = _largest_tile(t_len, SSD_ROWS, CHUNK)
    assert V7X_LANES % hd == 0 and gw % V7X_LANES == 0 and hpg <= V7X_LANES
    assert width // groups == gw
    off_b = off_xs + width
    off_c = off_b + groups * n_st
    assert off_z % gw == 0 and off_xs % gw == 0 and off_b % n_st == 0

    def per_group_lanes(v):
        v = v.reshape(groups, 1, hpg)
        return jnp.pad(v, ((0, 0), (0, 0), (0, V7X_LANES - hpg)))

    dt_g = dt_raw[:, :heads].reshape(t_len, groups, hpg).transpose(1, 0, 2)
    dt_g = jnp.pad(dt_g, ((0, 0), (0, 0), (0, V7X_LANES - hpg)))

    grp = lambda g, c: (g, 0, 0)
    return pl.pallas_call(
        functools.partial(_ssd_kernel, hpg=hpg, hd=hd),
        grid=(groups, t_len // rows),
        in_specs=[
            pl.BlockSpec((rows, gw), lambda g, c: (c, off_z // gw + g)),
            pl.BlockSpec((rows, gw), lambda g, c: (c, off_xs // gw + g)),
            pl.BlockSpec((rows, n_st), lambda g, c: (c, off_b // n_st + g)),
            pl.BlockSpec((rows, n_st), lambda g, c: (c, off_c // n_st + g)),
            pl.BlockSpec((None, rows, V7X_LANES), lambda g, c: (g, c, 0)),
            pl.BlockSpec((None, 1, V7X_LANES), grp),
            pl.BlockSpec((None, 1, V7X_LANES), grp),
            pl.BlockSpec((1, gw), lambda g, c: (0, g)),
            pl.BlockSpec((1, gw), lambda g, c: (0, g)),
        ],
        out_specs=pl.BlockSpec((rows, gw), lambda g, c: (c, g)),
        out_shape=jax.ShapeDtypeStruct((t_len, width), BF16),
        scratch_shapes=[pltpu.VMEM((n_st, gw), F32)],
        compiler_params=_params(("arbitrary", "arbitrary")),
        name="ssd",
    )(p, p, p, p, dt_g, per_group_lanes(dt_bias), per_group_lanes(a_log),
      jnp.repeat(d_skip, hd).reshape(1, width), norm_g.reshape(1, width))


def _row_sums(x):
    ones = jnp.ones((x.shape[1], V7X_LANES), BF16)
    hi = x.astype(BF16)
    lo = (x - hi.astype(F32)).astype(BF16)
    return (jnp.dot(hi, ones, preferred_element_type=F32) + jnp.dot(lo, ones, preferred_element_type=F32))


def _gmlp_kernel(u_ref, v_ref, vg_ref, ws_ref, bs_ref, og_ref, y_ref, yscr_ref, *, heads, hd):
    lanes = V7X_LANES
    reps = hd // lanes
    ri = lax.broadcasted_iota(I32, (CHUNK, CHUNK), 0)
    ci = lax.broadcasted_iota(I32, (CHUNK, CHUNK), 1)
    causal = ri >= ci
    ssq = jnp.zeros((CHUNK, lanes), F32)
    invs = []
    for h in range(heads):
        v_h = v_ref[:, h * hd:(h + 1) * hd].astype(F32)
        invs.append(lax.rsqrt(_row_sums(v_h * v_h) / hd + EPS))
    for h in range(heads):
        cols = slice(h * hd, (h + 1) * hd)
        u_h = u_ref[:, cols].astype(F32)
        v_h = v_ref[:, cols].astype(F32)
        v_n = v_h * jnp.tile(invs[h], (1, reps)) * vg_ref[:, cols]
        w_h = jnp.where(causal, ws_ref[h], 0.0).astype(BF16)
        sv = jnp.dot(w_h, v_n.astype(BF16), preferred_element_type=F32) + jnp.tile(bs_ref[h], (1, reps))
        y_h = u_h * sv
        yscr_ref[:, cols] = y_h
        y_sq = y_h * y_h
        for r in range(reps):
            ssq = ssq + y_sq[:, r * lanes:(r + 1) * lanes]
    scale = lax.rsqrt(_row_sums(ssq) / (heads * hd) + EPS)
    y_ref[...] = (yscr_ref[...] * jnp.tile(scale, (1, heads * reps)) * og_ref[...]).astype(BF16)


def _gmlp(p, v_norm_g, w_s, b_s, out_norm_g, off_u):
    t_len = p.shape[0]
    heads = w_s.shape[0]
    width = out_norm_g.shape[0]
    hd = width // heads
    assert w_s.shape[1] == CHUNK and off_u % width == 0 and hd % V7X_LANES == 0
    b_lanes = jnp.broadcast_to(b_s[:, :, None], (heads, CHUNK, V7X_LANES))
    return pl.pallas_call(
        functools.partial(_gmlp_kernel, heads=heads, hd=hd),
        grid=(t_len // CHUNK,),
        in_specs=[
            pl.BlockSpec((CHUNK, width), lambda c: (c, off_u // width)),
            pl.BlockSpec((CHUNK, width), lambda c: (c, off_u // width + 1)),
            pl.BlockSpec((1, width), lambda c: (0, 0)),
            pl.BlockSpec((heads, CHUNK, CHUNK), lambda c: (0, 0, 0)),
            pl.BlockSpec((heads, CHUNK, V7X_LANES), lambda c: (0, 0, 0)),
            pl.BlockSpec((1, width), lambda c: (0, 0)),
        ],
        out_specs=pl.BlockSpec((CHUNK, width), lambda c: (c, 0)),
        out_shape=jax.ShapeDtypeStruct((t_len, width), BF16),
        scratch_shapes=[pltpu.VMEM((CHUNK, width), F32)],
        compiler_params=_params(("arbitrary",)),
        name="gmlp",
    )(p, p, v_norm_g.reshape(1, width), w_s, b_lanes, out_norm_g.reshape(1, width))


def _outproj_kernel(ys_ref, yg_ref, wa_ref, wb_ref, x_ref, o_ref):
    acc = jnp.dot(ys_ref[...], wa_ref[...], preferred_element_type=F32)
    acc = acc + jnp.dot(yg_ref[...], wb_ref[...], preferred_element_type=F32)
    o_ref[...] = x_ref[...] + acc


def _outproj(y_ssd, y_gmlp, w_out, x):
    t_len, d = x.shape
    ka, kb = y_ssd.shape[1], y_gmlp.shape[1]
    assert ka == kb and w_out.shape[0] == ka + kb
    tm = _largest_tile(t_len, 512, V7X_SUBLANES)
    tn = _largest_tile(d, 512)
    return pl.pallas_call(
        _outproj_kernel,
        grid=(t_len // tm, d // tn),
        in_specs=[
            pl.BlockSpec((tm, ka), lambda i, j: (i, 0)),
            pl.BlockSpec((tm, kb), lambda i, j: (i, 0)),
            pl.BlockSpec((ka, tn), lambda i, j: (0, j)),
            pl.BlockSpec((kb, tn), lambda i, j: (1, j)),
            pl.BlockSpec((tm, tn), lambda i, j: (i, j)),
        ],
        out_specs=pl.BlockSpec((tm, tn), lambda i, j: (i, j)),
        out_shape=jax.ShapeDtypeStruct((t_len, d), F32),
        compiler_params=_params(("arbitrary", "arbitrary")),
        name="outproj",
    )(y_ssd, y_gmlp, w_out, w_out, x)


def _router_kernel(x_ref, g_ref, wr_ref, br_ref, xp_ref, idx_ref, gate_ref, *, n_exp):
    x = x_ref[...]
    tm, d = x.shape
    half = d // 2
    ms = jnp.mean(x * x, axis=-1, keepdims=True)
    xn = x * lax.rsqrt(ms + EPS) * g_ref[...]
    xp_ref[...] = _pack_pair(xn[:, :half], xn[:, half:])

    logits = jnp.dot(xn.astype(BF16), wr_ref[...], preferred_element_type=F32) + br_ref[...]
    lane = lax.broadcasted_iota(I32, (tm, V7X_LANES), 1)
    vals = jnp.where(lane < n_exp, logits, -jnp.inf)
    idx_out = jnp.zeros((tm, V7X_LANES), I32)
    val_out = jnp.full((tm, V7X_LANES), -jnp.inf, F32)
    for k in range(TOP_K):
        m = jnp.max(vals, axis=-1, keepdims=True)
        sel = jnp.min(jnp.where(vals == m, lane, V7X_LANES), axis=-1, keepdims=True)
        idx_out = jnp.where(lane == k, sel, idx_out)
        val_out = jnp.where(lane == k, m, val_out)
        vals = jnp.where(lane == sel, -jnp.inf, vals)
    e = jnp.exp(val_out - jnp.max(val_out, axis=-1, keepdims=True))
    idx_ref[...] = idx_out
    gate_ref[...] = e / jnp.sum(e, axis=-1, keepdims=True)


def _router(x1, g, w_router, b_router):
    t_len, d = x1.shape
    n_exp = w_router.shape[1]
    assert n_exp <= V7X_LANES and TOP_K <= n_exp
    tm = _largest_tile(t_len, 256, V7X_SUBLANES)
    wr = jnp.pad(w_router, ((0, 0), (0, V7X_LANES - n_exp))).astype(BF16)
    br = jnp.pad(b_router, (0, V7X_LANES - n_exp)).reshape(1, V7X_LANES)
    return pl.pallas_call(
        functools.partial(_router_kernel, n_exp=n_exp),
        grid=(t_len // tm,),
        in_specs=[
            pl.BlockSpec((tm, d), lambda i: (i, 0)),
            pl.BlockSpec((1, d), lambda i: (0, 0)),
            pl.BlockSpec((d, V7X_LANES), lambda i: (0, 0)),
            pl.BlockSpec((1, V7X_LANES), lambda i: (0, 0)),
        ],
        out_specs=[
            pl.BlockSpec((tm, d // 2), lambda i: (i, 0)),
            pl.BlockSpec((tm, V7X_LANES), lambda i: (i, 0)),
            pl.BlockSpec((tm, V7X_LANES), lambda i: (i, 0)),
        ],
        out_shape=[
            jax.ShapeDtypeStruct((t_len, d // 2), U32),
            jax.ShapeDtypeStruct((t_len, V7X_LANES), I32),
            jax.ShapeDtypeStruct((t_len, V7X_LANES), F32),
        ],
        compiler_params=_params(("arbitrary",)),
        name="router",
    )(x1, g.reshape(1, d), wr, br)


def _row_tokens_kernel(dest_ref, tok_ref, *, top_k):
    per = ISSUE_UNROLL // top_k

    def clear(q, carry):
        for u in range(ISSUE_UNROLL):
            tok_ref[q * ISSUE_UNROLL + u] = 0
        return carry

    def place(q, carry):
        for u in range(ISSUE_UNROLL):
            tok_ref[dest_ref[q * ISSUE_UNROLL + u]] = q * per + u // top_k
        return carry

    lax.fori_loop(0, tok_ref.shape[0] // ISSUE_UNROLL, clear, 0)
    lax.fori_loop(0, dest_ref.shape[0] // ISSUE_UNROLL, place, 0)


def _row_tokens(dest, n_rows, top_k):
    assert ISSUE_UNROLL % top_k == 0 and n_rows % ISSUE_UNROLL == 0 and dest.shape[0] % ISSUE_UNROLL == 0
    return pl.pallas_call(
        functools.partial(_row_tokens_kernel, top_k=top_k),
        in_specs=[pl.BlockSpec(memory_space=pltpu.SMEM)],
        out_specs=pl.BlockSpec(memory_space=pltpu.SMEM),
        out_shape=jax.ShapeDtypeStruct((n_rows,), I32),
        name="row_tokens",
    )(dest)


def _route_plan(top_idx, n_exp):
    t_len, top_k = top_idx.shape
    n_assign = t_len * top_k
    e = top_idx.reshape(n_assign)
    onehot = (e[:, None] == jnp.arange(n_exp, dtype=I32)[None, :]).astype(I32)
    csum = jnp.cumsum(onehot, axis=0)
    rank = jnp.sum(onehot * csum, axis=1) - 1
    counts = csum[-1]
    nb = (counts + MOE_BLK - 1) // MOE_BLK
    sb_end = jnp.cumsum(nb)
    sb_start = sb_end - nb
    dest = sb_start[e] * MOE_BLK + rank
    n_sb = n_assign // MOE_BLK + n_exp
    n_rows = n_sb * MOE_BLK
    row_tok = _row_tokens(dest.astype(I32), n_rows, top_k)

    nv = (nb + MOE_NSB - 1) // MOE_NSB
    v_end = jnp.cumsum(nv)
    v_start = v_end - nv
    n_pass = -(-n_sb // MOE_NSB) + n_exp
    vid = jnp.arange(n_pass, dtype=I32)
    ve = jnp.minimum(jnp.searchsorted(v_end, vid, side='right'), n_exp - 1).astype(I32)
    q = vid - v_start[ve]
    valid = vid < v_end[-1]
    vs = jnp.where(valid, sb_start[ve] + q * MOE_NSB, 0).astype(I32)
    vn = jnp.where(valid, jnp.clip(nb[ve] - q * MOE_NSB, 0, MOE_NSB), 0).astype(I32)
    last_e = ve[jnp.maximum(v_end[-1] - 1, 0)]
    ve = jnp.where(valid, ve, last_e).astype(I32)
    used = sb_end[-1:].astype(I32)
    n_live = jnp.maximum(v_end[-1], 1).astype(I32)
    return dict(row_tok=row_tok, dest=dest.astype(I32), ve=ve, vs=vs, vn=vn, used=used, n_rows=n_rows,
                n_pass=n_live)


def _for_each_block(n, compute, emit):
    def runs(b, count):
        prev = compute(b, 2)
        for t in range(1, count):
            cur = compute(b + 2 * t, 2)
            emit(b + 2 * (t - 1), 2, prev)
            prev = cur
        emit(b + 2 * (count - 1), 2, prev)

    octs = lax.shift_right_logical(n, 3)
    lax.fori_loop(0, octs, lambda q, carry: (runs(8 * q, 4), carry)[1], 0)

    @pl.when((n & 4) != 0)
    def _():
        runs(8 * octs, 2)

    @pl.when((n & 2) != 0)
    def _():
        runs(8 * octs + (n & 4), 1)

    @pl.when((n & 1) != 0)
    def _():
        emit(n - 1, 1, compute(n - 1, 1))


def _zero_tail(obuf, n_used, n_sb, n_j, out_copy):
    tile = obuf.shape[2]
    obuf[0, 0:MOE_BLK, :] = jnp.zeros((MOE_BLK, tile), obuf.dtype)

    def start(sb, carry):
        for jj in range(n_j):
            out_copy(0, 0, pl.multiple_of(sb * MOE_BLK, MOE_BLK), jj * tile).start()
        return carry

    def wait(sb, carry):
        for jj in range(n_j):
            out_copy(0, 0, 0, 0).wait()
        return carry

    lax.fori_loop(n_used, n_sb, start, 0)
    lax.fori_loop(n_used, n_sb, wait, 0)


def _moe_up_kernel(ve_ref, vs_ref, vn_ref, used_ref, tok_ref, xp_hbm, wg_ref, wl_ref, bg_ref, bl_ref, act_hbm,
                   xbuf, obuf, xsem, osem, ocnt, *, tf, n_j):
    v = pl.program_id(0)
    j = pl.program_id(1)
    n_pass = pl.num_programs(0)
    half = xbuf.shape[2]
    slot = v % 2
    n_sub = vn_ref[v]

    def row_copy(tok, sl, r0, u):
        window = xbuf.at[sl, pl.ds(r0, ISSUE_UNROLL), :]
        return pltpu.make_async_copy(xp_hbm.at[pl.ds(tok, 1), :], window.at[pl.ds(u, 1), :], xsem.at[sl])

    def gather_start(vv, sl):
        base = vs_ref[vv] * MOE_BLK

        def body(q, carry):
            r0 = pl.multiple_of(q * ISSUE_UNROLL, ISSUE_UNROLL)
            for u in range(ISSUE_UNROLL):
                row_copy(tok_ref[base + r0 + u], sl, r0, u).start()
            return carry

        lax.fori_loop(0, vn_ref[vv] * (MOE_BLK // ISSUE_UNROLL), body, 0)

    def gather_wait(vv, sl):
        def body(i, carry):
            r0 = pl.multiple_of(i * MOE_BLK, MOE_BLK)
            pltpu.make_async_copy(xp_hbm.at[pl.ds(0, MOE_BLK), :], xbuf.at[sl, pl.ds(r0, MOE_BLK), :],
                                  xsem.at[sl]).wait()
            return carry

        lax.fori_loop(0, vn_ref[vv], body, 0)

    def out_copy(p, r0, row, col):
        return pltpu.make_async_copy(obuf.at[p, pl.ds(r0, MOE_BLK), :],
                                     act_hbm.at[pl.ds(row, MOE_BLK), pl.ds(col, tf)], osem.at[p])

    def drain(p):
        def body(i, carry):
            out_copy(p, 0, 0, 0).wait()
            return carry

        lax.fori_loop(0, ocnt[p], body, 0)
        ocnt[p] = 0

    @pl.when(j == 0)
    def _():
        @pl.when(v == 0)
        def _():
            ocnt[0] = 0
            ocnt[1] = 0
            gather_start(0, 0)

        gather_wait(v, slot)

    step = v * n_j + j
    p = step % 2
    drain(p)

    @pl.when(n_sub > 0)
    def _():
        nxt = jnp.minimum(v + 1, n_pass - 1)
        n_next = jnp.where(v + 1 < n_pass, vn_ref[nxt], 0)
        groups = n_next * (MOE_BLK // (ISSUE_UNROLL * n_j))
        first = j * groups
        base = vs_ref[nxt] * MOE_BLK

        def request(q, carry):
            r0 = pl.multiple_of((first + q) * ISSUE_UNROLL, ISSUE_UNROLL)
            for u in range(ISSUE_UNROLL):
                row_copy(tok_ref[base + r0 + u], 1 - slot, r0, u).start()
            return carry

        lax.fori_loop(0, groups, request, 0)

        def compute(b, nb):
            r0 = pl.multiple_of(b * MOE_BLK, MOE_BLK)
            lo, hi = _unpack_pair(xbuf[slot, pl.ds(r0, nb * MOE_BLK), :])
            w_lo = jnp.concatenate([wg_ref[0:half, :], wl_ref[0:half, :]], axis=1).astype(BF16)
            w_hi = jnp.concatenate([wg_ref[half:2 * half, :], wl_ref[half:2 * half, :]], axis=1).astype(BF16)
            h = jnp.dot(lo.astype(BF16), w_lo, preferred_element_type=F32)
            h = h + jnp.dot(hi.astype(BF16), w_hi, preferred_element_type=F32)
            glu = jnp.minimum(h[:, 0:tf] + bg_ref[...], SWIGLU_LIMIT)
            lin = jnp.clip(h[:, tf:2 * tf] + bl_ref[...], -SWIGLU_LIMIT, SWIGLU_LIMIT)
            return (glu * jax.nn.sigmoid(SWIGLU_ALPHA * glu) * (lin + 1.0)).astype(BF16)

        def emit(b, nb, act):
            r0 = pl.multiple_of(b * MOE_BLK, MOE_BLK)
            obuf[p, pl.ds(r0, nb * MOE_BLK), :] = act
            for t in range(nb):
                row = pl.multiple_of((vs_ref[v] + b + t) * MOE_BLK, MOE_BLK)
                out_copy(p, r0 + t * MOE_BLK, row, pl.multiple_of(j * tf, tf)).start()

        _for_each_block(n_sub, compute, emit)
        ocnt[p] = n_sub

    @pl.when(step == n_pass * n_j - 1)
    def _():
        drain(0)
        drain(1)
        _zero_tail(obuf, used_ref[0], act_hbm.shape[0] // MOE_BLK, n_j, out_copy)


def _moe_up(xp, plan, w_gate_up, b_gate_up):
    n_exp, d, two_f = w_gate_up.shape
    f = two_f // 2
    half = xp.shape[1]
    assert half * 2 == d
    tf = _largest_tile(f, 256)
    n_j = f // tf
    rows_pass = MOE_NSB * MOE_BLK
    assert MOE_BLK % (ISSUE_UNROLL * n_j) == 0

    def w_map(off):
        def index_map(v, j, ve, vs, vn, used, tok):
            jj = jnp.where(vn[v] > 0, j, n_j - 1)
            return (ve[v], 0, off + jj)
        return index_map

    grid_spec = pltpu.PrefetchScalarGridSpec(
        num_scalar_prefetch=5,
        grid=(plan["n_pass"], n_j),
        in_specs=[
            pl.BlockSpec(memory_space=pl.ANY),
            pl.BlockSpec((None, d, tf), w_map(0)),
            pl.BlockSpec((None, d, tf), w_map(n_j)),
            pl.BlockSpec((None, 1, tf), w_map(0)),
            pl.BlockSpec((None, 1, tf), w_map(n_j)),
        ],
        out_specs=pl.BlockSpec(memory_space=pl.ANY),
        scratch_shapes=[
            pltpu.VMEM((2, rows_pass, half), U32),
            pltpu.VMEM((2, rows_pass, tf), BF16),
            pltpu.SemaphoreType.DMA((2,)),
            pltpu.SemaphoreType.DMA((2,)),
            pltpu.SMEM((2,), I32),
        ],
    )
    b3 = b_gate_up.reshape(n_exp, 1, two_f)
    return pl.pallas_call(
        functools.partial(_moe_up_kernel, tf=tf, n_j=n_j),
        grid_spec=grid_spec,
        out_shape=jax.ShapeDtypeStruct((plan["n_rows"], f), BF16),
        compiler_params=_params(("arbitrary", "arbitrary")),
        name="moe_up",
    )(plan["ve"], plan["vs"], plan["vn"], plan["used"], plan["row_tok"], xp, w_gate_up, w_gate_up, b3, b3)


def _moe_down_kernel(ve_ref, vs_ref, vn_ref, used_ref, act_hbm, wa_ref, wb_ref, ba_ref, bb_ref, y_hbm,
                     abuf, obuf, asem, osem, ocnt, *, tn):
    v = pl.program_id(0)
    j = pl.program_id(1)
    n_pass = pl.num_programs(0)
    n_j = pl.num_programs(1)
    slot = v % 2
    n_sub = vn_ref[v]

    def in_copy(row, sl, r0):
        return pltpu.make_async_copy(act_hbm.at[pl.ds(row, MOE_BLK), :],
                                     abuf.at[sl, pl.ds(r0, MOE_BLK), :], asem.at[sl])

    def load_start(vv, sl):
        def body(i, carry):
            r0 = pl.multiple_of(i * MOE_BLK, MOE_BLK)
            in_copy(pl.multiple_of((vs_ref[vv] + i) * MOE_BLK, MOE_BLK), sl, r0).start()
            return carry

        lax.fori_loop(0, vn_ref[vv], body, 0)

    def load_wait(vv, sl):
        def body(i, carry):
            in_copy(0, sl, pl.multiple_of(i * MOE_BLK, MOE_BLK)).wait()
            return carry

        lax.fori_loop(0, vn_ref[vv], body, 0)

    def out_copy(p, r0, row, col):
        return pltpu.make_async_copy(obuf.at[p, pl.ds(r0, MOE_BLK), :],
                                     y_hbm.at[pl.ds(row, MOE_BLK), pl.ds(col, tn)], osem.at[p])

    def drain(p):
        def body(i, carry):
            out_copy(p, 0, 0, 0).wait()
            return carry

        lax.fori_loop(0, ocnt[p], body, 0)
        ocnt[p] = 0

    @pl.when(j == 0)
    def _():
        @pl.when(v == 0)
        def _():
            ocnt[0] = 0
            ocnt[1] = 0
            load_start(0, 0)

        @pl.when(v + 1 < n_pass)
        def _():
            load_start(v + 1, 1 - slot)

        load_wait(v, slot)

    step = v * n_j + j
    p = step % 2
    drain(p)

    @pl.when(n_sub > 0)
    def _():
        def compute(b, nb):
            r0 = pl.multiple_of(b * MOE_BLK, MOE_BLK)
            w = jnp.concatenate([wa_ref[...], wb_ref[...]], axis=1).astype(BF16)
            y = jnp.dot(abuf[slot, pl.ds(r0, nb * MOE_BLK), :], w, preferred_element_type=F32)
            return _pack_pair(y[:, 0:tn] + ba_ref[...], y[:, tn:2 * tn] + bb_ref[...])

        def emit(b, nb, packed):
            r0 = pl.multiple_of(b * MOE_BLK, MOE_BLK)
            obuf[p, pl.ds(r0, nb * MOE_BLK), :] = packed
            for t in range(nb):
                row = pl.multiple_of((vs_ref[v] + b + t) * MOE_BLK, MOE_BLK)
                out_copy(p, r0 + t * MOE_BLK, row, pl.multiple_of(j * tn, tn)).start()

        _for_each_block(n_sub, compute, emit)
        ocnt[p] = n_sub

    @pl.when(step == n_pass * n_j - 1)
    def _():
        drain(0)
        drain(1)
        _zero_tail(obuf, used_ref[0], y_hbm.shape[0] // MOE_BLK, n_j, out_copy)


def _moe_down(act, plan, w_down, b_down):
    n_exp, f, d = w_down.shape
    half = d // 2
    tn = _largest_tile(half, 512)
    n_j = half // tn
    rows_pass = MOE_NSB * MOE_BLK

    def w_map(off):
        def index_map(v, j, ve, vs, vn, used):
            jj = jnp.where(vn[v] > 0, j, n_j - 1)
            return (ve[v], 0, off + jj)
        return index_map

    grid_spec = pltpu.PrefetchScalarGridSpec(
        num_scalar_prefetch=4,
        grid=(plan["n_pass"], n_j),
        in_specs=[
            pl.BlockSpec(memory_space=pl.ANY),
            pl.BlockSpec((None, f, tn), w_map(0)),
            pl.BlockSpec((None, f, tn), w_map(n_j)),
            pl.BlockSpec((None, 1, tn), w_map(0)),
            pl.BlockSpec((None, 1, tn), w_map(n_j)),
        ],
        out_specs=pl.BlockSpec(memory_space=pl.ANY),
        scratch_shapes=[
            pltpu.VMEM((2, rows_pass, f), BF16),
            pltpu.VMEM((2, rows_pass, tn), U32),
            pltpu.SemaphoreType.DMA((2,)),
            pltpu.SemaphoreType.DMA((2,)),
            pltpu.SMEM((2,), I32),
        ],
    )
    b3 = b_down.reshape(n_exp, 1, d)
    return pl.pallas_call(
        functools.partial(_moe_down_kernel, tn=tn),
        grid_spec=grid_spec,
        out_shape=jax.ShapeDtypeStruct((plan["n_rows"], half), U32),
        compiler_params=_params(("arbitrary", "arbitrary")),
        name="moe_down",
    )(plan["ve"], plan["vs"], plan["vn"], plan["used"], act, w_down, w_down, b3, b3)


def _combine_kernel(dest_ref, y_hbm, gate_ref, x_ref, g_ref, o_ref, ybuf, sem):
    i = pl.program_id(0)
    n_i = pl.num_programs(0)
    tm, d = x_ref.shape
    half = d // 2
    slot = i % 2

    def row_copy(row, sl, k, t0, u):
        window = ybuf.at[sl, k, pl.ds(t0, ISSUE_UNROLL), :]
        return pltpu.make_async_copy(y_hbm.at[pl.ds(row, 1), :], window.at[pl.ds(u, 1), :], sem.at[sl])

    def request(ii, sl, q):
        t0 = pl.multiple_of(q * ISSUE_UNROLL, ISSUE_UNROLL)
        for u in range(ISSUE_UNROLL):
            for k in range(TOP_K):
                row_copy(dest_ref[(ii * tm + t0 + u) * TOP_K + k], sl, k, t0, u).start()

    def gather_wait(sl):
        for k in range(TOP_K):
            pltpu.make_async_copy(y_hbm.at[pl.ds(0, tm), :], ybuf.at[sl, k], sem.at[sl]).wait()

    def combine(q, prefetch):
        rows = pl.ds(pl.multiple_of(q * ISSUE_UNROLL, ISSUE_UNROLL), ISSUE_UNROLL)
        acc_lo = x_ref[rows, 0:half]
        acc_hi = x_ref[rows, half:d]
        for k in range(TOP_K):
            lo, hi = _unpack_pair(ybuf[slot, k, rows, :])
            gk = gate_ref[rows, k:k + 1]
            acc_lo = acc_lo + gk * lo
            acc_hi = acc_hi + gk * hi
        if prefetch:
            request(i + 1, 1 - slot, q)
        ms = (jnp.sum(acc_lo * acc_lo, axis=-1, keepdims=True)
              + jnp.sum(acc_hi * acc_hi, axis=-1, keepdims=True)) / d
        r = lax.rsqrt(ms + EPS)
        o_ref[rows, 0:half] = acc_lo * r * g_ref[:, 0:half]
        o_ref[rows, half:d] = acc_hi * r * g_ref[:, half:d]

    n_groups = tm // ISSUE_UNROLL

    @pl.when(i == 0)
    def _():
        lax.fori_loop(0, n_groups, lambda q, c: (request(0, 0, q), c)[1], 0)

    gather_wait(slot)

    @pl.when(i + 1 < n_i)
    def _():
        lax.fori_loop(0, n_groups, lambda q, c: (combine(q, True), c)[1], 0, unroll=2)

    @pl.when(i + 1 == n_i)
    def _():
        lax.fori_loop(0, n_groups, lambda q, c: (combine(q, False), c)[1], 0, unroll=2)


def _combine(y_rows, dest, gate, x1, final_g):
    t_len, d = x1.shape
    half = d // 2
    tm = _largest_tile(t_len, 128, V7X_SUBLANES)
    grid_spec = pltpu.PrefetchScalarGridSpec(
        num_scalar_prefetch=1,
        grid=(t_len // tm,),
        in_specs=[
            pl.BlockSpec(memory_space=pl.ANY),
            pl.BlockSpec((tm, V7X_LANES), lambda i, dest: (i, 0)),
            pl.BlockSpec((tm, d), lambda i, dest: (i, 0)),
            pl.BlockSpec((1, d), lambda i, dest: (0, 0)),
        ],
        out_specs=pl.BlockSpec((tm, d), lambda i, dest: (i, 0)),
        scratch_shapes=[
            pltpu.VMEM((2, TOP_K, tm, half), U32),
            pltpu.SemaphoreType.DMA((2,)),
        ],
    )
    return pl.pallas_call(
        _combine_kernel,
        grid_spec=grid_spec,
        out_shape=jax.ShapeDtypeStruct((t_len, d), F32),
        compiler_params=_params(("arbitrary",)),
        name="combine",
    )(dest, y_rows, gate, x1, final_g.reshape(1, d))


def kernel(x, mix_norm_g, w_in, ssd_conv_w, ssd_conv_b, ssd_dt_bias, ssd_a_log, ssd_d, ssd_norm_g,
           gmlp_v_norm_g, gmlp_w_s, gmlp_b_s, gmlp_out_norm_g, w_out, ffn_norm_g, w_router, b_router,
           w_gate_up, b_gate_up, w_down, b_down, final_norm_g):
    bsz, t_len, d = x.shape
    assert bsz == 1
    depth = w_in.shape[0]
    ssd_w = ssd_norm_g.shape[-1]
    heads = ssd_dt_bias.shape[-1]
    gmlp_w = gmlp_out_norm_g.shape[-1]
    conv_dim = ssd_conv_b.shape[-1]
    assert heads <= V7X_LANES
    c_dt = ssd_w + conv_dim
    c_uv = c_dt + heads
    off_u, off_z, off_xs = 0, 2 * gmlp_w, 2 * gmlp_w + ssd_w

    h = x.reshape(t_len, d)
    out = None
    for layer in range(depth):
        w = w_in[layer]
        w_t = jnp.swapaxes(w, 0, 1)
        w_dt = jnp.pad(w_t[c_dt:c_uv], ((0, V7X_LANES - heads), (0, 0))).astype(BF16)
        p, dt_raw = _inproj(h, mix_norm_g[layer], w_t, w_dt, ssd_conv_w[layer], ssd_conv_b[layer],
                            2 * gmlp_w, ssd_w, c_uv)
        groups = (conv_dim - ssd_w) // (2 * SSD_STATE)
        y_ssd = _ssd(p, dt_raw, ssd_dt_bias[layer], ssd_a_log[layer], ssd_d[layer], ssd_norm_g[layer],
                     groups, off_z, off_xs)
        y_gmlp = _gmlp(p, gmlp_v_norm_g[layer], gmlp_w_s[layer], gmlp_b_s[layer],
                       gmlp_out_norm_g[layer], off_u)
        x1 = _outproj(y_ssd, y_gmlp, w_out[layer].astype(BF16), h)

        xp, idx_pad, gate_pad = _router(x1, ffn_norm_g[layer], w_router[layer], b_router[layer])
        plan = _route_plan(idx_pad[:, :TOP_K], w_router.shape[-1])
        act = _moe_up(xp, plan, w_gate_up[layer], b_gate_up[layer])
        y_rows = _moe_down(act, plan, w_down[layer], b_down[layer])
        last = layer == depth - 1
        assert last, "multi-layer stacking needs an un-normalised combine"
        out = _combine(y_rows, plan["dest"], gate_pad, x1, final_norm_g)
    return out.reshape(bsz, t_len, d)
```
